```python
import math
import jax, jax.numpy as jnp
from jax import lax
import numpy as np

D_MODEL = 2048
BATCH = 1
SEQ = 8192
DEPTH = 2
DEC_BATCH = 16
DEC_SEQ = 64
PAST_LEN = 4096

CHUNK = 64
N_LEFT_CHUNKS = 8
LEFT_ROWS = N_LEFT_CHUNKS * CHUNK
BAND = (N_LEFT_CHUNKS + 1) * CHUNK
N_HEADS = 16
HEAD_DIM = D_MODEL // N_HEADS
MAX_REL = 256
N_REL = 2 * MAX_REL + 1
CONV_W = 3
D_FF = -(-8 * D_MODEL // (3 * 256)) * 256
N_A = DEPTH // 2
N_B = DEPTH - N_A
ALPHA = (2.0 * DEPTH) ** 0.25
BETA = (8.0 * DEPTH) ** -0.25
LN_EPS = 1e-5

kernel_name = "yoco_shortconv_chunkband_deepnorm_step"


def layer_norm(x, g, b):
    xf = x.astype(jnp.float32)
    mu = jnp.mean(xf, axis=-1, keepdims=True)
    var = jnp.mean(jnp.square(xf - mu), axis=-1, keepdims=True)
    return ((xf - mu) * lax.rsqrt(var + LN_EPS) * g + b).astype(x.dtype)


def post_norm(x, sub, g, b):
    return layer_norm(ALPHA * x + sub, g, b)


def swiglu(x, w_gate_up, w_down):
    g, u = jnp.split(x @ w_gate_up, 2, axis=-1)
    return (jax.nn.silu(g) * u) @ w_down


def short_conv_mixer(x, hist, w_in, conv_w, w_out):
    s = x.shape[1]
    b, c, h = jnp.split(x @ w_in, 3, axis=-1)
    u = c * h
    full = jnp.concatenate([hist.astype(u.dtype), u], axis=1)
    conv = conv_w[0] * full[:, 0:s]
    for t in range(1, CONV_W):
        conv = conv + conv_w[t] * full[:, t:t + s]
    return (b * conv) @ w_out, full[:, -(CONV_W - 1):]


def rel_bias_lookup(rel_bias, rel):
    return rel_bias[:, jnp.clip(rel, -MAX_REL, MAX_REL) + MAX_REL]


def band_attention_prompt(q, k, v, rel_bias):
    n, s = q.shape[:2]
    nc = s // CHUNK
    pad = jnp.zeros((n, LEFT_ROWS, N_HEADS, HEAD_DIM), k.dtype)
    kp = jnp.concatenate([pad, k], axis=1)
    vp = jnp.concatenate([pad.astype(v.dtype), v], axis=1)
    idx = jnp.arange(nc)[:, None] * CHUNK + jnp.arange(BAND)[None, :]
    kb = kp[:, idx]
    vb = vp[:, idx]
    qc = q.reshape(n, nc, CHUNK, N_HEADS, HEAD_DIM)
    rel = jnp.arange(CHUNK)[:, None] - jnp.arange(BAND)[None, :] + LEFT_ROWS
    bias = rel_bias_lookup(rel_bias, rel).astype(jnp.float32)
    valid = (idx - LEFT_ROWS) >= 0
    scores = jnp.einsum('ncqhd,nckhd->nchqk', qc, kb).astype(jnp.float32) * (HEAD_DIM ** -0.5)
    scores = jnp.where(valid[None, :, None, None, :], scores + bias[None, None], -jnp.inf)
    p = jax.nn.softmax(scores, axis=-1).astype(v.dtype)
    out = jnp.einsum('nchqk,nckhd->ncqhd', p, vb)
    return out.reshape(n, s, N_HEADS * HEAD_DIM)


def attention_with_past(q, k_all, v_all, rel_bias, n_past):
    n, s = q.shape[:2]
    rel = (n_past + jnp.arange(s))[:, None] - jnp.arange(n_past + s)[None, :]
    bias = rel_bias_lookup(rel_bias, rel).astype(jnp.float32)
    scores = jnp.einsum('nqhd,nkhd->nhqk', q, k_all).astype(jnp.float32) * (HEAD_DIM ** -0.5)
    p = jax.nn.softmax(scores + bias[None], axis=-1).astype(v_all.dtype)
    out = jnp.einsum('nhqk,nkhd->nqhd', p, v_all)
    return out.reshape(n, s, N_HEADS * HEAD_DIM)


def run_trunk(x, conv_hist, cache_k, cache_v, w_in_a, conv_w, w_out_a, w_kv, w_q, w_o,
              rel_bias, ln_g, ln_b, w_gate_up, w_down):
    n, s, _ = x.shape
    new_conv = []
    k = v = k_all = v_all = None
    for l in range(DEPTH):
        if l < N_A:
            mix, st = short_conv_mixer(x, conv_hist[l], w_in_a[l], conv_w[l], w_out_a[l])
            new_conv.append(st)
        else:
            if l == N_A:
                k, v = jnp.split(x @ w_kv, 2, axis=-1)
                k = k.reshape(n, s, N_HEADS, HEAD_DIM)
                v = v.reshape(n, s, N_HEADS, HEAD_DIM)
                if cache_k is not None:
                    k_all = jnp.concatenate([cache_k.astype(k.dtype), k], axis=1)
                    v_all = jnp.concatenate([cache_v.astype(v.dtype), v], axis=1)
            bl = l - N_A
            q = (x @ w_q[bl]).reshape(n, s, N_HEADS, HEAD_DIM)
            if cache_k is None:
                att = band_attention_prompt(q, k, v, rel_bias[bl])
            else:
                att = attention_with_past(q, k_all, v_all, rel_bias[bl], cache_k.shape[1])
            mix = att @ w_o[bl]
        x = post_norm(x, mix, ln_g[l, 0], ln_b[l, 0])
        x = post_norm(x, swiglu(x, w_gate_up[l], w_down[l]), ln_g[l, 1], ln_b[l, 1])
    return x, jnp.stack(new_conv), k, v


def setup_inputs(seed: int = 0) -> dict:
    key = jax.random.key(seed)
    ks = jax.random.split(key, 20)

    def nrm(k, shape, scale):
        return jax.random.normal(k, shape, jnp.float32) * scale

    r = min(LEFT_ROWS, PAST_LEN)
    d = D_MODEL
    x_prompt = nrm(ks[0], (BATCH, SEQ, d), 1.0)
    x_sample = nrm(ks[1], (DEC_BATCH, DEC_SEQ, d), 1.0)
    state_conv = nrm(ks[2], (N_A, DEC_BATCH, CONV_W - 1, d), 1.0)
    cache_k = nrm(ks[3], (DEC_BATCH, r, N_HEADS, HEAD_DIM), 1.0)
    cache_v = nrm(ks[4], (DEC_BATCH, r, N_HEADS, HEAD_DIM), 1.0)
    w_in_a = nrm(ks[5], (N_A, d, 3 * d), d ** -0.5)
    conv_w = nrm(ks[6], (N_A, CONV_W, d), CONV_W ** -0.5)
    w_out_a = nrm(ks[7], (N_A, d, d), BETA * d ** -0.5)
    w_kv = jnp.concatenate([nrm(ks[8], (d, d), d ** -0.5),
                            nrm(ks[9], (d, d), BETA * d ** -0.5)], axis=1)
    w_q = nrm(ks[10], (N_B, d, d), d ** -0.5)
    w_o = nrm(ks[11], (N_B, d, d), BETA * d ** -0.5)
    rel_bias = nrm(ks[12], (N_B, N_HEADS, N_REL), 0.1)
    ln_g = 1.0 + nrm(ks[13], (DEPTH, 2, d), 0.05)
    ln_b = nrm(ks[14], (DEPTH, 2, d), 0.05)
    w_gate_up = nrm(ks[15], (DEPTH, d, 2 * D_FF), d ** -0.5)
    w_down = nrm(ks[16], (DEPTH, D_FF, d), BETA * D_FF ** -0.5)
    return {"x_prompt": x_prompt, "x_sample": x_sample, "state_conv": state_conv,
            "cache_k": cache_k, "cache_v": cache_v, "w_in_a": w_in_a, "conv_w": conv_w,
            "w_out_a": w_out_a, "w_kv": w_kv, "w_q": w_q, "w_o": w_o, "rel_bias": rel_bias,
            "ln_g": ln_g, "ln_b": ln_b, "w_gate_up": w_gate_up, "w_down": w_down}


def reference(x_prompt, x_sample, state_conv, cache_k, cache_v, w_in_a, conv_w, w_out_a,
              w_kv, w_q, w_o, rel_bias, ln_g, ln_b, w_gate_up, w_down):
    weights = (w_in_a, conv_w, w_out_a, w_kv, w_q, w_o, rel_bias, ln_g, ln_b, w_gate_up, w_down)
    conv_zero = jnp.zeros((N_A, x_prompt.shape[0], CONV_W - 1, D_MODEL), x_prompt.dtype)
    y_prompt, conv_prompt, k_full, v_full = run_trunk(x_prompt, conv_zero, None, None, *weights)
    y_sample, conv_sample, k_sample, v_sample = run_trunk(x_sample, state_conv, cache_k, cache_v, *weights)
    k_prompt = k_full[:, -LEFT_ROWS:]
    v_prompt = v_full[:, -LEFT_ROWS:]
    return (y_prompt, y_sample, conv_prompt, conv_sample, k_prompt, v_prompt, k_sample, v_sample)
```

```python
import functools

import jax
import jax.numpy as jnp
from jax import lax
from jax.experimental import pallas as pl
from jax.experimental.pallas import tpu as pltpu

D_MODEL = 2048
DEPTH = 2
CHUNK = 64
N_LEFT_CHUNKS = 8
LEFT_ROWS = N_LEFT_CHUNKS * CHUNK
BAND = LEFT_ROWS + CHUNK
N_HEADS = 16
HEAD_DIM = D_MODEL // N_HEADS
MAX_REL = 256
N_REL = 2 * MAX_REL + 1
N_REL_PAD = 640
CONV_W = 3
D_FF = 5632
ALPHA = (2.0 * DEPTH) ** 0.25
LN_EPS = 1e-5
SCALE = HEAD_DIM ** -0.5

LANES = 128
VMEM_LIMIT_BYTES = 56 * 1024 * 1024

BF16 = jnp.bfloat16
F32 = jnp.float32


def _params(n_axes):
    return pltpu.CompilerParams(dimension_semantics=("arbitrary",) * n_axes,
                                vmem_limit_bytes=VMEM_LIMIT_BYTES)


def _dot(a, b):
    return jnp.dot(a, b, preferred_element_type=F32)


def _conv_gate_kernel(x_ref, wb_ref, wc_ref, wh_ref, hist_ref, cw_ref,
                      g_ref, state_ref, xb_ref, *carry_ref, nseq):
    i = pl.program_id(0)
    j = pl.program_id(1)
    tm, tn = g_ref.shape
    ls = tm // nseq

    @pl.when(j == 0)
    def _():
        xb_ref[...] = x_ref[...].astype(BF16)

    xb = xb_ref[...]
    b = _dot(xb, wb_ref[...].astype(BF16))
    c = _dot(xb, wc_ref[...].astype(BF16))
    h = _dot(xb, wh_ref[...].astype(BF16))
    u = c * h

    carried = bool(carry_ref)
    if carried:
        (carry_ref,) = carry_ref

        @pl.when(i == 0)
        def _():
            carry_ref[j] = hist_ref[0]

        prev = carry_ref[j][None]
    else:
        prev = hist_ref[...]

    shape3 = (nseq, ls, tn)
    pos = lax.broadcasted_iota(jnp.int32, shape3, 1)
    u3 = u.reshape(shape3)
    p1 = pltpu.roll(u, 1, 0).reshape(shape3)
    p2 = pltpu.roll(u, 2, 0).reshape(shape3)
    h0 = prev[:, 0:1, :]
    h1 = prev[:, 1:2, :]
    p1 = jnp.where(pos == 0, h1, p1)
    p2 = jnp.where(pos == 0, h0, jnp.where(pos == 1, h1, p2))
    conv = cw_ref[0:1, :] * p2 + cw_ref[1:2, :] * p1 + cw_ref[2:3, :] * u3
    g_ref[...] = (b * conv.reshape(tm, tn)).astype(BF16)

    new_state = u3[:, ls - 2:ls, :]
    state_ref[...] = new_state
    if carried:
        carry_ref[j] = new_state[0]


def _conv_gate(x, w_in, hist, conv_w, *, seq_len, tm, tn):
    m, d = x.shape
    nj = d // tn
    carry = seq_len > tm
    nseq = 1 if carry else tm // seq_len
    hist_map = (lambda i, j: (0, 0, j)) if carry else (lambda i, j: (i, 0, j))
    scratch = [pltpu.VMEM((tm, d), BF16)]
    if carry:
        scratch.append(pltpu.VMEM((nj, CONV_W - 1, tn), F32))
    n_states = (m // tm) * nseq
    g, states = pl.pallas_call(
        functools.partial(_conv_gate_kernel, nseq=nseq),
        grid=(m // tm, nj),
        in_specs=[
            pl.BlockSpec((tm, d), lambda i, j: (i, 0)),
            pl.BlockSpec((d, tn), lambda i, j: (0, j)),
            pl.BlockSpec((d, tn), lambda i, j: (0, nj + j)),
            pl.BlockSpec((d, tn), lambda i, j: (0, 2 * nj + j)),
            pl.BlockSpec((nseq, CONV_W - 1, tn), hist_map),
            pl.BlockSpec((CONV_W, tn), lambda i, j: (0, j)),
        ],
        out_specs=[
            pl.BlockSpec((tm, tn), lambda i, j: (i, j)),
            pl.BlockSpec((nseq, CONV_W - 1, tn), lambda i, j: (i, 0, j)),
        ],
        out_shape=[
            jax.ShapeDtypeStruct((m, d), BF16),
            jax.ShapeDtypeStruct((n_states, CONV_W - 1, d), F32),
        ],
        scratch_shapes=scratch,
        compiler_params=_params(2),
        name="conv_gate",
    )(x, w_in, w_in, w_in, hist, conv_w)
    return g, states[-hist.shape[0]:]


LN_ROWS = 64


def _mm_res_ln_kernel(a_ref, w_ref, x_ref, g_ref, b_ref, out_ref, outb_ref):
    k = pl.program_id(1)
    tm, d = out_ref.shape

    @pl.when(k == 0)
    def _():
        out_ref[...] = ALPHA * x_ref[...]

    if len(a_ref.shape) == 3:
        a = jnp.concatenate([a_ref[h] for h in range(a_ref.shape[0])], axis=1)
    else:
        a = a_ref[...]
    out_ref[...] += _dot(a, w_ref[...].astype(BF16))

    @pl.when(k == pl.num_programs(1) - 1)
    def _():
        gain = g_ref[...]
        bias = b_ref[...]

        def body(r, carry):
            rows = pl.ds(pl.multiple_of(r * LN_ROWS, LN_ROWS), LN_ROWS)
            y = out_ref[rows, :]
            mu = jnp.mean(y, axis=-1, keepdims=True)
            yc = y - mu
            var = jnp.mean(yc * yc, axis=-1, keepdims=True)
            z = yc * lax.rsqrt(var + LN_EPS) * gain + bias
            out_ref[rows, :] = z
            outb_ref[rows, :] = z.astype(BF16)
            return carry

        lax.fori_loop(0, tm // LN_ROWS, body, 0)


def _mm_res_ln(a, w, x, gain, bias, *, tm, tk):
    m, d = x.shape
    kdim = w.shape[0]
    if a.ndim == 3:
        a_spec = pl.BlockSpec((tk // HEAD_DIM, tm, HEAD_DIM), lambda i, k: (k, i, 0))
    else:
        a_spec = pl.BlockSpec((tm, tk), lambda i, k: (i, k))
    return pl.pallas_call(
        _mm_res_ln_kernel,
        grid=(m // tm, kdim // tk),
        in_specs=[
            a_spec,
            pl.BlockSpec((tk, d), lambda i, k: (k, 0)),
            pl.BlockSpec((tm, d), lambda i, k: (i, 0)),
            pl.BlockSpec((1, d), lambda i, k: (0, 0)),
            pl.BlockSpec((1, d), lambda i, k: (0, 0)),
        ],
        out_specs=[
            pl.BlockSpec((tm, d), lambda i, k: (i, 0)),
            pl.BlockSpec((tm, d), lambda i, k: (i, 0)),
        ],
        out_shape=[
            jax.ShapeDtypeStruct((m, d), F32),
            jax.ShapeDtypeStruct((m, d), BF16),
        ],
        compiler_params=_params(2),
        name="mm_res_ln",
    )(a, w, x, gain.reshape(1, d), bias.reshape(1, d))


def _gate_up_kernel(xb_ref, wg_ref, wu_ref, h_ref):
    xb = xb_ref[...]
    g = _dot(xb, wg_ref[...].astype(BF16))
    u = _dot(xb, wu_ref[...].astype(BF16))
    h_ref[...] = (jax.nn.silu(g) * u).astype(BF16)


def _gate_up(xb, w_gate_up, *, tm, tf):
    m, d = xb.shape
    nf = D_FF // tf
    return pl.pallas_call(
        _gate_up_kernel,
        grid=(m // tm, nf),
        in_specs=[
            pl.BlockSpec((tm, d), lambda i, j: (i, 0)),
            pl.BlockSpec((d, tf), lambda i, j: (0, j)),
            pl.BlockSpec((d, tf), lambda i, j: (0, nf + j)),
        ],
        out_specs=pl.BlockSpec((tm, tf), lambda i, j: (i, j)),
        out_shape=jax.ShapeDtypeStruct((m, D_FF), BF16),
        compiler_params=_params(2),
        name="gate_up",
    )(xb, w_gate_up, w_gate_up)


def _qkv_kernel(xb_ref, wq_ref, wk_ref, wv_ref, q_ref, kb_ref, vb_ref, kf_ref, vf_ref):
    xb = xb_ref[...]
    q = _dot(xb, wq_ref[...].astype(BF16))
    k = _dot(xb, wk_ref[...].astype(BF16))
    v = _dot(xb, wv_ref[...].astype(BF16))
    kf_ref[...] = k
    vf_ref[...] = v
    for hh in range(q_ref.shape[0]):
        cols = slice(hh * HEAD_DIM, (hh + 1) * HEAD_DIM)
        q_ref[hh] = q[:, cols].astype(BF16)
        kb_ref[hh] = k[:, cols].astype(BF16)
        vb_ref[hh] = v[:, cols].astype(BF16)


def _qkv(xb, w_q, w_kv, *, tm, tn):
    m, d = xb.shape
    nj = d // tn
    hb = tn // HEAD_DIM
    hm_spec = pl.BlockSpec((hb, tm, HEAD_DIM), lambda i, j: (j, i, 0))
    hm_shape = jax.ShapeDtypeStruct((N_HEADS, m, HEAD_DIM), BF16)
    return pl.pallas_call(
        _qkv_kernel,
        grid=(m // tm, nj),
        in_specs=[
            pl.BlockSpec((tm, d), lambda i, j: (i, 0)),
            pl.BlockSpec((d, tn), lambda i, j: (0, j)),
            pl.BlockSpec((d, tn), lambda i, j: (0, j)),
            pl.BlockSpec((d, tn), lambda i, j: (0, nj + j)),
        ],
        out_specs=[hm_spec, hm_spec, hm_spec,
                   pl.BlockSpec((tm, tn), lambda i, j: (i, j)),
                   pl.BlockSpec((tm, tn), lambda i, j: (i, j))],
        out_shape=[hm_shape, hm_shape, hm_shape,
                   jax.ShapeDtypeStruct((m, d), F32),
                   jax.ShapeDtypeStruct((m, d), F32)],
        compiler_params=_params(2),
        name="qkv",
    )(xb, w_q, w_kv, w_kv)


def _bias_kernel(rb_ref, out_ref):
    rb = rb_ref[...]
    hi = rb.astype(BF16)
    r1 = rb - hi.astype(F32)
    mid = r1.astype(BF16)
    lo = (r1 - mid.astype(F32)).astype(BF16)
    src = lax.broadcasted_iota(jnp.int32, (N_REL_PAD, N_REL_PAD), 0)
    m = lax.broadcasted_iota(jnp.int32, (N_REL_PAD, N_REL_PAD), 1)
    idx = jnp.clip(LEFT_ROWS + CHUNK - 1 - m, -MAX_REL, MAX_REL) + MAX_REL
    onehot = (src == idx).astype(BF16)
    t = (_dot(hi, onehot) + _dot(mid, onehot)) + _dot(lo, onehot)
    for h in range(N_HEADS):
        rows = jnp.broadcast_to(t[h:h + 1, :], (CHUNK, N_REL_PAD))
        rows = pltpu.roll(rows, N_REL_PAD - (CHUNK - 1), 1, stride=1, stride_axis=0)
        out_ref[h] = rows[:, :BAND]


def _bias_table(rel_bias):
    rb = jnp.pad(rel_bias, ((0, 0), (0, N_REL_PAD - N_REL)))
    return pl.pallas_call(
        _bias_kernel,
        out_shape=jax.ShapeDtypeStruct((N_HEADS, CHUNK, BAND), F32),
        name="rel_bias_table",
    )(rb)


def _attend(q, kband, vband, bias, valid):
    s = lax.dot_general(q, kband, (((1,), (1,)), ((), ())), preferred_element_type=F32)
    s = s * SCALE + bias
    if valid is not None:
        s = jnp.where(valid, s, -jnp.inf)
    e = jnp.exp(s - jnp.max(s, axis=-1, keepdims=True))
    p = e * (1.0 / jnp.sum(e, axis=-1, keepdims=True))
    return _dot(p.astype(BF16), vband)


GROUP = LEFT_ROWS // CHUNK


def _attn_prompt_kernel(q_ref, kp_ref, kc_ref, vp_ref, vc_ref, bias_ref, o_ref, kwin, vwin):
    g = pl.program_id(0)
    col = lax.broadcasted_iota(jnp.int32, (CHUNK, BAND), 1)

    def head(h, carry):
        kwin[0:LEFT_ROWS] = kp_ref[h]
        kwin[LEFT_ROWS:] = kc_ref[h]
        vwin[0:LEFT_ROWS] = vp_ref[h]
        vwin[LEFT_ROWS:] = vc_ref[h]
        bias = bias_ref[h]
        for c in range(GROUP):
            valid = col >= jnp.where(g > 0, 0, (GROUP - c) * CHUNK)
            band = slice(c * CHUNK, c * CHUNK + BAND)
            rows = slice(c * CHUNK, (c + 1) * CHUNK)
            o = _attend(q_ref[h, rows, :], kwin[band, :], vwin[band, :], bias, valid)
            o_ref[h, rows, :] = o.astype(BF16)
        return carry

    lax.fori_loop(0, N_HEADS, head, 0)


def _attn_prompt(q, kb, vb, bias):
    _, m, _ = q.shape
    blk = (N_HEADS, LEFT_ROWS, HEAD_DIM)
    cur = pl.BlockSpec(blk, lambda g: (0, g, 0))
    prev = pl.BlockSpec(blk, lambda g: (0, jnp.maximum(g - 1, 0), 0))
    return pl.pallas_call(
        _attn_prompt_kernel,
        grid=(m // LEFT_ROWS,),
        in_specs=[cur, prev, cur, prev, cur,
                  pl.BlockSpec((N_HEADS, CHUNK, BAND), lambda g: (0, 0, 0))],
        out_specs=cur,
        out_shape=jax.ShapeDtypeStruct(q.shape, BF16),
        scratch_shapes=[pltpu.VMEM((2 * LEFT_ROWS, HEAD_DIM), BF16),
                        pltpu.VMEM((2 * LEFT_ROWS, HEAD_DIM), BF16)],
        compiler_params=_params(1),
        name="attn_prompt",
    )(q, kb, kb, vb, vb, bias)


def _attn_sample_kernel(q_ref, kc_ref, vc_ref, ck_ref, cv_ref, bias_ref, o_ref, kwin, vwin):
    for h in range(N_HEADS):
        cols = slice(h * HEAD_DIM, (h + 1) * HEAD_DIM)
        kwin[0:LEFT_ROWS] = ck_ref[0, :, cols].astype(BF16)
        kwin[LEFT_ROWS:] = kc_ref[h]
        vwin[0:LEFT_ROWS] = cv_ref[0, :, cols].astype(BF16)
        vwin[LEFT_ROWS:] = vc_ref[h]
        o = _attend(q_ref[h], kwin[...], vwin[...], bias_ref[h], None)
        o_ref[h] = o.astype(BF16)


def _attn_sample(q, kb, vb, cache_k, cache_v, bias):
    n = cache_k.shape[0]
    new = pl.BlockSpec((N_HEADS, CHUNK, HEAD_DIM), lambda s: (0, s, 0))
    cache = pl.BlockSpec((1, LEFT_ROWS, D_MODEL), lambda s: (s, 0, 0))
    return pl.pallas_call(
        _attn_sample_kernel,
        grid=(n,),
        in_specs=[new, new, new, cache, cache,
                  pl.BlockSpec((N_HEADS, CHUNK, BAND), lambda s: (0, 0, 0))],
        out_specs=new,
        out_shape=jax.ShapeDtypeStruct(q.shape, BF16),
        scratch_shapes=[pltpu.VMEM((BAND, HEAD_DIM), BF16),
                        pltpu.VMEM((BAND, HEAD_DIM), BF16)],
        compiler_params=_params(1),
        name="attn_sample",
    )(q, kb, vb, cache_k, cache_v, bias)


TM = 1024
TN_IN = 256
TN_QKV = 512
TK_OUT = 512
TF = 512


def _ffn(x, xb, w_gate_up, w_down, gain, bias):
    h = _gate_up(xb, w_gate_up, tm=TM, tf=TF)
    return _mm_res_ln(h, w_down, x, gain, bias, tm=TM, tk=TK_OUT)


def _trunk(x, hist, seq_len, cache, bias_table, w_in_a, conv_w, w_out_a, w_kv, w_q, w_o,
           ln_g, ln_b, w_gate_up, w_down):
    g, conv_state = _conv_gate(x, w_in_a[0], hist, conv_w[0], seq_len=seq_len, tm=TM, tn=TN_IN)
    x, xb = _mm_res_ln(g, w_out_a[0], x, ln_g[0, 0], ln_b[0, 0], tm=TM, tk=TK_OUT)
    x, xb = _ffn(x, xb, w_gate_up[0], w_down[0], ln_g[0, 1], ln_b[0, 1])

    q, kb, vb, k, v = _qkv(xb, w_q[0], w_kv, tm=TM, tn=TN_QKV)
    if cache is None:
        att = _attn_prompt(q, kb, vb, bias_table)
    else:
        att = _attn_sample(q, kb, vb, cache[0], cache[1], bias_table)
    x, xb = _mm_res_ln(att, w_o[0], x, ln_g[1, 0], ln_b[1, 0], tm=TM, tk=TK_OUT)
    x, _ = _ffn(x, xb, w_gate_up[1], w_down[1], ln_g[1, 1], ln_b[1, 1])
    return x, conv_state, k, v


def kernel(x_prompt, x_sample, state_conv, cache_k, cache_v, w_in_a, conv_w, w_out_a, w_kv, w_q,
           w_o, rel_bias, ln_g, ln_b, w_gate_up, w_down):
    batch, seq, d = x_prompt.shape
    dec_batch, dec_seq, _ = x_sample.shape
    assert batch == 1 and DEPTH == 2 and dec_seq == CHUNK and cache_k.shape[1] == LEFT_ROWS
    weights = (w_in_a, conv_w, w_out_a, w_kv, w_q, w_o, ln_g, ln_b, w_gate_up, w_down)
    bias_table = _bias_table(rel_bias[0])

    conv_zero = jnp.zeros((batch, CONV_W - 1, d), x_prompt.dtype)
    y_p, conv_p, k_p, v_p = _trunk(x_prompt.reshape(seq, d), conv_zero, seq, None,
                                   bias_table, *weights)
    cache = (cache_k.reshape(dec_batch, LEFT_ROWS, d), cache_v.reshape(dec_batch, LEFT_ROWS, d))
    y_s, conv_s, k_s, v_s = _trunk(x_sample.reshape(dec_batch * dec_seq, d), state_conv[0],
                                   dec_seq, cache, bias_table, *weights)

    kv_prompt = (batch, LEFT_ROWS, N_HEADS, HEAD_DIM)
    kv_sample = (dec_batch, dec_seq, N_HEADS, HEAD_DIM)
    return (y_p.reshape(batch, seq, d),
            y_s.reshape(dec_batch, dec_seq, d),
            conv_p.reshape(1, batch, CONV_W - 1, d),
            conv_s.reshape(1, dec_batch, CONV_W - 1, d),
            k_p[-LEFT_ROWS:].reshape(kv_prompt),
            v_p[-LEFT_ROWS:].reshape(kv_prompt),
            k_s.reshape(kv_sample),
            v_s.reshape(kv_sample))
```

```python
import functools

import jax
import jax.numpy as jnp
from jax import lax
from jax.experimental import pallas as pl
from jax.experimental.pallas import tpu as pltpu

D_MODEL = 2048
DEPTH = 2
CHUNK = 64
N_LEFT_CHUNKS = 8
LEFT_ROWS = N_LEFT_CHUNKS * CHUNK
BAND = LEFT_ROWS + CHUNK
N_HEADS = 16
HEAD_DIM = D_MODEL // N_HEADS
MAX_REL = 256
N_REL = 2 * MAX_REL + 1
CONV_W = 3
D_FF = 5632
ALPHA = (2.0 * DEPTH) ** 0.25
LN_EPS = 1e-5
SCALE = HEAD_DIM ** -0.5

LANES = 128
VMEM_LIMIT_BYTES = 56 * 1024 * 1024

PAIR = 2 * CHUNK
PAIR_BAND = BAND + CHUNK
N_REL_PAD = 640
T_PAD = 768

BF16 = jnp.bfloat16
F32 = jnp.float32


def _params(n_axes):
    return pltpu.CompilerParams(dimension_semantics=("arbitrary",) * n_axes,
                                vmem_limit_bytes=VMEM_LIMIT_BYTES)


def _dot(a, b):
    return jnp.dot(a, b, preferred_element_type=F32)


def _dot_t(a, b):
    return lax.dot_general(a, b, (((1,), (1,)), ((), ())), preferred_element_type=F32)


def _conv_gate_kernel(x_ref, wb_ref, wc_ref, wh_ref, hist_ref, cw_ref,
                      g_ref, state_ref, xb_ref, *carry_ref, nseq):
    i = pl.program_id(0)
    j = pl.program_id(1)
    tm, tn = g_ref.shape
    ls = tm // nseq

    @pl.when(j == 0)
    def _():
        xb_ref[...] = x_ref[...].astype(BF16)

    xb = xb_ref[...]
    b = _dot(xb, wb_ref[...].astype(BF16))
    c = _dot(xb, wc_ref[...].astype(BF16))
    h = _dot(xb, wh_ref[...].astype(BF16))
    u = c * h

    carried = bool(carry_ref)
    if carried:
        (carry_ref,) = carry_ref

        @pl.when(i == 0)
        def _():
            carry_ref[j] = hist_ref[0]

        prev = carry_ref[j][None]
    else:
        prev = hist_ref[...]

    shape3 = (nseq, ls, tn)
    pos = lax.broadcasted_iota(jnp.int32, shape3, 1)
    u3 = u.reshape(shape3)
    p1 = pltpu.roll(u, 1, 0).reshape(shape3)
    p2 = pltpu.roll(u, 2, 0).reshape(shape3)
    h0 = prev[:, 0:1, :]
    h1 = prev[:, 1:2, :]
    p1 = jnp.where(pos == 0, h1, p1)
    p2 = jnp.where(pos == 0, h0, jnp.where(pos == 1, h1, p2))
    conv = cw_ref[0:1, :] * p2 + cw_ref[1:2, :] * p1 + cw_ref[2:3, :] * u3
    g_ref[...] = (b * conv.reshape(tm, tn)).astype(BF16)

    new_state = u3[:, ls - 2:ls, :]
    state_ref[...] = new_state
    if carried:
        carry_ref[j] = new_state[0]


def _conv_gate(x, w_in, hist, conv_w, *, seq_len, tm, tn):
    m, d = x.shape
    nj = d // tn
    carry = seq_len > tm
    nseq = 1 if carry else tm // seq_len
    hist_map = (lambda i, j: (0, 0, j)) if carry else (lambda i, j: (i, 0, j))
    scratch = [pltpu.VMEM((tm, d), BF16)]
    if carry:
        scratch.append(pltpu.VMEM((nj, CONV_W - 1, tn), F32))
    n_states = (m // tm) * nseq
    g, states = pl.pallas_call(
        functools.partial(_conv_gate_kernel, nseq=nseq),
        grid=(m // tm, nj),
        in_specs=[
            pl.BlockSpec((tm, d), lambda i, j: (i, 0)),
            pl.BlockSpec((d, tn), lambda i, j: (0, j)),
            pl.BlockSpec((d, tn), lambda i, j: (0, nj + j)),
            pl.BlockSpec((d, tn), lambda i, j: (0, 2 * nj + j)),
            pl.BlockSpec((nseq, CONV_W - 1, tn), hist_map),
            pl.BlockSpec((CONV_W, tn), lambda i, j: (0, j)),
        ],
        out_specs=[
            pl.BlockSpec((tm, tn), lambda i, j: (i, j)),
            pl.BlockSpec((nseq, CONV_W - 1, tn), lambda i, j: (i, 0, j)),
        ],
        out_shape=[
            jax.ShapeDtypeStruct((m, d), BF16),
            jax.ShapeDtypeStruct((n_states, CONV_W - 1, d), F32),
        ],
        scratch_shapes=scratch,
        compiler_params=_params(2),
        name="conv_gate",
    )(x, w_in, w_in, w_in, hist, conv_w)
    return g, states[-hist.shape[0]:]


LN_ROWS = 64


def _mm_res_ln_kernel(a_ref, w_ref, x_ref, g_ref, b_ref, out_ref, outb_ref):
    k = pl.program_id(1)
    tm, d = out_ref.shape

    @pl.when(k == 0)
    def _():
        out_ref[...] = ALPHA * x_ref[...]

    if len(a_ref.shape) == 3:
        a = jnp.concatenate([a_ref[h] for h in range(a_ref.shape[0])], axis=1)
    else:
        a = a_ref[...]
    out_ref[...] += _dot(a, w_ref[...].astype(BF16))

    @pl.when(k == pl.num_programs(1) - 1)
    def _():
        gain = g_ref[...]
        bias = b_ref[...]

        def body(r, carry):
            rows = pl.ds(pl.multiple_of(r * LN_ROWS, LN_ROWS), LN_ROWS)
            y = out_ref[rows, :]
            mu = jnp.mean(y, axis=-1, keepdims=True)
            yc = y - mu
            var = jnp.mean(yc * yc, axis=-1, keepdims=True)
            z = yc * lax.rsqrt(var + LN_EPS) * gain + bias
            out_ref[rows, :] = z
            outb_ref[rows, :] = z.astype(BF16)
            return carry

        lax.fori_loop(0, tm // LN_ROWS, body, 0)


def _mm_res_ln(a, w, x, gain, bias, *, tm, tk):
    m, d = x.shape
    kdim = w.shape[0]
    if a.ndim == 3:
        a_spec = pl.BlockSpec((tk // HEAD_DIM, tm, HEAD_DIM), lambda i, k: (k, i, 0))
    else:
        a_spec = pl.BlockSpec((tm, tk), lambda i, k: (i, k))
    return pl.pallas_call(
        _mm_res_ln_kernel,
        grid=(m // tm, kdim // tk),
        in_specs=[
            a_spec,
            pl.BlockSpec((tk, d), lambda i, k: (k, 0)),
            pl.BlockSpec((tm, d), lambda i, k: (i, 0)),
            pl.BlockSpec((1, d), lambda i, k: (0, 0)),
            pl.BlockSpec((1, d), lambda i, k: (0, 0)),
        ],
        out_specs=[
            pl.BlockSpec((tm, d), lambda i, k: (i, 0)),
            pl.BlockSpec((tm, d), lambda i, k: (i, 0)),
        ],
        out_shape=[
            jax.ShapeDtypeStruct((m, d), F32),
            jax.ShapeDtypeStruct((m, d), BF16),
        ],
        compiler_params=_params(2),
        name="mm_res_ln",
    )(a, w, x, gain.reshape(1, d), bias.reshape(1, d))


def _gate_up_kernel(xb_ref, wg_ref, wu_ref, h_ref):
    xb = xb_ref[...]
    g = _dot(xb, wg_ref[...].astype(BF16))
    u = _dot(xb, wu_ref[...].astype(BF16))
    h_ref[...] = (jax.nn.silu(g) * u).astype(BF16)


def _gate_up(xb, w_gate_up, *, tm, tf):
    m, d = xb.shape
    nf = D_FF // tf
    return pl.pallas_call(
        _gate_up_kernel,
        grid=(m // tm, nf),
        in_specs=[
            pl.BlockSpec((tm, d), lambda i, j: (i, 0)),
            pl.BlockSpec((d, tf), lambda i, j: (0, j)),
            pl.BlockSpec((d, tf), lambda i, j: (0, nf + j)),
        ],
        out_specs=pl.BlockSpec((tm, tf), lambda i, j: (i, j)),
        out_shape=jax.ShapeDtypeStruct((m, D_FF), BF16),
        compiler_params=_params(2),
        name="gate_up",
    )(xb, w_gate_up, w_gate_up)


def _qkv_kernel(xb_ref, wq_ref, wk_ref, wv_ref, q_ref, kb_ref, vb_ref, kf_ref, vf_ref, *,
                transposed_v):
    xb = xb_ref[...]
    q = _dot(xb, wq_ref[...].astype(BF16))
    k = _dot(xb, wk_ref[...].astype(BF16))
    v = _dot(xb, wv_ref[...].astype(BF16))
    kf_ref[...] = k
    vf_ref[...] = v
    for hh in range(q_ref.shape[0]):
        cols = slice(hh * HEAD_DIM, (hh + 1) * HEAD_DIM)
        q_ref[hh] = q[:, cols].astype(BF16)
        kb_ref[hh] = k[:, cols].astype(BF16)
        if transposed_v:
            vb_ref[hh] = v[:, cols].T.astype(BF16)
        else:
            vb_ref[hh] = v[:, cols].astype(BF16)


def _qkv(xb, w_q, w_kv, *, tm, tn, transposed_v):
    m, d = xb.shape
    nj = d // tn
    hb = tn // HEAD_DIM
    hm_spec = pl.BlockSpec((hb, tm, HEAD_DIM), lambda i, j: (j, i, 0))
    hm_shape = jax.ShapeDtypeStruct((N_HEADS, m, HEAD_DIM), BF16)
    if transposed_v:
        v_spec = pl.BlockSpec((hb, HEAD_DIM, tm), lambda i, j: (j, 0, i))
        v_shape = jax.ShapeDtypeStruct((N_HEADS, HEAD_DIM, m), BF16)
    else:
        v_spec, v_shape = hm_spec, hm_shape
    return pl.pallas_call(
        functools.partial(_qkv_kernel, transposed_v=transposed_v),
        grid=(m // tm, nj),
        in_specs=[
            pl.BlockSpec((tm, d), lambda i, j: (i, 0)),
            pl.BlockSpec((d, tn), lambda i, j: (0, j)),
            pl.BlockSpec((d, tn), lambda i, j: (0, j)),
            pl.BlockSpec((d, tn), lambda i, j: (0, nj + j)),
        ],
        out_specs=[hm_spec, hm_spec, v_spec,
                   pl.BlockSpec((tm, tn), lambda i, j: (i, j)),
                   pl.BlockSpec((tm, tn), lambda i, j: (i, j))],
        out_shape=[hm_shape, hm_shape, v_shape,
                   jax.ShapeDtypeStruct((m, d), F32),
                   jax.ShapeDtypeStruct((m, d), F32)],
        compiler_params=_params(2),
        name="qkv",
    )(xb, w_q, w_kv, w_kv)


def _split_heads_kernel(x_ref, o_ref):
    for h in range(N_HEADS):
        o_ref[:, h, :] = x_ref[:, h * HEAD_DIM:(h + 1) * HEAD_DIM]


def _split_heads(x, *, rows, tr):
    m, d = x.shape
    first = (m - rows) // tr
    return pl.pallas_call(
        _split_heads_kernel,
        grid=(rows // tr,),
        in_specs=[pl.BlockSpec((tr, d), lambda i: (first + i, 0))],
        out_specs=pl.BlockSpec((tr, N_HEADS, HEAD_DIM), lambda i: (i, 0, 0)),
        out_shape=jax.ShapeDtypeStruct((rows, N_HEADS, HEAD_DIM), x.dtype),
        compiler_params=_params(1),
        name="split_heads",
    )(x)


def _bias_kernel(rb_ref, chunk_ref, pair_ref):
    rb = rb_ref[...]
    hi = rb.astype(BF16)
    r1 = rb - hi.astype(F32)
    mid = r1.astype(BF16)
    lo = (r1 - mid.astype(F32)).astype(BF16)
    src = lax.broadcasted_iota(jnp.int32, (N_REL_PAD, T_PAD), 0)
    m = lax.broadcasted_iota(jnp.int32, (N_REL_PAD, T_PAD), 1)
    idx = jnp.clip(LEFT_ROWS + CHUNK - 1 - m, -MAX_REL, MAX_REL) + MAX_REL
    onehot = (src == idx).astype(BF16)
    t = (_dot(hi, onehot) + _dot(mid, onehot)) + _dot(lo, onehot)

    key = lax.broadcasted_iota(jnp.int32, (PAIR_BAND, PAIR), 0)
    qry = lax.broadcasted_iota(jnp.int32, (PAIR_BAND, PAIR), 1)
    in_band = ((qry < CHUNK) & (key < BAND)) | ((qry >= CHUNK) & (key >= CHUNK))
    for h in range(N_HEADS):
        rows = jnp.broadcast_to(t[h:h + 1, :], (PAIR, T_PAD))
        rows = pltpu.roll(rows, T_PAD - (CHUNK - 1), 1, stride=1, stride_axis=0)
        chunk_ref[h] = rows[:CHUNK, :BAND]
        pair_ref[h] = jnp.where(in_band, rows[:, :PAIR_BAND].T, -jnp.inf)


def _bias_tables(rel_bias):
    rb = jnp.pad(rel_bias, ((0, 0), (0, N_REL_PAD - N_REL)))
    return pl.pallas_call(
        _bias_kernel,
        out_shape=[jax.ShapeDtypeStruct((N_HEADS, CHUNK, BAND), F32),
                   jax.ShapeDtypeStruct((N_HEADS, PAIR_BAND, PAIR), F32)],
        name="rel_bias_tables",
    )(rb)


PAIRS = LEFT_ROWS // PAIR


def _attn_prompt_kernel(q_ref, kp_ref, kc_ref, vp_ref, vc_ref, bias_ref, o_ref, kwin, vwin):
    g = pl.program_id(0)
    key = lax.broadcasted_iota(jnp.int32, (PAIR_BAND, PAIR), 0)

    def head(h, first):
        kwin[0:LEFT_ROWS] = kp_ref[h]
        kwin[LEFT_ROWS:] = kc_ref[h]
        vwin[:, 0:LEFT_ROWS] = vp_ref[h]
        vwin[:, LEFT_ROWS:] = vc_ref[h]
        bias = bias_ref[h]
        for p in range(PAIRS):
            rows = slice(p * PAIR, (p + 1) * PAIR)
            band = slice(p * PAIR, p * PAIR + PAIR_BAND)
            s = _dot_t(kwin[band, :], q_ref[h, rows, :]) * SCALE + bias
            if first:
                s = jnp.where(key >= LEFT_ROWS - p * PAIR, s, -jnp.inf)
            e = jnp.exp(s - jnp.max(s, axis=0, keepdims=True))
            denom = jnp.sum(e, axis=0, keepdims=True)
            o = _dot(vwin[:, band], e.astype(BF16)) * (1.0 / denom)
            o_ref[h, rows, :] = o.T.astype(BF16)

    def heads(first):
        def body(h, carry):
            head(h, first)
            return carry
        lax.fori_loop(0, N_HEADS, body, 0)

    pl.when(g == 0)(functools.partial(heads, True))
    pl.when(g > 0)(functools.partial(heads, False))


def _attn_prompt(q, kb, vt, bias):
    _, m, _ = q.shape
    blk = (N_HEADS, LEFT_ROWS, HEAD_DIM)
    cur = pl.BlockSpec(blk, lambda g: (0, g, 0))
    prev = pl.BlockSpec(blk, lambda g: (0, jnp.maximum(g - 1, 0), 0))
    blk_t = (N_HEADS, HEAD_DIM, LEFT_ROWS)
    cur_t = pl.BlockSpec(blk_t, lambda g: (0, 0, g))
    prev_t = pl.BlockSpec(blk_t, lambda g: (0, 0, jnp.maximum(g - 1, 0)))
    return pl.pallas_call(
        _attn_prompt_kernel,
        grid=(m // LEFT_ROWS,),
        in_specs=[cur, prev, cur, prev_t, cur_t,
                  pl.BlockSpec((N_HEADS, PAIR_BAND, PAIR), lambda g: (0, 0, 0))],
        out_specs=cur,
        out_shape=jax.ShapeDtypeStruct(q.shape, BF16),
        scratch_shapes=[pltpu.VMEM((2 * LEFT_ROWS, HEAD_DIM), BF16),
                        pltpu.VMEM((HEAD_DIM, 2 * LEFT_ROWS), BF16)],
        compiler_params=_params(1),
        name="attn_prompt",
    )(q, kb, kb, vt, vt, bias)


def _attn_sample_kernel(q_ref, kc_ref, vc_ref, ck_ref, cv_ref, bias_ref, o_ref, kwin, vwin):
    for h in range(N_HEADS):
        kwin[0:LEFT_ROWS] = ck_ref[0, :, h, :].astype(BF16)
        kwin[LEFT_ROWS:] = kc_ref[h]
        vwin[0:LEFT_ROWS] = cv_ref[0, :, h, :].astype(BF16)
        vwin[LEFT_ROWS:] = vc_ref[h]
        s = _dot_t(q_ref[h], kwin[...]) * SCALE + bias_ref[h]
        e = jnp.exp(s - jnp.max(s, axis=-1, keepdims=True))
        p = e * (1.0 / jnp.sum(e, axis=-1, keepdims=True))
        o_ref[h] = _dot(p.astype(BF16), vwin[...]).astype(BF16)


def _attn_sample(q, kb, vb, cache_k, cache_v, bias):
    n = cache_k.shape[0]
    new = pl.BlockSpec((N_HEADS, CHUNK, HEAD_DIM), lambda s: (0, s, 0))
    cache = pl.BlockSpec((1, LEFT_ROWS, N_HEADS, HEAD_DIM), lambda s: (s, 0, 0, 0))
    return pl.pallas_call(
        _attn_sample_kernel,
        grid=(n,),
        in_specs=[new, new, new, cache, cache,
                  pl.BlockSpec((N_HEADS, CHUNK, BAND), lambda s: (0, 0, 0))],
        out_specs=new,
        out_shape=jax.ShapeDtypeStruct(q.shape, BF16),
        scratch_shapes=[pltpu.VMEM((BAND, HEAD_DIM), BF16),
                        pltpu.VMEM((BAND, HEAD_DIM), BF16)],
        compiler_params=_params(1),
        name="attn_sample",
    )(q, kb, vb, cache_k, cache_v, bias)


TM = 1024
TN_IN = 256
TN_QKV = 512
TK_OUT = 512
TF = 512
TR_SPLIT = 256


def _ffn(x, xb, w_gate_up, w_down, gain, bias):
    h = _gate_up(xb, w_gate_up, tm=TM, tf=TF)
    return _mm_res_ln(h, w_down, x, gain, bias, tm=TM, tk=TK_OUT)


def _trunk(x, hist, seq_len, cache, bias_tables, kv_rows, w_in_a, conv_w, w_out_a, w_kv, w_q,
           w_o, ln_g, ln_b, w_gate_up, w_down):
    g, conv_state = _conv_gate(x, w_in_a[0], hist, conv_w[0], seq_len=seq_len, tm=TM, tn=TN_IN)
    x, xb = _mm_res_ln(g, w_out_a[0], x, ln_g[0, 0], ln_b[0, 0], tm=TM, tk=TK_OUT)
    x, xb = _ffn(x, xb, w_gate_up[0], w_down[0], ln_g[0, 1], ln_b[0, 1])

    bias_chunk, bias_pair = bias_tables
    q, kb, vb, k, v = _qkv(xb, w_q[0], w_kv, tm=TM, tn=TN_QKV, transposed_v=cache is None)
    if cache is None:
        att = _attn_prompt(q, kb, vb, bias_pair)
    else:
        att = _attn_sample(q, kb, vb, cache[0], cache[1], bias_chunk)
    x, xb = _mm_res_ln(att, w_o[0], x, ln_g[1, 0], ln_b[1, 0], tm=TM, tk=TK_OUT)
    x, _ = _ffn(x, xb, w_gate_up[1], w_down[1], ln_g[1, 1], ln_b[1, 1])
    k = _split_heads(k, rows=kv_rows, tr=TR_SPLIT)
    v = _split_heads(v, rows=kv_rows, tr=TR_SPLIT)
    return x, conv_state, k, v


def kernel(x_prompt, x_sample, state_conv, cache_k, cache_v, w_in_a, conv_w, w_out_a, w_kv, w_q,
           w_o, rel_bias, ln_g, ln_b, w_gate_up, w_down):
    batch, seq, d = x_prompt.shape
    dec_batch, dec_seq, _ = x_sample.shape
    assert batch == 1 and DEPTH == 2 and dec_seq == CHUNK and cache_k.shape[1] == LEFT_ROWS
    weights = (w_in_a, conv_w, w_out_a, w_kv, w_q, w_o, ln_g, ln_b, w_gate_up, w_down)
    bias_tables = _bias_tables(rel_bias[0])

    conv_zero = jnp.zeros((batch, CONV_W - 1, d), x_prompt.dtype)
    y_p, conv_p, k_p, v_p = _trunk(x_prompt.reshape(seq, d), conv_zero, seq, None,
                                   bias_tables, LEFT_ROWS, *weights)
    y_s, conv_s, k_s, v_s = _trunk(x_sample.reshape(dec_batch * dec_seq, d), state_conv[0],
                                   dec_seq, (cache_k, cache_v), bias_tables,
                                   dec_batch * dec_seq, *weights)

    kv_prompt = (batch, LEFT_ROWS, N_HEADS, HEAD_DIM)
    kv_sample = (dec_batch, dec_seq, N_HEADS, HEAD_DIM)
    return (y_p.reshape(batch, seq, d),
            y_s.reshape(dec_batch, dec_seq, d),
            conv_p.reshape(1, batch, CONV_W - 1, d),
            conv_s.reshape(1, dec_batch, CONV_W - 1, d),
            k_p.reshape(kv_prompt),
            v_p.reshape(kv_prompt),
            k_s.reshape(kv_sample),
            v_s.reshape(kv_sample))
```

```python
import functools

import jax
import jax.numpy as jnp
from jax import lax
from jax.experimental import pallas as pl
from jax.experimental.pallas import tpu as pltpu

D_MODEL = 2048
DEPTH = 2
CHUNK = 64
N_LEFT_CHUNKS = 8
LEFT_ROWS = N_LEFT_CHUNKS * CHUNK
BAND = LEFT_ROWS + CHUNK
N_HEADS = 16
HEAD_DIM = D_MODEL // N_HEADS
MAX_REL = 256
N_REL = 2 * MAX_REL + 1
CONV_W = 3
D_FF = 5632
ALPHA = (2.0 * DEPTH) ** 0.25
LN_EPS = 1e-5
SCALE = HEAD_DIM ** -0.5

LANES = 128
VMEM_LIMIT_BYTES = 56 * 1024 * 1024

PAIR = 2 * CHUNK
PAIR_BAND = BAND + CHUNK
N_REL_PAD = 640
T_PAD = 768

BF16 = jnp.bfloat16
F32 = jnp.float32


def _params(n_axes):
    return pltpu.CompilerParams(dimension_semantics=("arbitrary",) * n_axes,
                                vmem_limit_bytes=VMEM_LIMIT_BYTES)


def _dot(a, b):
    return jnp.dot(a, b, preferred_element_type=F32)


def _dot_t(a, b):
    return lax.dot_general(a, b, (((1,), (1,)), ((), ())), preferred_element_type=F32)


def _wspec(weight, blk, imap):
    arr, layer, first_col = weight
    first = first_col // blk[1]
    if arr.ndim == 3:
        return pl.BlockSpec((None,) + blk, lambda *g: (layer,) + _shift(imap(*g), first))
    return pl.BlockSpec(blk, lambda *g: _shift(imap(*g), first))


def _shift(idx, first):
    return (idx[0], idx[1] + first)


def _wload(w_ref, wb_ref):
    w = w_ref[...]
    if w.dtype != BF16:
        w = w.astype(BF16)
    if wb_ref is not None:
        wb_ref[...] = w
    return w


def _wb_outputs(weights, emit, blk, imap, shape):
    if not emit:
        return [], []
    return ([pl.BlockSpec(blk, imap)] * len(weights),
            [jax.ShapeDtypeStruct(shape, BF16)] * len(weights))


def _as_weights(arrs):
    return tuple((a, 0, 0) for a in arrs)


def _conv_gate_kernel(x_ref, wb_ref, wc_ref, wh_ref, hist_ref, cw_ref, g_ref, state_ref, *rest,
                      nseq, emit, carried):
    rest = list(rest)
    wbo = [rest.pop(0) for _ in range(3)] if emit else [None] * 3
    xb_ref = rest.pop(0)
    i = pl.program_id(0)
    j = pl.program_id(1)
    tm, tn = g_ref.shape
    ls = tm // nseq

    @pl.when(j == 0)
    def _():
        xb_ref[...] = x_ref[...].astype(BF16)

    xb = xb_ref[...]
    b = _dot(xb, _wload(wb_ref, wbo[0]))
    c = _dot(xb, _wload(wc_ref, wbo[1]))
    h = _dot(xb, _wload(wh_ref, wbo[2]))
    u = c * h

    if carried:
        carry_ref = rest.pop(0)

        @pl.when(i == 0)
        def _():
            carry_ref[j] = hist_ref[0]

        prev = carry_ref[j][None]
    else:
        prev = hist_ref[...]

    shape3 = (nseq, ls, tn)
    pos = lax.broadcasted_iota(jnp.int32, shape3, 1)
    u3 = u.reshape(shape3)
    p1 = pltpu.roll(u, 1, 0).reshape(shape3)
    p2 = pltpu.roll(u, 2, 0).reshape(shape3)
    h0 = prev[:, 0:1, :]
    h1 = prev[:, 1:2, :]
    p1 = jnp.where(pos == 0, h1, p1)
    p2 = jnp.where(pos == 0, h0, jnp.where(pos == 1, h1, p2))
    conv = cw_ref[0:1, :] * p2 + cw_ref[1:2, :] * p1 + cw_ref[2:3, :] * u3
    g_ref[...] = (b * conv.reshape(tm, tn)).astype(BF16)

    new_state = u3[:, ls - 2:ls, :]
    state_ref[...] = new_state
    if carried:
        carry_ref[j] = new_state[0]


def _conv_gate(x, weights, hist, conv_w, *, seq_len, tm, tn, emit):
    m, d = x.shape
    nj = d // tn
    carried = seq_len > tm
    nseq = 1 if carried else tm // seq_len
    assert not emit or m == tm
    hist_map = (lambda i, j: (0, 0, j)) if carried else (lambda i, j: (i, 0, j))
    scratch = [pltpu.VMEM((tm, d), BF16)]
    if carried:
        scratch.append(pltpu.VMEM((nj, CONV_W - 1, tn), F32))
    wmap = lambda i, j: (0, j)
    wb_specs, wb_shapes = _wb_outputs(weights, emit, (d, tn), wmap, (d, d))
    n_states = (m // tm) * nseq
    g, states, *wb = pl.pallas_call(
        functools.partial(_conv_gate_kernel, nseq=nseq, emit=emit, carried=carried),
        grid=(m // tm, nj),
        in_specs=[pl.BlockSpec((tm, d), lambda i, j: (i, 0))]
        + [_wspec(w, (d, tn), wmap) for w in weights]
        + [pl.BlockSpec((nseq, CONV_W - 1, tn), hist_map),
           pl.BlockSpec((None, CONV_W, tn), lambda i, j: (0, 0, j))],
        out_specs=[
            pl.BlockSpec((tm, tn), lambda i, j: (i, j)),
            pl.BlockSpec((nseq, CONV_W - 1, tn), lambda i, j: (i, 0, j)),
        ] + wb_specs,
        out_shape=[
            jax.ShapeDtypeStruct((m, d), BF16),
            jax.ShapeDtypeStruct((n_states, CONV_W - 1, d), F32),
        ] + wb_shapes,
        scratch_shapes=scratch,
        compiler_params=_params(2),
        name="conv_gate",
    )(x, *[w[0] for w in weights], hist, conv_w)
    return g, states[-hist.shape[0]:], _as_weights(wb)


LN_ROWS = 64


def _mm_res_ln_kernel(a_ref, w_ref, x_ref, g_ref, b_ref, out_ref, outb_ref, *wbo):
    k = pl.program_id(1)
    tm, d = out_ref.shape

    @pl.when(k == 0)
    def _():
        out_ref[...] = ALPHA * x_ref[...]

    if len(a_ref.shape) == 3:
        a = jnp.concatenate([a_ref[h] for h in range(a_ref.shape[0])], axis=1)
    else:
        a = a_ref[...]
    out_ref[...] += _dot(a, _wload(w_ref, wbo[0] if wbo else None))

    @pl.when(k == pl.num_programs(1) - 1)
    def _():
        gain = g_ref[...]
        bias = b_ref[...]

        def body(r, carry):
            rows = pl.ds(pl.multiple_of(r * LN_ROWS, LN_ROWS), LN_ROWS)
            y = out_ref[rows, :]
            mu = jnp.mean(y, axis=-1, keepdims=True)
            yc = y - mu
            var = jnp.mean(yc * yc, axis=-1, keepdims=True)
            z = yc * lax.rsqrt(var + LN_EPS) * gain + bias
            out_ref[rows, :] = z
            outb_ref[rows, :] = z.astype(BF16)
            return carry

        lax.fori_loop(0, tm // LN_ROWS, body, 0)


def _mm_res_ln(a, weight, x, ln_g, ln_b, ln_idx, *, tm, tk, emit):
    m, d = x.shape
    kdim = weight[0].shape[-2]
    assert not emit or m == tm
    if a.ndim == 3:
        a_spec = pl.BlockSpec((tk // HEAD_DIM, tm, HEAD_DIM), lambda i, k: (k, i, 0))
    else:
        a_spec = pl.BlockSpec((tm, tk), lambda i, k: (i, k))
    wmap = lambda i, k: (k, 0)
    wb_specs, wb_shapes = _wb_outputs((weight,), emit, (tk, d), wmap, (kdim, d))
    layer, sub = ln_idx
    ln_spec = pl.BlockSpec((None, None, 1, d), lambda i, k: (layer, sub, 0, 0))
    x_new, xb_new, *wb = pl.pallas_call(
        _mm_res_ln_kernel,
        grid=(m // tm, kdim // tk),
        in_specs=[
            a_spec,
            _wspec(weight, (tk, d), wmap),
            pl.BlockSpec((tm, d), lambda i, k: (i, 0)),
            ln_spec,
            ln_spec,
        ],
        out_specs=[
            pl.BlockSpec((tm, d), lambda i, k: (i, 0)),
            pl.BlockSpec((tm, d), lambda i, k: (i, 0)),
        ] + wb_specs,
        out_shape=[
            jax.ShapeDtypeStruct((m, d), F32),
            jax.ShapeDtypeStruct((m, d), BF16),
        ] + wb_shapes,
        compiler_params=_params(2),
        name="mm_res_ln",
    )(a, weight[0], x, ln_g.reshape(DEPTH, 2, 1, d), ln_b.reshape(DEPTH, 2, 1, d))
    return x_new, xb_new, _as_weights(wb)


def _gate_up_kernel(xb_ref, wg_ref, wu_ref, h_ref, *wbo):
    wbo = wbo or (None, None)
    xb = xb_ref[...]
    g = _dot(xb, _wload(wg_ref, wbo[0]))
    u = _dot(xb, _wload(wu_ref, wbo[1]))
    h_ref[...] = (jax.nn.silu(g) * u).astype(BF16)


def _gate_up(xb, weights, *, tm, tf, emit):
    m, d = xb.shape
    nf = D_FF // tf
    assert not emit or m == tm
    wmap = lambda i, j: (0, j)
    wb_specs, wb_shapes = _wb_outputs(weights, emit, (d, tf), wmap, (d, D_FF))
    h, *wb = pl.pallas_call(
        _gate_up_kernel,
        grid=(m // tm, nf),
        in_specs=[pl.BlockSpec((tm, d), lambda i, j: (i, 0))]
        + [_wspec(w, (d, tf), wmap) for w in weights],
        out_specs=[pl.BlockSpec((tm, tf), lambda i, j: (i, j))] + wb_specs,
        out_shape=[jax.ShapeDtypeStruct((m, D_FF), BF16)] + wb_shapes,
        compiler_params=_params(2),
        name="gate_up",
    )(xb, *[w[0] for w in weights])
    return h, _as_weights(wb)


def _qkv_kernel(xb_ref, wq_ref, wk_ref, wv_ref, q_ref, kb_ref, vb_ref, kf_ref, vf_ref, *wbo,
                transposed_v):
    wbo = wbo or (None, None, None)
    xb = xb_ref[...]
    q = _dot(xb, _wload(wq_ref, wbo[0]))
    k = _dot(xb, _wload(wk_ref, wbo[1]))
    v = _dot(xb, _wload(wv_ref, wbo[2]))
    kf_ref[...] = k
    vf_ref[...] = v
    for hh in range(q_ref.shape[0]):
        cols = slice(hh * HEAD_DIM, (hh + 1) * HEAD_DIM)
        q_ref[hh] = q[:, cols].astype(BF16)
        kb_ref[hh] = k[:, cols].astype(BF16)
        if transposed_v:
            vb_ref[hh] = v[:, cols].T.astype(BF16)
        else:
            vb_ref[hh] = v[:, cols].astype(BF16)


def _qkv(xb, weights, *, tm, tn, transposed_v, emit):
    m, d = xb.shape
    nj = d // tn
    hb = tn // HEAD_DIM
    assert not emit or m == tm
    hm_spec = pl.BlockSpec((hb, tm, HEAD_DIM), lambda i, j: (j, i, 0))
    hm_shape = jax.ShapeDtypeStruct((N_HEADS, m, HEAD_DIM), BF16)
    if transposed_v:
        v_spec = pl.BlockSpec((hb, HEAD_DIM, tm), lambda i, j: (j, 0, i))
        v_shape = jax.ShapeDtypeStruct((N_HEADS, HEAD_DIM, m), BF16)
    else:
        v_spec, v_shape = hm_spec, hm_shape
    wmap = lambda i, j: (0, j)
    wb_specs, wb_shapes = _wb_outputs(weights, emit, (d, tn), wmap, (d, d))
    q, kb, vb, k, v, *wb = pl.pallas_call(
        functools.partial(_qkv_kernel, transposed_v=transposed_v),
        grid=(m // tm, nj),
        in_specs=[pl.BlockSpec((tm, d), lambda i, j: (i, 0))]
        + [_wspec(w, (d, tn), wmap) for w in weights],
        out_specs=[hm_spec, hm_spec, v_spec,
                   pl.BlockSpec((tm, tn), lambda i, j: (i, j)),
                   pl.BlockSpec((tm, tn), lambda i, j: (i, j))] + wb_specs,
        out_shape=[hm_shape, hm_shape, v_shape,
                   jax.ShapeDtypeStruct((m, d), F32),
                   jax.ShapeDtypeStruct((m, d), F32)] + wb_shapes,
        compiler_params=_params(2),
        name="qkv",
    )(xb, *[w[0] for w in weights])
    return q, kb, vb, k, v, _as_weights(wb)


def _split_heads_kernel(x_ref, o_ref):
    for h in range(N_HEADS):
        o_ref[:, h, :] = x_ref[:, h * HEAD_DIM:(h + 1) * HEAD_DIM]


def _split_heads(x, *, rows, tr):
    m, d = x.shape
    first = (m - rows) // tr
    return pl.pallas_call(
        _split_heads_kernel,
        grid=(rows // tr,),
        in_specs=[pl.BlockSpec((tr, d), lambda i: (first + i, 0))],
        out_specs=pl.BlockSpec((tr, N_HEADS, HEAD_DIM), lambda i: (i, 0, 0)),
        out_shape=jax.ShapeDtypeStruct((rows, N_HEADS, HEAD_DIM), x.dtype),
        compiler_params=_params(1),
        name="split_heads",
    )(x)


def _bias_kernel(rb_ref, chunk_ref, pair_ref):
    rb = rb_ref[...]
    hi = rb.astype(BF16)
    r1 = rb - hi.astype(F32)
    mid = r1.astype(BF16)
    lo = (r1 - mid.astype(F32)).astype(BF16)
    src = lax.broadcasted_iota(jnp.int32, (N_REL_PAD, T_PAD), 0)
    m = lax.broadcasted_iota(jnp.int32, (N_REL_PAD, T_PAD), 1)
    idx = jnp.clip(LEFT_ROWS + CHUNK - 1 - m, -MAX_REL, MAX_REL) + MAX_REL
    onehot = (src == idx).astype(BF16)
    t = (_dot(hi, onehot) + _dot(mid, onehot)) + _dot(lo, onehot)

    key = lax.broadcasted_iota(jnp.int32, (PAIR_BAND, PAIR), 0)
    qry = lax.broadcasted_iota(jnp.int32, (PAIR_BAND, PAIR), 1)
    in_band = ((qry < CHUNK) & (key < BAND)) | ((qry >= CHUNK) & (key >= CHUNK))
    for h in range(N_HEADS):
        rows = jnp.broadcast_to(t[h:h + 1, :], (PAIR, T_PAD))
        rows = pltpu.roll(rows, T_PAD - (CHUNK - 1), 1, stride=1, stride_axis=0)
        chunk_ref[h] = rows[:CHUNK, :BAND]
        pair_ref[h] = jnp.where(in_band, rows[:, :PAIR_BAND].T, -jnp.inf)


def _bias_tables(rel_bias):
    rb = jnp.pad(rel_bias, ((0, 0), (0, N_REL_PAD - N_REL)))
    return pl.pallas_call(
        _bias_kernel,
        out_shape=[jax.ShapeDtypeStruct((N_HEADS, CHUNK, BAND), F32),
                   jax.ShapeDtypeStruct((N_HEADS, PAIR_BAND, PAIR), F32)],
        name="rel_bias_tables",
    )(rb)


PAIRS = LEFT_ROWS // PAIR


def _attn_prompt_kernel(q_ref, kp_ref, kc_ref, vp_ref, vc_ref, bias_ref, o_ref, kwin, vwin):
    g = pl.program_id(0)
    key = lax.broadcasted_iota(jnp.int32, (PAIR_BAND, PAIR), 0)

    def head(h, first):
        kwin[0:LEFT_ROWS] = kp_ref[h]
        kwin[LEFT_ROWS:] = kc_ref[h]
        vwin[:, 0:LEFT_ROWS] = vp_ref[h]
        vwin[:, LEFT_ROWS:] = vc_ref[h]
        bias = bias_ref[h]
        for p in range(PAIRS):
            rows = slice(p * PAIR, (p + 1) * PAIR)
            band = slice(p * PAIR, p * PAIR + PAIR_BAND)
            s = _dot_t(kwin[band, :], q_ref[h, rows, :]) * SCALE + bias
            if first:
                s = jnp.where(key >= LEFT_ROWS - p * PAIR, s, -jnp.inf)
            e = jnp.exp(s - jnp.max(s, axis=0, keepdims=True))
            denom = jnp.sum(e, axis=0, keepdims=True)
            o = _dot(vwin[:, band], e.astype(BF16)) * (1.0 / denom)
            o_ref[h, rows, :] = o.T.astype(BF16)

    def heads(first):
        def body(h, carry):
            head(h, first)
            return carry
        lax.fori_loop(0, N_HEADS, body, 0)

    pl.when(g == 0)(functools.partial(heads, True))
    pl.when(g > 0)(functools.partial(heads, False))


def _attn_prompt(q, kb, vt, bias):
    _, m, _ = q.shape
    blk = (N_HEADS, LEFT_ROWS, HEAD_DIM)
    cur = pl.BlockSpec(blk, lambda g: (0, g, 0))
    prev = pl.BlockSpec(blk, lambda g: (0, jnp.maximum(g - 1, 0), 0))
    blk_t = (N_HEADS, HEAD_DIM, LEFT_ROWS)
    cur_t = pl.BlockSpec(blk_t, lambda g: (0, 0, g))
    prev_t = pl.BlockSpec(blk_t, lambda g: (0, 0, jnp.maximum(g - 1, 0)))
    return pl.pallas_call(
        _attn_prompt_kernel,
        grid=(m // LEFT_ROWS,),
        in_specs=[cur, prev, cur, prev_t, cur_t,
                  pl.BlockSpec((N_HEADS, PAIR_BAND, PAIR), lambda g: (0, 0, 0))],
        out_specs=cur,
        out_shape=jax.ShapeDtypeStruct(q.shape, BF16),
        scratch_shapes=[pltpu.VMEM((2 * LEFT_ROWS, HEAD_DIM), BF16),
                        pltpu.VMEM((HEAD_DIM, 2 * LEFT_ROWS), BF16)],
        compiler_params=_params(1),
        name="attn_prompt",
    )(q, kb, kb, vt, vt, bias)


def _attn_sample_kernel(q_ref, kc_ref, vc_ref, ck_ref, cv_ref, bias_ref, o_ref, kwin, vwin):
    for h in range(N_HEADS):
        kwin[0:LEFT_ROWS] = ck_ref[0, :, h, :].astype(BF16)
        kwin[LEFT_ROWS:] = kc_ref[h]
        vwin[0:LEFT_ROWS] = cv_ref[0, :, h, :].astype(BF16)
        vwin[LEFT_ROWS:] = vc_ref[h]
        s = _dot_t(q_ref[h], kwin[...]) * SCALE + bias_ref[h]
        e = jnp.exp(s - jnp.max(s, axis=-1, keepdims=True))
        p = e * (1.0 / jnp.sum(e, axis=-1, keepdims=True))
        o_ref[h] = _dot(p.astype(BF16), vwin[...]).astype(BF16)


def _attn_sample(q, kb, vb, cache_k, cache_v, bias):
    n = cache_k.shape[0]
    new = pl.BlockSpec((N_HEADS, CHUNK, HEAD_DIM), lambda s: (0, s, 0))
    cache = pl.BlockSpec((1, LEFT_ROWS, N_HEADS, HEAD_DIM), lambda s: (s, 0, 0, 0))
    return pl.pallas_call(
        _attn_sample_kernel,
        grid=(n,),
        in_specs=[new, new, new, cache, cache,
                  pl.BlockSpec((N_HEADS, CHUNK, BAND), lambda s: (0, 0, 0))],
        out_specs=new,
        out_shape=jax.ShapeDtypeStruct(q.shape, BF16),
        scratch_shapes=[pltpu.VMEM((BAND, HEAD_DIM), BF16),
                        pltpu.VMEM((BAND, HEAD_DIM), BF16)],
        compiler_params=_params(1),
        name="attn_sample",
    )(q, kb, vb, cache_k, cache_v, bias)


TM = 1024
TN_IN = 256
TK_OUT = 512
TF = 512
TR_SPLIT = 256


def _tn_qkv(emit):
    return 256 if emit else 512


def _f32_weights(w_in_a, w_out_a, w_kv, w_q, w_o, w_gate_up, w_down):
    return dict(
        w_in=tuple((w_in_a, 0, t * D_MODEL) for t in range(3)),
        w_out=(w_out_a, 0, 0),
        w_qkv=((w_q, 0, 0), (w_kv, 0, 0), (w_kv, 0, D_MODEL)),
        w_o=(w_o, 0, 0),
        w_gate_up=tuple(tuple((w_gate_up, l, t * D_FF) for t in range(2)) for l in range(DEPTH)),
        w_down=tuple((w_down, l, 0) for l in range(DEPTH)),
    )


def _trunk(x, hist, seq_len, cache, bias_tables, kv_rows, w, conv_w, ln_g, ln_b, *, emit):
    wb = dict(w_gate_up=[None] * DEPTH, w_down=[None] * DEPTH)

    def ffn(x, xb, layer):
        h, wb["w_gate_up"][layer] = _gate_up(xb, w["w_gate_up"][layer], tm=TM, tf=TF, emit=emit)
        x, xb, (wb["w_down"][layer],) = _pad1(_mm_res_ln(
            h, w["w_down"][layer], x, ln_g, ln_b, (layer, 1), tm=TM, tk=TK_OUT, emit=emit))
        return x, xb

    g, conv_state, wb["w_in"] = _conv_gate(x, w["w_in"], hist, conv_w, seq_len=seq_len,
                                           tm=TM, tn=TN_IN, emit=emit)
    x, xb, (wb["w_out"],) = _pad1(_mm_res_ln(g, w["w_out"], x, ln_g, ln_b, (0, 0),
                                            tm=TM, tk=TK_OUT, emit=emit))
    x, xb = ffn(x, xb, 0)

    bias_chunk, bias_pair = bias_tables
    q, kb, vb, k, v, wb["w_qkv"] = _qkv(xb, w["w_qkv"], tm=TM, tn=_tn_qkv(emit),
                                        transposed_v=cache is None, emit=emit)
    if cache is None:
        att = _attn_prompt(q, kb, vb, bias_pair)
    else:
        att = _attn_sample(q, kb, vb, cache[0], cache[1], bias_chunk)
    x, xb, (wb["w_o"],) = _pad1(_mm_res_ln(att, w["w_o"], x, ln_g, ln_b, (1, 0),
                                          tm=TM, tk=TK_OUT, emit=emit))
    x, _ = ffn(x, xb, 1)
    k = _split_heads(k, rows=kv_rows, tr=TR_SPLIT)
    v = _split_heads(v, rows=kv_rows, tr=TR_SPLIT)
    return (x, conv_state, k, v), wb


def _pad1(res):
    x, xb, wts = res
    return x, xb, (wts + (None,))[:1]


def kernel(x_prompt, x_sample, state_conv, cache_k, cache_v, w_in_a, conv_w, w_out_a, w_kv, w_q,
           w_o, rel_bias, ln_g, ln_b, w_gate_up, w_down):
    batch, seq, d = x_prompt.shape
    dec_batch, dec_seq, _ = x_sample.shape
    assert batch == 1 and DEPTH == 2 and dec_seq == CHUNK and cache_k.shape[1] == LEFT_ROWS
    assert dec_batch * dec_seq == TM
    bias_tables = _bias_tables(rel_bias[0])
    w_f32 = _f32_weights(w_in_a, w_out_a, w_kv, w_q, w_o, w_gate_up, w_down)

    (y_s, conv_s, k_s, v_s), w_bf16 = _trunk(
        x_sample.reshape(dec_batch * dec_seq, d), state_conv[0], dec_seq, (cache_k, cache_v),
        bias_tables, dec_batch * dec_seq, w_f32, conv_w, ln_g, ln_b, emit=True)
    conv_zero = jnp.zeros((batch, CONV_W - 1, d), x_prompt.dtype)
    (y_p, conv_p, k_p, v_p), _ = _trunk(
        x_prompt.reshape(seq, d), conv_zero, seq, None, bias_tables, LEFT_ROWS,
        w_bf16, conv_w, ln_g, ln_b, emit=False)

    kv_prompt = (batch, LEFT_ROWS, N_HEADS, HEAD_DIM)
    kv_sample = (dec_batch, dec_seq, N_HEADS, HEAD_DIM)
    return (y_p.reshape(batch, seq, d),
            y_s.reshape(dec_batch, dec_seq, d),
            conv_p.reshape(1, batch, CONV_W - 1, d),
            conv_s.reshape(1, dec_batch, CONV_W - 1, d),
            k_p.reshape(kv_prompt),
            v_p.reshape(kv_prompt),
            k_s.reshape(kv_sample),
            v_s.reshape(kv_sample))
```

```python
import functools

import jax
import jax.numpy as jnp
from jax import lax
from jax.experimental import pallas as pl
from jax.experimental.pallas import tpu as pltpu

D_MODEL = 2048
DEPTH = 2
CHUNK = 64
N_LEFT_CHUNKS = 8
LEFT_ROWS = N_LEFT_CHUNKS * CHUNK
BAND = LEFT_ROWS + CHUNK
N_HEADS = 16
HEAD_DIM = D_MODEL // N_HEADS
MAX_REL = 256
N_REL = 2 * MAX_REL + 1
CONV_W = 3
D_FF = 5632
ALPHA = (2.0 * DEPTH) ** 0.25
LN_EPS = 1e-5
SCALE = HEAD_DIM ** -0.5

LANES = 128
VMEM_LIMIT_BYTES = 56 * 1024 * 1024

PAIR = 2 * CHUNK
PAIR_BAND = BAND + CHUNK
N_REL_PAD = 640
T_PAD = 768

BF16 = jnp.bfloat16
F32 = jnp.float32


def _params(n_axes):
    return pltpu.CompilerParams(dimension_semantics=("arbitrary",) * n_axes,
                                vmem_limit_bytes=VMEM_LIMIT_BYTES)


def _dot(a, b):
    return jnp.dot(a, b, preferred_element_type=F32)


def _dot_t(a, b):
    return lax.dot_general(a, b, (((1,), (1,)), ((), ())), preferred_element_type=F32)


def _wspec(weight, blk, imap):
    arr, layer, first_col = weight
    first = first_col // blk[1]
    if arr.ndim == 3:
        return pl.BlockSpec((None,) + blk, lambda *g: (layer,) + _shift(imap(*g), first))
    return pl.BlockSpec(blk, lambda *g: _shift(imap(*g), first))


def _shift(idx, first):
    return (idx[0], idx[1] + first)


def _wload(w_ref, wb_ref):
    w = w_ref[...]
    if w.dtype != BF16:
        w = w.astype(BF16)
    if wb_ref is not None:
        wb_ref[...] = w
    return w


def _wb_outputs(weights, emit, blk, imap, shape):
    if not emit:
        return [], []
    return ([pl.BlockSpec(blk, imap)] * len(weights),
            [jax.ShapeDtypeStruct(shape, BF16)] * len(weights))


def _as_weights(arrs):
    return tuple((a, 0, 0) for a in arrs)


def _conv_gate_kernel(x_ref, wb_ref, wc_ref, wh_ref, hist_ref, cw_ref, g_ref, state_ref, *rest,
                      nseq, emit, carried):
    rest = list(rest)
    wbo = [rest.pop(0) for _ in range(3)] if emit else [None] * 3
    xb_ref = rest.pop(0)
    i = pl.program_id(0)
    j = pl.program_id(1)
    tm, tn = g_ref.shape
    ls = tm // nseq

    @pl.when(j == 0)
    def _():
        xb_ref[...] = x_ref[...].astype(BF16)

    xb = xb_ref[...]
    b = _dot(xb, _wload(wb_ref, wbo[0]))
    c = _dot(xb, _wload(wc_ref, wbo[1]))
    h = _dot(xb, _wload(wh_ref, wbo[2]))
    u = c * h

    if carried:
        carry_ref = rest.pop(0)

        @pl.when(i == 0)
        def _():
            carry_ref[j] = hist_ref[0]

        prev = carry_ref[j][None]
    else:
        prev = hist_ref[...]

    shape3 = (nseq, ls, tn)
    pos = lax.broadcasted_iota(jnp.int32, shape3, 1)
    u3 = u.reshape(shape3)
    p1 = pltpu.roll(u, 1, 0).reshape(shape3)
    p2 = pltpu.roll(u, 2, 0).reshape(shape3)
    h0 = prev[:, 0:1, :]
    h1 = prev[:, 1:2, :]
    p1 = jnp.where(pos == 0, h1, p1)
    p2 = jnp.where(pos == 0, h0, jnp.where(pos == 1, h1, p2))
    conv = cw_ref[0:1, :] * p2 + cw_ref[1:2, :] * p1 + cw_ref[2:3, :] * u3
    g_ref[...] = (b * conv.reshape(tm, tn)).astype(BF16)

    new_state = u3[:, ls - 2:ls, :]
    state_ref[...] = new_state
    if carried:
        carry_ref[j] = new_state[0]


def _conv_gate(x, weights, hist, conv_w, *, seq_len, tm, tn, emit):
    m, d = x.shape
    nj = d // tn
    carried = seq_len > tm
    nseq = 1 if carried else tm // seq_len
    assert not emit or m == tm
    hist_map = (lambda i, j: (0, 0, j)) if carried else (lambda i, j: (i, 0, j))
    scratch = [pltpu.VMEM((tm, d), BF16)]
    if carried:
        scratch.append(pltpu.VMEM((nj, CONV_W - 1, tn), F32))
    wmap = lambda i, j: (0, j)
    wb_specs, wb_shapes = _wb_outputs(weights, emit, (d, tn), wmap, (d, d))
    n_states = (m // tm) * nseq
    g, states, *wb = pl.pallas_call(
        functools.partial(_conv_gate_kernel, nseq=nseq, emit=emit, carried=carried),
        grid=(m // tm, nj),
        in_specs=[pl.BlockSpec((tm, d), lambda i, j: (i, 0))]
        + [_wspec(w, (d, tn), wmap) for w in weights]
        + [pl.BlockSpec((nseq, CONV_W - 1, tn), hist_map),
           pl.BlockSpec((None, CONV_W, tn), lambda i, j: (0, 0, j))],
        out_specs=[
            pl.BlockSpec((tm, tn), lambda i, j: (i, j)),
            pl.BlockSpec((nseq, CONV_W - 1, tn), lambda i, j: (i, 0, j)),
        ] + wb_specs,
        out_shape=[
            jax.ShapeDtypeStruct((m, d), BF16),
            jax.ShapeDtypeStruct((n_states, CONV_W - 1, d), F32),
        ] + wb_shapes,
        scratch_shapes=scratch,
        compiler_params=_params(2),
        name="conv_gate",
    )(x, *[w[0] for w in weights], hist, conv_w)
    return g, states[-hist.shape[0]:], _as_weights(wb)


LN_ROWS = 64


def _mm_res_ln_kernel(a_ref, w_ref, x_ref, g_ref, b_ref, out_ref, outb_ref, *wbo):
    k = pl.program_id(1)
    tm, d = out_ref.shape

    @pl.when(k == 0)
    def _():
        out_ref[...] = ALPHA * x_ref[...]

    if len(a_ref.shape) == 3:
        a = jnp.concatenate([a_ref[h] for h in range(a_ref.shape[0])], axis=1)
    else:
        a = a_ref[...]
    out_ref[...] += _dot(a, _wload(w_ref, wbo[0] if wbo else None))

    @pl.when(k == pl.num_programs(1) - 1)
    def _():
        gain = g_ref[...]
        bias = b_ref[...]

        for r in range(0, tm, LN_ROWS):
            rows = slice(r, r + LN_ROWS)
            y = out_ref[rows, :]
            mu = jnp.mean(y, axis=-1, keepdims=True)
            yc = y - mu
            var = jnp.mean(yc * yc, axis=-1, keepdims=True)
            z = yc * lax.rsqrt(var + LN_EPS) * gain + bias
            out_ref[rows, :] = z
            outb_ref[rows, :] = z.astype(BF16)


def _mm_res_ln(a, weight, x, ln_g, ln_b, ln_idx, *, tm, tk, emit):
    m, d = x.shape
    kdim = weight[0].shape[-2]
    assert not emit or m == tm
    if a.ndim == 3:
        a_spec = pl.BlockSpec((tk // HEAD_DIM, tm, HEAD_DIM), lambda i, k: (k, i, 0))
    else:
        a_spec = pl.BlockSpec((tm, tk), lambda i, k: (i, k))
    wmap = lambda i, k: (k, 0)
    wb_specs, wb_shapes = _wb_outputs((weight,), emit, (tk, d), wmap, (kdim, d))
    layer, sub = ln_idx
    ln_spec = pl.BlockSpec((None, None, 1, d), lambda i, k: (layer, sub, 0, 0))
    x_new, xb_new, *wb = pl.pallas_call(
        _mm_res_ln_kernel,
        grid=(m // tm, kdim // tk),
        in_specs=[
            a_spec,
            _wspec(weight, (tk, d), wmap),
            pl.BlockSpec((tm, d), lambda i, k: (i, 0)),
            ln_spec,
            ln_spec,
        ],
        out_specs=[
            pl.BlockSpec((tm, d), lambda i, k: (i, 0)),
            pl.BlockSpec((tm, d), lambda i, k: (i, 0)),
        ] + wb_specs,
        out_shape=[
            jax.ShapeDtypeStruct((m, d), F32),
            jax.ShapeDtypeStruct((m, d), BF16),
        ] + wb_shapes,
        compiler_params=_params(2),
        name="mm_res_ln",
    )(a, weight[0], x, ln_g.reshape(DEPTH, 2, 1, d), ln_b.reshape(DEPTH, 2, 1, d))
    return x_new, xb_new, _as_weights(wb)


def _gate_up_kernel(xb_ref, wg_ref, wu_ref, h_ref, *wbo):
    wbo = wbo or (None, None)
    xb = xb_ref[...]
    g = _dot(xb, _wload(wg_ref, wbo[0]))
    u = _dot(xb, _wload(wu_ref, wbo[1]))
    h_ref[...] = (jax.nn.silu(g) * u).astype(BF16)


def _gate_up(xb, weights, *, tm, tf, emit):
    m, d = xb.shape
    nf = D_FF // tf
    assert not emit or m == tm
    wmap = lambda i, j: (0, j)
    wb_specs, wb_shapes = _wb_outputs(weights, emit, (d, tf), wmap, (d, D_FF))
    h, *wb = pl.pallas_call(
        _gate_up_kernel,
        grid=(m // tm, nf),
        in_specs=[pl.BlockSpec((tm, d), lambda i, j: (i, 0))]
        + [_wspec(w, (d, tf), wmap) for w in weights],
        out_specs=[pl.BlockSpec((tm, tf), lambda i, j: (i, j))] + wb_specs,
        out_shape=[jax.ShapeDtypeStruct((m, D_FF), BF16)] + wb_shapes,
        compiler_params=_params(2),
        name="gate_up",
    )(xb, *[w[0] for w in weights])
    return h, _as_weights(wb)


def _qkv_kernel(xb_ref, wq_ref, wk_ref, wv_ref, q_ref, kb_ref, vb_ref, kf_ref, vf_ref, *wbo,
                transposed_v):
    wbo = wbo or (None, None, None)
    xb = xb_ref[...]
    q = _dot(xb, _wload(wq_ref, wbo[0]))
    k = _dot(xb, _wload(wk_ref, wbo[1]))
    v = _dot(xb, _wload(wv_ref, wbo[2]))
    kf_ref[...] = k
    vf_ref[...] = v
    for hh in range(q_ref.shape[0]):
        cols = slice(hh * HEAD_DIM, (hh + 1) * HEAD_DIM)
        q_ref[hh] = q[:, cols].astype(BF16)
        kb_ref[hh] = k[:, cols].astype(BF16)
        if transposed_v:
            vb_ref[hh] = v[:, cols].T.astype(BF16)
        else:
            vb_ref[hh] = v[:, cols].astype(BF16)


def _qkv(xb, weights, *, tm, tn, transposed_v, emit):
    m, d = xb.shape
    nj = d // tn
    hb = tn // HEAD_DIM
    assert not emit or m == tm
    hm_spec = pl.BlockSpec((hb, tm, HEAD_DIM), lambda i, j: (j, i, 0))
    hm_shape = jax.ShapeDtypeStruct((N_HEADS, m, HEAD_DIM), BF16)
    if transposed_v:
        v_spec = pl.BlockSpec((hb, HEAD_DIM, tm), lambda i, j: (j, 0, i))
        v_shape = jax.ShapeDtypeStruct((N_HEADS, HEAD_DIM, m), BF16)
    else:
        v_spec, v_shape = hm_spec, hm_shape
    wmap = lambda i, j: (0, j)
    wb_specs, wb_shapes = _wb_outputs(weights, emit, (d, tn), wmap, (d, d))
    q, kb, vb, k, v, *wb = pl.pallas_call(
        functools.partial(_qkv_kernel, transposed_v=transposed_v),
        grid=(m // tm, nj),
        in_specs=[pl.BlockSpec((tm, d), lambda i, j: (i, 0))]
        + [_wspec(w, (d, tn), wmap) for w in weights],
        out_specs=[hm_spec, hm_spec, v_spec,
                   pl.BlockSpec((tm, tn), lambda i, j: (i, j)),
                   pl.BlockSpec((tm, tn), lambda i, j: (i, j))] + wb_specs,
        out_shape=[hm_shape, hm_shape, v_shape,
                   jax.ShapeDtypeStruct((m, d), F32),
                   jax.ShapeDtypeStruct((m, d), F32)] + wb_shapes,
        compiler_params=_params(2),
        name="qkv",
    )(xb, *[w[0] for w in weights])
    return q, kb, vb, k, v, _as_weights(wb)


def _split_heads_kernel(x_ref, o_ref):
    for h in range(N_HEADS):
        o_ref[:, h, :] = x_ref[:, h * HEAD_DIM:(h + 1) * HEAD_DIM]


def _split_heads(x, *, rows, tr):
    m, d = x.shape
    first = (m - rows) // tr
    return pl.pallas_call(
        _split_heads_kernel,
        grid=(rows // tr,),
        in_specs=[pl.BlockSpec((tr, d), lambda i: (first + i, 0))],
        out_specs=pl.BlockSpec((tr, N_HEADS, HEAD_DIM), lambda i: (i, 0, 0)),
        out_shape=jax.ShapeDtypeStruct((rows, N_HEADS, HEAD_DIM), x.dtype),
        compiler_params=_params(1),
        name="split_heads",
    )(x)


def _bias_kernel(rb_ref, heads_ref, pair_ref):
    rb = rb_ref[...]
    hi = rb.astype(BF16)
    r1 = rb - hi.astype(F32)
    mid = r1.astype(BF16)
    lo = (r1 - mid.astype(F32)).astype(BF16)
    src = lax.broadcasted_iota(jnp.int32, (N_REL_PAD, T_PAD), 0)
    m = lax.broadcasted_iota(jnp.int32, (N_REL_PAD, T_PAD), 1)
    idx = jnp.clip(LEFT_ROWS + CHUNK - 1 - m, -MAX_REL, MAX_REL) + MAX_REL
    onehot = (src == idx).astype(BF16)
    t = (_dot(hi, onehot) + _dot(mid, onehot)) + _dot(lo, onehot)

    key = lax.broadcasted_iota(jnp.int32, (PAIR_BAND, PAIR), 0)
    qry = lax.broadcasted_iota(jnp.int32, (PAIR_BAND, PAIR), 1)
    in_band = ((qry < CHUNK) & (key < BAND)) | ((qry >= CHUNK) & (key >= CHUNK))
    chunk_rows = []
    for h in range(N_HEADS):
        rows = jnp.broadcast_to(t[h:h + 1, :], (PAIR, T_PAD))
        rows = pltpu.roll(rows, T_PAD - (CHUNK - 1), 1, stride=1, stride_axis=0)
        chunk_rows.append(rows[:CHUNK, :PAIR_BAND])
        pair_ref[h] = jnp.where(in_band, rows[:, :PAIR_BAND].T, -jnp.inf)
    for hp in range(N_HEADS // 2):
        both = jnp.concatenate(chunk_rows[2 * hp:2 * hp + 2], axis=0)
        heads_ref[hp] = both.T[:BAND]


def _bias_tables(rel_bias):
    rb = jnp.pad(rel_bias, ((0, 0), (0, N_REL_PAD - N_REL)))
    return pl.pallas_call(
        _bias_kernel,
        out_shape=[jax.ShapeDtypeStruct((N_HEADS // 2, BAND, 2 * CHUNK), F32),
                   jax.ShapeDtypeStruct((N_HEADS, PAIR_BAND, PAIR), F32)],
        name="rel_bias_tables",
    )(rb)


PAIRS = LEFT_ROWS // PAIR


def _attn_prompt_kernel(q_ref, kp_ref, kc_ref, vp_ref, vc_ref, bias_ref, o_ref):
    g = pl.program_id(0)
    key = lax.broadcasted_iota(jnp.int32, (PAIR_BAND, PAIR), 0)

    def head(h, first):
        bias = bias_ref[h]
        for p in range(PAIRS):
            rows = slice(p * PAIR, (p + 1) * PAIR)
            lo = slice(p * PAIR, LEFT_ROWS)
            hi = slice(0, (p + 1) * PAIR)
            kband = jnp.concatenate([kp_ref[h, lo, :], kc_ref[h, hi, :]], axis=0)
            vband = jnp.concatenate([vp_ref[h, :, lo], vc_ref[h, :, hi]], axis=1)
            s = _dot_t(kband, q_ref[h, rows, :]) * SCALE + bias
            if first:
                s = jnp.where(key >= LEFT_ROWS - p * PAIR, s, -jnp.inf)
            e = jnp.exp(s - jnp.max(s, axis=0, keepdims=True))
            denom = jnp.sum(e, axis=0, keepdims=True)
            o = _dot(vband, e.astype(BF16)) * (1.0 / denom)
            o_ref[h, rows, :] = o.T.astype(BF16)

    def heads(first):
        def body(h, carry):
            head(h, first)
            return carry
        lax.fori_loop(0, N_HEADS, body, 0, unroll=8)

    pl.when(g == 0)(functools.partial(heads, True))
    pl.when(g > 0)(functools.partial(heads, False))


def _attn_prompt(q, kb, vt, bias):
    _, m, _ = q.shape
    blk = (N_HEADS, LEFT_ROWS, HEAD_DIM)
    cur = pl.BlockSpec(blk, lambda g: (0, g, 0))
    prev = pl.BlockSpec(blk, lambda g: (0, jnp.maximum(g - 1, 0), 0))
    blk_t = (N_HEADS, HEAD_DIM, LEFT_ROWS)
    cur_t = pl.BlockSpec(blk_t, lambda g: (0, 0, g))
    prev_t = pl.BlockSpec(blk_t, lambda g: (0, 0, jnp.maximum(g - 1, 0)))
    return pl.pallas_call(
        _attn_prompt_kernel,
        grid=(m // LEFT_ROWS,),
        in_specs=[cur, prev, cur, prev_t, cur_t,
                  pl.BlockSpec((N_HEADS, PAIR_BAND, PAIR), lambda g: (0, 0, 0))],
        out_specs=cur,
        out_shape=jax.ShapeDtypeStruct(q.shape, BF16),
        compiler_params=_params(1),
        name="attn_prompt",
    )(q, kb, kb, vt, vt, bias)


def _cache_copies(ck_hbm, cv_hbm, kbuf, vbuf, sem, stream, slot):
    return [pltpu.make_async_copy(src.at[stream, :, h, :],
                                  buf.at[slot, :, h * HEAD_DIM:(h + 1) * HEAD_DIM],
                                  sem.at[slot])
            for src, buf in ((ck_hbm, kbuf), (cv_hbm, vbuf)) for h in range(N_HEADS)]


def _attn_sample_kernel(q_ref, kc_ref, vc_ref, ck_hbm, cv_hbm, bias_ref, o_ref, kbuf, vbuf, sem):
    n = pl.program_id(0)
    slot = n % 2
    copies = functools.partial(_cache_copies, ck_hbm, cv_hbm, kbuf, vbuf, sem)

    @pl.when(n == 0)
    def _():
        for cp in copies(0, 0):
            cp.start()

    @pl.when(n + 1 < pl.num_programs(0))
    def _():
        for cp in copies(n + 1, 1 - slot):
            cp.start()

    for cp in copies(n, slot):
        cp.wait()

    zeros = jnp.zeros((CHUNK, HEAD_DIM), BF16)
    for hp in range(N_HEADS // 2):
        h0, h1 = 2 * hp, 2 * hp + 1
        cols = slice(h0 * HEAD_DIM, (h1 + 1) * HEAD_DIM)
        knew = jnp.concatenate([kc_ref[h0], kc_ref[h1]], axis=1)
        vnew = jnp.concatenate([vc_ref[h0], vc_ref[h1]], axis=1)
        kband = jnp.concatenate([kbuf[slot, :, cols].astype(BF16), knew], axis=0)
        vband = jnp.concatenate([vbuf[slot, :, cols].astype(BF16), vnew], axis=0)
        qbd = jnp.concatenate([jnp.concatenate([q_ref[h0], zeros], axis=1),
                               jnp.concatenate([zeros, q_ref[h1]], axis=1)], axis=0)
        s = _dot_t(kband, qbd) * SCALE + bias_ref[hp]
        e = jnp.exp(s - jnp.max(s, axis=0, keepdims=True))
        denom = jnp.sum(e, axis=0, keepdims=True)
        o = lax.dot_general(vband, e.astype(BF16), (((0,), (0,)), ((), ())),
                            preferred_element_type=F32) * (1.0 / denom)
        o_ref[h0] = o[:HEAD_DIM].T[:CHUNK].astype(BF16)
        o_ref[h1] = o[HEAD_DIM:].T[CHUNK:].astype(BF16)


def _attn_sample(q, kb, vb, cache_k, cache_v, bias):
    n = cache_k.shape[0]
    new = pl.BlockSpec((N_HEADS, CHUNK, HEAD_DIM), lambda s: (0, s, 0))
    cache = pl.BlockSpec(memory_space=pl.ANY)
    return pl.pallas_call(
        _attn_sample_kernel,
        grid=(n,),
        in_specs=[new, new, new, cache, cache,
                  pl.BlockSpec((N_HEADS // 2, BAND, 2 * CHUNK), lambda s: (0, 0, 0))],
        out_specs=new,
        out_shape=jax.ShapeDtypeStruct(q.shape, BF16),
        scratch_shapes=[pltpu.VMEM((2, LEFT_ROWS, D_MODEL), F32),
                        pltpu.VMEM((2, LEFT_ROWS, D_MODEL), F32),
                        pltpu.SemaphoreType.DMA((2,))],
        compiler_params=_params(1),
        name="attn_sample",
    )(q, kb, vb, cache_k, cache_v, bias)


TM = 1024
TN_IN = 256
TK_OUT = 512
TF = 512
TR_SPLIT = 256


def _tn_qkv(emit):
    return 256 if emit else 512


def _f32_weights(w_in_a, w_out_a, w_kv, w_q, w_o, w_gate_up, w_down):
    return dict(
        w_in=tuple((w_in_a, 0, t * D_MODEL) for t in range(3)),
        w_out=(w_out_a, 0, 0),
        w_qkv=((w_q, 0, 0), (w_kv, 0, 0), (w_kv, 0, D_MODEL)),
        w_o=(w_o, 0, 0),
        w_gate_up=tuple(tuple((w_gate_up, l, t * D_FF) for t in range(2)) for l in range(DEPTH)),
        w_down=tuple((w_down, l, 0) for l in range(DEPTH)),
    )


def _trunk(x, hist, seq_len, cache, bias_tables, kv_rows, w, conv_w, ln_g, ln_b, *, emit):
    wb = dict(w_gate_up=[None] * DEPTH, w_down=[None] * DEPTH)

    def ffn(x, xb, layer):
        h, wb["w_gate_up"][layer] = _gate_up(xb, w["w_gate_up"][layer], tm=TM, tf=TF, emit=emit)
        x, xb, (wb["w_down"][layer],) = _pad1(_mm_res_ln(
            h, w["w_down"][layer], x, ln_g, ln_b, (layer, 1), tm=TM, tk=TK_OUT, emit=emit))
        return x, xb

    g, conv_state, wb["w_in"] = _conv_gate(x, w["w_in"], hist, conv_w, seq_len=seq_len,
                                           tm=TM, tn=TN_IN, emit=emit)
    x, xb, (wb["w_out"],) = _pad1(_mm_res_ln(g, w["w_out"], x, ln_g, ln_b, (0, 0),
                                            tm=TM, tk=TK_OUT, emit=emit))
    x, xb = ffn(x, xb, 0)

    bias_heads, bias_pair = bias_tables
    q, kb, vb, k, v, wb["w_qkv"] = _qkv(xb, w["w_qkv"], tm=TM, tn=_tn_qkv(emit),
                                        transposed_v=cache is None, emit=emit)
    if cache is None:
        att = _attn_prompt(q, kb, vb, bias_pair)
    else:
        att = _attn_sample(q, kb, vb, cache[0], cache[1], bias_heads)
    x, xb, (wb["w_o"],) = _pad1(_mm_res_ln(att, w["w_o"], x, ln_g, ln_b, (1, 0),
                                          tm=TM, tk=TK_OUT, emit=emit))
    x, _ = ffn(x, xb, 1)
    k = _split_heads(k, rows=kv_rows, tr=TR_SPLIT)
    v = _split_heads(v, rows=kv_rows, tr=TR_SPLIT)
    return (x, conv_state, k, v), wb


def _pad1(res):
    x, xb, wts = res
    return x, xb, (wts + (None,))[:1]


def kernel(x_prompt, x_sample, state_conv, cache_k, cache_v, w_in_a, conv_w, w_out_a, w_kv, w_q,
           w_o, rel_bias, ln_g, ln_b, w_gate_up, w_down):
    batch, seq, d = x_prompt.shape
    dec_batch, dec_seq, _ = x_sample.shape
    assert batch == 1 and DEPTH == 2 and dec_seq == CHUNK and cache_k.shape[1] == LEFT_ROWS
    assert dec_batch * dec_seq == TM
    bias_tables = _bias_tables(rel_bias[0])
    w_f32 = _f32_weights(w_in_a, w_out_a, w_kv, w_q, w_o, w_gate_up, w_down)

    (y_s, conv_s, k_s, v_s), w_bf16 = _trunk(
        x_sample.reshape(dec_batch * dec_seq, d), state_conv[0], dec_seq, (cache_k, cache_v),
        bias_tables, dec_batch * dec_seq, w_f32, conv_w, ln_g, ln_b, emit=True)
    conv_zero = jnp.zeros((batch, CONV_W - 1, d), x_prompt.dtype)
    (y_p, conv_p, k_p, v_p), _ = _trunk(
        x_prompt.reshape(seq, d), conv_zero, seq, None, bias_tables, LEFT_ROWS,
        w_bf16, conv_w, ln_g, ln_b, emit=False)

    kv_prompt = (batch, LEFT_ROWS, N_HEADS, HEAD_DIM)
    kv_sample = (dec_batch, dec_seq, N_HEADS, HEAD_DIM)
    return (y_p.reshape(batch, seq, d),
            y_s.reshape(dec_batch, dec_seq, d),
            conv_p.reshape(1, batch, CONV_W - 1, d),
            conv_s.reshape(1, dec_batch, CONV_W - 1, d),
            k_p.reshape(kv_prompt),
            v_p.reshape(kv_prompt),
            k_s.reshape(kv_sample),
            v_s.reshape(kv_sample))
```

```python
import functools

import jax
import jax.numpy as jnp
from jax import lax
from jax.experimental import pallas as pl
from jax.experimental.pallas import tpu as pltpu

D_MODEL = 2048
DEPTH = 2
CHUNK = 64
N_LEFT_CHUNKS = 8
LEFT_ROWS = N_LEFT_CHUNKS * CHUNK
BAND = LEFT_ROWS + CHUNK
N_HEADS = 16
HEAD_DIM = D_MODEL // N_HEADS
MAX_REL = 256
N_REL = 2 * MAX_REL + 1
CONV_W = 3
D_FF = 5632
ALPHA = (2.0 * DEPTH) ** 0.25
LN_EPS = 1e-5
SCALE = HEAD_DIM ** -0.5

LANES = 128
VMEM_LIMIT_BYTES = 56 * 1024 * 1024

PAIR = 2 * CHUNK
PAIR_BAND = BAND + CHUNK
N_REL_PAD = 640
T_PAD = 768

BF16 = jnp.bfloat16
F32 = jnp.float32


def _params(n_axes):
    return pltpu.CompilerParams(dimension_semantics=("arbitrary",) * n_axes,
                                vmem_limit_bytes=VMEM_LIMIT_BYTES)


def _dot(a, b):
    return jnp.dot(a, b, preferred_element_type=F32)


def _dot_t(a, b):
    return lax.dot_general(a, b, (((1,), (1,)), ((), ())), preferred_element_type=F32)


def _wspec(weight, blk, imap):
    arr, layer, first_col = weight
    first = first_col // blk[1]
    if arr.ndim == 3:
        return pl.BlockSpec((None,) + blk, lambda *g: (layer,) + _shift(imap(*g), first))
    return pl.BlockSpec(blk, lambda *g: _shift(imap(*g), first))


def _shift(idx, first):
    return (idx[0], idx[1] + first)


def _wload(w_ref, wb_ref):
    w = w_ref[...]
    if w.dtype != BF16:
        w = w.astype(BF16)
    if wb_ref is not None:
        wb_ref[...] = w
    return w


def _wb_outputs(weights, emit, blk, imap, shape):
    if not emit:
        return [], []
    return ([pl.BlockSpec(blk, imap)] * len(weights),
            [jax.ShapeDtypeStruct(shape, BF16)] * len(weights))


def _as_weights(arrs):
    return tuple((a, 0, 0) for a in arrs)


def _conv_gate_kernel(x_ref, wb_ref, wc_ref, wh_ref, hist_ref, cw_ref, g_ref, state_ref, *rest,
                      nseq, emit, carried):
    rest = list(rest)
    wbo = [rest.pop(0) for _ in range(3)] if emit else [None] * 3
    xb_ref = rest.pop(0)
    i = pl.program_id(0)
    j = pl.program_id(1)
    tm, tn = g_ref.shape
    ls = tm // nseq

    @pl.when(j == 0)
    def _():
        xb_ref[...] = x_ref[...].astype(BF16)

    xb = xb_ref[...]
    b = _dot(xb, _wload(wb_ref, wbo[0]))
    c = _dot(xb, _wload(wc_ref, wbo[1]))
    h = _dot(xb, _wload(wh_ref, wbo[2]))
    u = c * h

    if carried:
        carry_ref = rest.pop(0)

        @pl.when(i == 0)
        def _():
            carry_ref[j] = hist_ref[0]

        prev = carry_ref[j][None]
    else:
        prev = hist_ref[...]

    shape3 = (nseq, ls, tn)
    pos = lax.broadcasted_iota(jnp.int32, shape3, 1)
    u3 = u.reshape(shape3)
    p1 = pltpu.roll(u, 1, 0).reshape(shape3)
    p2 = pltpu.roll(u, 2, 0).reshape(shape3)
    h0 = prev[:, 0:1, :]
    h1 = prev[:, 1:2, :]
    p1 = jnp.where(pos == 0, h1, p1)
    p2 = jnp.where(pos == 0, h0, jnp.where(pos == 1, h1, p2))
    conv = cw_ref[0:1, :] * p2 + cw_ref[1:2, :] * p1 + cw_ref[2:3, :] * u3
    g_ref[...] = (b * conv.reshape(tm, tn)).astype(BF16)

    new_state = u3[:, ls - 2:ls, :]
    state_ref[...] = new_state
    if carried:
        carry_ref[j] = new_state[0]


def _conv_gate(x, weights, hist, conv_w, *, seq_len, tm, tn, emit):
    m, d = x.shape
    nj = d // tn
    carried = seq_len > tm
    nseq = 1 if carried else tm // seq_len
    assert not emit or m == tm
    hist_map = (lambda i, j: (0, 0, j)) if carried else (lambda i, j: (i, 0, j))
    scratch = [pltpu.VMEM((tm, d), BF16)]
    if carried:
        scratch.append(pltpu.VMEM((nj, CONV_W - 1, tn), F32))
    wmap = lambda i, j: (0, j)
    wb_specs, wb_shapes = _wb_outputs(weights, emit, (d, tn), wmap, (d, d))
    n_states = (m // tm) * nseq
    g, states, *wb = pl.pallas_call(
        functools.partial(_conv_gate_kernel, nseq=nseq, emit=emit, carried=carried),
        grid=(m // tm, nj),
        in_specs=[pl.BlockSpec((tm, d), lambda i, j: (i, 0))]
        + [_wspec(w, (d, tn), wmap) for w in weights]
        + [pl.BlockSpec((nseq, CONV_W - 1, tn), hist_map),
           pl.BlockSpec((None, CONV_W, tn), lambda i, j: (0, 0, j))],
        out_specs=[
            pl.BlockSpec((tm, tn), lambda i, j: (i, j)),
            pl.BlockSpec((nseq, CONV_W - 1, tn), lambda i, j: (i, 0, j)),
        ] + wb_specs,
        out_shape=[
            jax.ShapeDtypeStruct((m, d), BF16),
            jax.ShapeDtypeStruct((n_states, CONV_W - 1, d), F32),
        ] + wb_shapes,
        scratch_shapes=scratch,
        compiler_params=_params(2),
        name="conv_gate",
    )(x, *[w[0] for w in weights], hist, conv_w)
    return g, states[-hist.shape[0]:], _as_weights(wb)


LN_ROWS = 64


def _layer_norm_rows(y, gain, bias, out_ref, outb_ref, row0=0):
    for r in range(0, y.shape[0], LN_ROWS):
        yr = y[r:r + LN_ROWS]
        mu = jnp.mean(yr, axis=-1, keepdims=True)
        yc = yr - mu
        var = jnp.mean(yc * yc, axis=-1, keepdims=True)
        z = yc * lax.rsqrt(var + LN_EPS) * gain + bias
        rows = slice(row0 + r, row0 + r + LN_ROWS)
        out_ref[rows, :] = z
        outb_ref[rows, :] = z.astype(BF16)


def _mm_res_ln_kernel(a_ref, w_ref, x_ref, g_ref, b_ref, out_ref, outb_ref, *wbo):
    k = pl.program_id(1)
    tm, d = out_ref.shape

    @pl.when(k == 0)
    def _():
        out_ref[...] = ALPHA * x_ref[...]

    if len(a_ref.shape) == 3:
        a = jnp.concatenate([a_ref[h] for h in range(a_ref.shape[0])], axis=1)
    else:
        a = a_ref[...]
    w = _wload(w_ref, wbo[0] if wbo else None)
    last = pl.num_programs(1) - 1

    @pl.when(k < last)
    def _():
        out_ref[...] += _dot(a, w)

    @pl.when(k == last)
    def _():
        half = tm // 2
        for r in (0, half):
            y = out_ref[r:r + half, :] + _dot(a[r:r + half], w)
            _layer_norm_rows(y, g_ref[...], b_ref[...], out_ref, outb_ref, r)


def _mm_res_ln(a, weight, x, ln_g, ln_b, ln_idx, *, tm, tk, emit):
    m, d = x.shape
    kdim = weight[0].shape[-2]
    assert not emit or m == tm
    if a.ndim == 3:
        a_spec = pl.BlockSpec((tk // HEAD_DIM, tm, HEAD_DIM), lambda i, k: (k, i, 0))
    else:
        a_spec = pl.BlockSpec((tm, tk), lambda i, k: (i, k))
    wmap = lambda i, k: (k, 0)
    wb_specs, wb_shapes = _wb_outputs((weight,), emit, (tk, d), wmap, (kdim, d))
    layer, sub = ln_idx
    ln_spec = pl.BlockSpec((None, None, 1, d), lambda i, k: (layer, sub, 0, 0))
    x_new, xb_new, *wb = pl.pallas_call(
        _mm_res_ln_kernel,
        grid=(m // tm, kdim // tk),
        in_specs=[
            a_spec,
            _wspec(weight, (tk, d), wmap),
            pl.BlockSpec((tm, d), lambda i, k: (i, 0)),
            ln_spec,
            ln_spec,
        ],
        out_specs=[
            pl.BlockSpec((tm, d), lambda i, k: (i, 0)),
            pl.BlockSpec((tm, d), lambda i, k: (i, 0)),
        ] + wb_specs,
        out_shape=[
            jax.ShapeDtypeStruct((m, d), F32),
            jax.ShapeDtypeStruct((m, d), BF16),
        ] + wb_shapes,
        compiler_params=_params(2),
        name="mm_res_ln",
    )(a, weight[0], x, ln_g.reshape(DEPTH, 2, 1, d), ln_b.reshape(DEPTH, 2, 1, d))
    return x_new, xb_new, _as_weights(wb)


def _proj_res_ln_kernel(a_ref, w_ref, x_ref, g_ref, b_ref, out_ref, outb_ref):
    if len(a_ref.shape) == 3:
        a = jnp.concatenate([a_ref[h] for h in range(a_ref.shape[0])], axis=1)
    else:
        a = a_ref[...]
    y = ALPHA * x_ref[...] + _dot(a, w_ref[...])
    _layer_norm_rows(y, g_ref[...], b_ref[...], out_ref, outb_ref)


def _proj_res_ln(a, weight, x, ln_g, ln_b, ln_idx, *, tm):
    m, d = x.shape
    w = weight[0]
    assert w.shape == (d, d) and w.dtype == BF16
    if a.ndim == 3:
        a_spec = pl.BlockSpec((N_HEADS, tm, HEAD_DIM), lambda i: (0, i, 0))
    else:
        a_spec = pl.BlockSpec((tm, d), lambda i: (i, 0))
    layer, sub = ln_idx
    ln_spec = pl.BlockSpec((None, None, 1, d), lambda i: (layer, sub, 0, 0))
    return pl.pallas_call(
        _proj_res_ln_kernel,
        grid=(m // tm,),
        in_specs=[
            a_spec,
            pl.BlockSpec((d, d), lambda i: (0, 0)),
            pl.BlockSpec((tm, d), lambda i: (i, 0)),
            ln_spec,
            ln_spec,
        ],
        out_specs=[
            pl.BlockSpec((tm, d), lambda i: (i, 0)),
            pl.BlockSpec((tm, d), lambda i: (i, 0)),
        ],
        out_shape=[
            jax.ShapeDtypeStruct((m, d), F32),
            jax.ShapeDtypeStruct((m, d), BF16),
        ],
        compiler_params=_params(1),
        name="proj_res_ln",
    )(a, w, x, ln_g.reshape(DEPTH, 2, 1, d), ln_b.reshape(DEPTH, 2, 1, d))


def _gate_up_kernel(xb_ref, wg_ref, wu_ref, h_ref, *wbo):
    wbo = wbo or (None, None)
    xb = xb_ref[...]
    g = _dot(xb, _wload(wg_ref, wbo[0]))
    u = _dot(xb, _wload(wu_ref, wbo[1]))
    h_ref[...] = (jax.nn.silu(g) * u).astype(BF16)


def _gate_up(xb, weights, *, tm, tf, emit):
    m, d = xb.shape
    nf = D_FF // tf
    assert not emit or m == tm
    wmap = lambda i, j: (0, j)
    wb_specs, wb_shapes = _wb_outputs(weights, emit, (d, tf), wmap, (d, D_FF))
    h, *wb = pl.pallas_call(
        _gate_up_kernel,
        grid=(m // tm, nf),
        in_specs=[pl.BlockSpec((tm, d), lambda i, j: (i, 0))]
        + [_wspec(w, (d, tf), wmap) for w in weights],
        out_specs=[pl.BlockSpec((tm, tf), lambda i, j: (i, j))] + wb_specs,
        out_shape=[jax.ShapeDtypeStruct((m, D_FF), BF16)] + wb_shapes,
        compiler_params=_params(2),
        name="gate_up",
    )(xb, *[w[0] for w in weights])
    return h, _as_weights(wb)


def _qkv_kernel(xb_ref, wq_ref, wk_ref, wv_ref, q_ref, kb_ref, vb_ref, kf_ref, vf_ref, *wbo,
                transposed_v):
    wbo = wbo or (None, None, None)
    xb = xb_ref[...]
    q = _dot(xb, _wload(wq_ref, wbo[0]))
    k = _dot(xb, _wload(wk_ref, wbo[1]))
    v = _dot(xb, _wload(wv_ref, wbo[2]))
    kf_ref[...] = k
    vf_ref[...] = v
    for hh in range(q_ref.shape[0]):
        cols = slice(hh * HEAD_DIM, (hh + 1) * HEAD_DIM)
        q_ref[hh] = q[:, cols].astype(BF16)
        kb_ref[hh] = k[:, cols].astype(BF16)
        if transposed_v:
            vb_ref[hh] = v[:, cols].T.astype(BF16)
        else:
            vb_ref[hh] = v[:, cols].astype(BF16)


def _qkv(xb, weights, *, tm, tn, transposed_v, emit):
    m, d = xb.shape
    nj = d // tn
    hb = tn // HEAD_DIM
    assert not emit or m == tm
    hm_spec = pl.BlockSpec((hb, tm, HEAD_DIM), lambda i, j: (j, i, 0))
    hm_shape = jax.ShapeDtypeStruct((N_HEADS, m, HEAD_DIM), BF16)
    if transposed_v:
        v_spec = pl.BlockSpec((hb, HEAD_DIM, tm), lambda i, j: (j, 0, i))
        v_shape = jax.ShapeDtypeStruct((N_HEADS, HEAD_DIM, m), BF16)
    else:
        v_spec, v_shape = hm_spec, hm_shape
    wmap = lambda i, j: (0, j)
    wb_specs, wb_shapes = _wb_outputs(weights, emit, (d, tn), wmap, (d, d))
    q, kb, vb, k, v, *wb = pl.pallas_call(
        functools.partial(_qkv_kernel, transposed_v=transposed_v),
        grid=(m // tm, nj),
        in_specs=[pl.BlockSpec((tm, d), lambda i, j: (i, 0))]
        + [_wspec(w, (d, tn), wmap) for w in weights],
        out_specs=[hm_spec, hm_spec, v_spec,
                   pl.BlockSpec((tm, tn), lambda i, j: (i, j)),
                   pl.BlockSpec((tm, tn), lambda i, j: (i, j))] + wb_specs,
        out_shape=[hm_shape, hm_shape, v_shape,
                   jax.ShapeDtypeStruct((m, d), F32),
                   jax.ShapeDtypeStruct((m, d), F32)] + wb_shapes,
        compiler_params=_params(2),
        name="qkv",
    )(xb, *[w[0] for w in weights])
    return q, kb, vb, k, v, _as_weights(wb)


def _split_heads_kernel(x_ref, o_ref):
    for h in range(N_HEADS):
        o_ref[:, h, :] = x_ref[:, h * HEAD_DIM:(h + 1) * HEAD_DIM]


def _split_heads(x, *, rows, tr):
    m, d = x.shape
    first = (m - rows) // tr
    return pl.pallas_call(
        _split_heads_kernel,
        grid=(rows // tr,),
        in_specs=[pl.BlockSpec((tr, d), lambda i: (first + i, 0))],
        out_specs=pl.BlockSpec((tr, N_HEADS, HEAD_DIM), lambda i: (i, 0, 0)),
        out_shape=jax.ShapeDtypeStruct((rows, N_HEADS, HEAD_DIM), x.dtype),
        compiler_params=_params(1),
        name="split_heads",
    )(x)


def _bias_kernel(rb_ref, heads_ref, pair_ref):
    rb = rb_ref[...]
    hi = rb.astype(BF16)
    r1 = rb - hi.astype(F32)
    mid = r1.astype(BF16)
    lo = (r1 - mid.astype(F32)).astype(BF16)
    src = lax.broadcasted_iota(jnp.int32, (N_REL_PAD, T_PAD), 0)
    m = lax.broadcasted_iota(jnp.int32, (N_REL_PAD, T_PAD), 1)
    idx = jnp.clip(LEFT_ROWS + CHUNK - 1 - m, -MAX_REL, MAX_REL) + MAX_REL
    onehot = (src == idx).astype(BF16)
    t = (_dot(hi, onehot) + _dot(mid, onehot)) + _dot(lo, onehot)

    key = lax.broadcasted_iota(jnp.int32, (PAIR_BAND, PAIR), 0)
    qry = lax.broadcasted_iota(jnp.int32, (PAIR_BAND, PAIR), 1)
    in_band = ((qry < CHUNK) & (key < BAND)) | ((qry >= CHUNK) & (key >= CHUNK))
    chunk_rows = []
    for h in range(N_HEADS):
        rows = jnp.broadcast_to(t[h:h + 1, :], (PAIR, T_PAD))
        rows = pltpu.roll(rows, T_PAD - (CHUNK - 1), 1, stride=1, stride_axis=0)
        chunk_rows.append(rows[:CHUNK, :PAIR_BAND])
        pair_ref[h] = jnp.where(in_band, rows[:, :PAIR_BAND].T, -jnp.inf)
    for hp in range(N_HEADS // 2):
        both = jnp.concatenate(chunk_rows[2 * hp:2 * hp + 2], axis=0)
        heads_ref[hp] = both.T[:BAND]


def _bias_tables(rel_bias):
    rb = jnp.pad(rel_bias, ((0, 0), (0, N_REL_PAD - N_REL)))
    return pl.pallas_call(
        _bias_kernel,
        out_shape=[jax.ShapeDtypeStruct((N_HEADS // 2, BAND, 2 * CHUNK), F32),
                   jax.ShapeDtypeStruct((N_HEADS, PAIR_BAND, PAIR), F32)],
        name="rel_bias_tables",
    )(rb)


PAIRS = LEFT_ROWS // PAIR


def _attn_prompt_kernel(q_ref, kp_ref, kc_ref, vp_ref, vc_ref, bias_ref, o_ref):
    g = pl.program_id(0)
    key = lax.broadcasted_iota(jnp.int32, (PAIR_BAND, PAIR), 0)

    def head(h, first):
        bias = bias_ref[h]
        for p in range(PAIRS):
            rows = slice(p * PAIR, (p + 1) * PAIR)
            lo = slice(p * PAIR, LEFT_ROWS)
            hi = slice(0, (p + 1) * PAIR)
            kband = jnp.concatenate([kp_ref[h, lo, :], kc_ref[h, hi, :]], axis=0)
            vband = jnp.concatenate([vp_ref[h, :, lo], vc_ref[h, :, hi]], axis=1)
            s = _dot_t(kband, q_ref[h, rows, :]) * SCALE + bias
            if first:
                s = jnp.where(key >= LEFT_ROWS - p * PAIR, s, -jnp.inf)
            e = jnp.exp(s - jnp.max(s, axis=0, keepdims=True))
            denom = jnp.sum(e, axis=0, keepdims=True)
            o = _dot(vband, e.astype(BF16)) * (1.0 / denom)
            o_ref[h, rows, :] = o.T.astype(BF16)

    def heads(first):
        def body(h, carry):
            head(h, first)
            return carry
        lax.fori_loop(0, N_HEADS, body, 0, unroll=8)

    pl.when(g == 0)(functools.partial(heads, True))
    pl.when(g > 0)(functools.partial(heads, False))


def _attn_prompt(q, kb, vt, bias):
    _, m, _ = q.shape
    blk = (N_HEADS, LEFT_ROWS, HEAD_DIM)
    cur = pl.BlockSpec(blk, lambda g: (0, g, 0))
    prev = pl.BlockSpec(blk, lambda g: (0, jnp.maximum(g - 1, 0), 0))
    blk_t = (N_HEADS, HEAD_DIM, LEFT_ROWS)
    cur_t = pl.BlockSpec(blk_t, lambda g: (0, 0, g))
    prev_t = pl.BlockSpec(blk_t, lambda g: (0, 0, jnp.maximum(g - 1, 0)))
    return pl.pallas_call(
        _attn_prompt_kernel,
        grid=(m // LEFT_ROWS,),
        in_specs=[cur, prev, cur, prev_t, cur_t,
                  pl.BlockSpec((N_HEADS, PAIR_BAND, PAIR), lambda g: (0, 0, 0))],
        out_specs=cur,
        out_shape=jax.ShapeDtypeStruct(q.shape, BF16),
        compiler_params=_params(1),
        name="attn_prompt",
    )(q, kb, kb, vt, vt, bias)


def _cache_copies(ck_hbm, cv_hbm, kbuf, vbuf, sem, stream, slot):
    return [pltpu.make_async_copy(src.at[stream, :, h, :],
                                  buf.at[slot, :, h * HEAD_DIM:(h + 1) * HEAD_DIM],
                                  sem.at[slot])
            for src, buf in ((ck_hbm, kbuf), (cv_hbm, vbuf)) for h in range(N_HEADS)]


def _attn_sample_kernel(q_ref, kc_ref, vc_ref, ck_hbm, cv_hbm, bias_ref, o_ref, kbuf, vbuf, sem):
    n = pl.program_id(0)
    slot = n % 2
    copies = functools.partial(_cache_copies, ck_hbm, cv_hbm, kbuf, vbuf, sem)

    @pl.when(n == 0)
    def _():
        for cp in copies(0, 0):
            cp.start()

    @pl.when(n + 1 < pl.num_programs(0))
    def _():
        for cp in copies(n + 1, 1 - slot):
            cp.start()

    for cp in copies(n, slot):
        cp.wait()

    zeros = jnp.zeros((CHUNK, HEAD_DIM), BF16)
    for hp in range(N_HEADS // 2):
        h0, h1 = 2 * hp, 2 * hp + 1
        cols = slice(h0 * HEAD_DIM, (h1 + 1) * HEAD_DIM)
        knew = jnp.concatenate([kc_ref[h0], kc_ref[h1]], axis=1)
        vnew = jnp.concatenate([vc_ref[h0], vc_ref[h1]], axis=1)
        kband = jnp.concatenate([kbuf[slot, :, cols].astype(BF16), knew], axis=0)
        vband = jnp.concatenate([vbuf[slot, :, cols].astype(BF16), vnew], axis=0)
        qbd = jnp.concatenate([jnp.concatenate([q_ref[h0], zeros], axis=1),
                               jnp.concatenate([zeros, q_ref[h1]], axis=1)], axis=0)
        s = _dot_t(kband, qbd) * SCALE + bias_ref[hp]
        e = jnp.exp(s - jnp.max(s, axis=0, keepdims=True))
        denom = jnp.sum(e, axis=0, keepdims=True)
        o = lax.dot_general(vband, e.astype(BF16), (((0,), (0,)), ((), ())),
                            preferred_element_type=F32) * (1.0 / denom)
        o_ref[h0] = o[:HEAD_DIM].T[:CHUNK].astype(BF16)
        o_ref[h1] = o[HEAD_DIM:].T[CHUNK:].astype(BF16)


def _attn_sample(q, kb, vb, cache_k, cache_v, bias):
    n = cache_k.shape[0]
    new = pl.BlockSpec((N_HEADS, CHUNK, HEAD_DIM), lambda s: (0, s, 0))
    cache = pl.BlockSpec(memory_space=pl.ANY)
    return pl.pallas_call(
        _attn_sample_kernel,
        grid=(n,),
        in_specs=[new, new, new, cache, cache,
                  pl.BlockSpec((N_HEADS // 2, BAND, 2 * CHUNK), lambda s: (0, 0, 0))],
        out_specs=new,
        out_shape=jax.ShapeDtypeStruct(q.shape, BF16),
        scratch_shapes=[pltpu.VMEM((2, LEFT_ROWS, D_MODEL), F32),
                        pltpu.VMEM((2, LEFT_ROWS, D_MODEL), F32),
                        pltpu.SemaphoreType.DMA((2,))],
        compiler_params=_params(1),
        name="attn_sample",
    )(q, kb, vb, cache_k, cache_v, bias)


TM = 1024
TK_OUT = 512
TM_PROJ = 512
TF = 512
TR_SPLIT = 256


def _tn_three(emit):
    return 256 if emit else 512


def _f32_weights(w_in_a, w_out_a, w_kv, w_q, w_o, w_gate_up, w_down):
    return dict(
        w_in=tuple((w_in_a, 0, t * D_MODEL) for t in range(3)),
        w_out=(w_out_a, 0, 0),
        w_qkv=((w_q, 0, 0), (w_kv, 0, 0), (w_kv, 0, D_MODEL)),
        w_o=(w_o, 0, 0),
        w_gate_up=tuple(tuple((w_gate_up, l, t * D_FF) for t in range(2)) for l in range(DEPTH)),
        w_down=tuple((w_down, l, 0) for l in range(DEPTH)),
    )


def _trunk(x, hist, seq_len, cache, bias_tables, kv_rows, w, conv_w, ln_g, ln_b, *, emit):
    wb = dict(w_gate_up=[None] * DEPTH, w_down=[None] * DEPTH)

    def ffn(x, xb, layer):
        h, wb["w_gate_up"][layer] = _gate_up(xb, w["w_gate_up"][layer], tm=TM, tf=TF, emit=emit)
        x, xb, (wb["w_down"][layer],) = _pad1(_mm_res_ln(
            h, w["w_down"][layer], x, ln_g, ln_b, (layer, 1), tm=TM, tk=TK_OUT, emit=emit))
        return x, xb

    g, conv_state, wb["w_in"] = _conv_gate(x, w["w_in"], hist, conv_w, seq_len=seq_len,
                                           tm=TM, tn=_tn_three(emit), emit=emit)
    def proj(a, name, ln_idx):
        if emit:
            x_new, xb_new, (wb[name],) = _pad1(_mm_res_ln(a, w[name], x, ln_g, ln_b, ln_idx,
                                                          tm=TM, tk=TK_OUT, emit=True))
            return x_new, xb_new
        return _proj_res_ln(a, w[name], x, ln_g, ln_b, ln_idx, tm=TM_PROJ)

    x, xb = proj(g, "w_out", (0, 0))
    x, xb = ffn(x, xb, 0)

    bias_heads, bias_pair = bias_tables
    q, kb, vb, k, v, wb["w_qkv"] = _qkv(xb, w["w_qkv"], tm=TM, tn=_tn_three(emit),
                                        transposed_v=cache is None, emit=emit)
    if cache is None:
        att = _attn_prompt(q, kb, vb, bias_pair)
    else:
        att = _attn_sample(q, kb, vb, cache[0], cache[1], bias_heads)
    x, xb = proj(att, "w_o", (1, 0))
    x, _ = ffn(x, xb, 1)
    k = _split_heads(k, rows=kv_rows, tr=TR_SPLIT)
    v = _split_heads(v, rows=kv_rows, tr=TR_SPLIT)
    return (x, conv_state, k, v), wb


def _pad1(res):
    x, xb, wts = res
    return x, xb, (wts + (None,))[:1]


def kernel(x_prompt, x_sample, state_conv, cache_k, cache_v, w_in_a, conv_w, w_out_a, w_kv, w_q,
           w_o, rel_bias, ln_g, ln_b, w_gate_up, w_down):
    batch, seq, d = x_prompt.shape
    dec_batch, dec_seq, _ = x_sample.shape
    assert batch == 1 and DEPTH == 2 and dec_seq == CHUNK and cache_k.shape[1] == LEFT_ROWS
    assert dec_batch * dec_seq == TM
    bias_tables = _bias_tables(rel_bias[0])
    w_f32 = _f32_weights(w_in_a, w_out_a, w_kv, w_q, w_o, w_gate_up, w_down)

    (y_s, conv_s, k_s, v_s), w_bf16 = _trunk(
        x_sample.reshape(dec_batch * dec_seq, d), state_conv[0], dec_seq, (cache_k, cache_v),
        bias_tables, dec_batch * dec_seq, w_f32, conv_w, ln_g, ln_b, emit=True)
    conv_zero = jnp.zeros((batch, CONV_W - 1, d), x_prompt.dtype)
    (y_p, conv_p, k_p, v_p), _ = _trunk(
        x_prompt.reshape(seq, d), conv_zero, seq, None, bias_tables, LEFT_ROWS,
        w_bf16, conv_w, ln_g, ln_b, emit=False)

    kv_prompt = (batch, LEFT_ROWS, N_HEADS, HEAD_DIM)
    kv_sample = (dec_batch, dec_seq, N_HEADS, HEAD_DIM)
    return (y_p.reshape(batch, seq, d),
            y_s.reshape(dec_batch, dec_seq, d),
            conv_p.reshape(1, batch, CONV_W - 1, d),
            conv_s.reshape(1, dec_batch, CONV_W - 1, d),
            k_p.reshape(kv_prompt),
            v_p.reshape(kv_prompt),
            k_s.reshape(kv_sample),
            v_s.reshape(kv_sample))
```

```python
import functools

import jax
import jax.numpy as jnp
from jax import lax
from jax.experimental import pallas as pl
from jax.experimental.pallas import tpu as pltpu

D_MODEL = 2048
DEPTH = 2
CHUNK = 64
N_LEFT_CHUNKS = 8
LEFT_ROWS = N_LEFT_CHUNKS * CHUNK
BAND = LEFT_ROWS + CHUNK
N_HEADS = 16
HEAD_DIM = D_MODEL // N_HEADS
MAX_REL = 256
N_REL = 2 * MAX_REL + 1
CONV_W = 3
D_FF = 5632
ALPHA = (2.0 * DEPTH) ** 0.25
LN_EPS = 1e-5
SCALE = HEAD_DIM ** -0.5

LANES = 128
VMEM_LIMIT_BYTES = 56 * 1024 * 1024

PAIR = 2 * CHUNK
PAIR_BAND = BAND + CHUNK
N_REL_PAD = 640
T_PAD = 768

BF16 = jnp.bfloat16
F32 = jnp.float32


def _params(n_axes):
    return pltpu.CompilerParams(dimension_semantics=("arbitrary",) * n_axes,
                                vmem_limit_bytes=VMEM_LIMIT_BYTES)


def _dot(a, b):
    return jnp.dot(a, b, preferred_element_type=F32)


def _dot_t(a, b):
    return lax.dot_general(a, b, (((1,), (1,)), ((), ())), preferred_element_type=F32)


def _wspec(weight, blk, imap):
    arr, layer, first_col = weight
    first = first_col // blk[1]
    if arr.ndim == 3:
        return pl.BlockSpec((None,) + blk, lambda *g: (layer,) + _shift(imap(*g), first))
    return pl.BlockSpec(blk, lambda *g: _shift(imap(*g), first))


def _shift(idx, first):
    return (idx[0], idx[1] + first)


def _wload(w_ref, wb_ref):
    w = w_ref[...]
    if w.dtype != BF16:
        w = w.astype(BF16)
    if wb_ref is not None:
        wb_ref[...] = w
    return w


def _wb_outputs(weights, emit, blk, imap, shape):
    if not emit:
        return [], []
    return ([pl.BlockSpec(blk, imap)] * len(weights),
            [jax.ShapeDtypeStruct(shape, BF16)] * len(weights))


def _as_weights(arrs):
    return tuple((a, 0, 0) for a in arrs)


def _conv_gate_kernel(x_ref, wb_ref, wc_ref, wh_ref, hist_ref, cw_ref, g_ref, state_ref, *rest,
                      nseq, emit, carried):
    rest = list(rest)
    wbo = [rest.pop(0) for _ in range(3)] if emit else [None] * 3
    xb_ref = rest.pop(0)
    i = pl.program_id(0)
    j = pl.program_id(1)
    tm, tn = g_ref.shape
    ls = tm // nseq

    @pl.when(j == 0)
    def _():
        xb_ref[...] = x_ref[...].astype(BF16)

    xb = xb_ref[...]
    b = _dot(xb, _wload(wb_ref, wbo[0]))
    c = _dot(xb, _wload(wc_ref, wbo[1]))
    h = _dot(xb, _wload(wh_ref, wbo[2]))
    u = c * h

    if carried:
        carry_ref = rest.pop(0)

        @pl.when(i == 0)
        def _():
            carry_ref[j] = hist_ref[0]

        prev = carry_ref[j][None]
    else:
        prev = hist_ref[...]

    shape3 = (nseq, ls, tn)
    pos = lax.broadcasted_iota(jnp.int32, shape3, 1)
    u3 = u.reshape(shape3)
    p1 = pltpu.roll(u, 1, 0).reshape(shape3)
    p2 = pltpu.roll(u, 2, 0).reshape(shape3)
    h0 = prev[:, 0:1, :]
    h1 = prev[:, 1:2, :]
    p1 = jnp.where(pos == 0, h1, p1)
    p2 = jnp.where(pos == 0, h0, jnp.where(pos == 1, h1, p2))
    conv = cw_ref[0:1, :] * p2 + cw_ref[1:2, :] * p1 + cw_ref[2:3, :] * u3
    g_ref[...] = (b * conv.reshape(tm, tn)).astype(BF16)

    new_state = u3[:, ls - 2:ls, :]
    state_ref[...] = new_state
    if carried:
        carry_ref[j] = new_state[0]


def _conv_gate(x, weights, hist, conv_w, *, seq_len, tm, tn, emit):
    m, d = x.shape
    nj = d // tn
    carried = seq_len > tm
    nseq = 1 if carried else tm // seq_len
    assert not emit or m == tm
    hist_map = (lambda i, j: (0, 0, j)) if carried else (lambda i, j: (i, 0, j))
    scratch = [pltpu.VMEM((tm, d), BF16)]
    if carried:
        scratch.append(pltpu.VMEM((nj, CONV_W - 1, tn), F32))
    wmap = lambda i, j: (0, j)
    wb_specs, wb_shapes = _wb_outputs(weights, emit, (d, tn), wmap, (d, d))
    n_states = (m // tm) * nseq
    g, states, *wb = pl.pallas_call(
        functools.partial(_conv_gate_kernel, nseq=nseq, emit=emit, carried=carried),
        grid=(m // tm, nj),
        in_specs=[pl.BlockSpec((tm, d), lambda i, j: (i, 0))]
        + [_wspec(w, (d, tn), wmap) for w in weights]
        + [pl.BlockSpec((nseq, CONV_W - 1, tn), hist_map),
           pl.BlockSpec((None, CONV_W, tn), lambda i, j: (0, 0, j))],
        out_specs=[
            pl.BlockSpec((tm, tn), lambda i, j: (i, j)),
            pl.BlockSpec((nseq, CONV_W - 1, tn), lambda i, j: (i, 0, j)),
        ] + wb_specs,
        out_shape=[
            jax.ShapeDtypeStruct((m, d), BF16),
            jax.ShapeDtypeStruct((n_states, CONV_W - 1, d), F32),
        ] + wb_shapes,
        scratch_shapes=scratch,
        compiler_params=_params(2),
        name="conv_gate",
    )(x, *[w[0] for w in weights], hist, conv_w)
    return g, states[-hist.shape[0]:], _as_weights(wb)


LN_ROWS = 64


def _layer_norm_rows(y, gain, bias, out_ref, outb_ref):
    for r in range(0, y.shape[0], LN_ROWS):
        rows = slice(r, r + LN_ROWS)
        yr = y[rows, :]
        mu = jnp.mean(yr, axis=-1, keepdims=True)
        yc = yr - mu
        var = jnp.mean(yc * yc, axis=-1, keepdims=True)
        z = yc * lax.rsqrt(var + LN_EPS) * gain + bias
        out_ref[rows, :] = z
        if outb_ref is not None:
            outb_ref[rows, :] = z.astype(BF16)


def _mm_res_ln_kernel(a_ref, w_ref, x_ref, g_ref, b_ref, out_ref, outb_ref, *wbo):
    k = pl.program_id(1)
    tm, d = out_ref.shape

    @pl.when(k == 0)
    def _():
        out_ref[...] = ALPHA * x_ref[...]

    if len(a_ref.shape) == 3:
        a = jnp.concatenate([a_ref[h] for h in range(a_ref.shape[0])], axis=1)
    else:
        a = a_ref[...]
    out_ref[...] += _dot(a, _wload(w_ref, wbo[0] if wbo else None))

    @pl.when(k == pl.num_programs(1) - 1)
    def _():
        _layer_norm_rows(out_ref, g_ref[...], b_ref[...], out_ref, outb_ref)


def _mm_res_ln(a, weight, x, ln_g, ln_b, ln_idx, *, tm, tk, emit):
    m, d = x.shape
    kdim = weight[0].shape[-2]
    assert not emit or m == tm
    if a.ndim == 3:
        a_spec = pl.BlockSpec((tk // HEAD_DIM, tm, HEAD_DIM), lambda i, k: (k, i, 0))
    else:
        a_spec = pl.BlockSpec((tm, tk), lambda i, k: (i, k))
    wmap = lambda i, k: (k, 0)
    wb_specs, wb_shapes = _wb_outputs((weight,), emit, (tk, d), wmap, (kdim, d))
    layer, sub = ln_idx
    ln_spec = pl.BlockSpec((None, None, 1, d), lambda i, k: (layer, sub, 0, 0))
    x_new, xb_new, *wb = pl.pallas_call(
        _mm_res_ln_kernel,
        grid=(m // tm, kdim // tk),
        in_specs=[
            a_spec,
            _wspec(weight, (tk, d), wmap),
            pl.BlockSpec((tm, d), lambda i, k: (i, 0)),
            ln_spec,
            ln_spec,
        ],
        out_specs=[
            pl.BlockSpec((tm, d), lambda i, k: (i, 0)),
            pl.BlockSpec((tm, d), lambda i, k: (i, 0)),
        ] + wb_specs,
        out_shape=[
            jax.ShapeDtypeStruct((m, d), F32),
            jax.ShapeDtypeStruct((m, d), BF16),
        ] + wb_shapes,
        compiler_params=_params(2),
        name="mm_res_ln",
    )(a, weight[0], x, ln_g.reshape(DEPTH, 2, 1, d), ln_b.reshape(DEPTH, 2, 1, d))
    return x_new, xb_new, _as_weights(wb)


def _proj_res_ln_kernel(a_ref, w_ref, x_ref, g_ref, b_ref, out_ref, outb_ref):
    if len(a_ref.shape) == 3:
        a = jnp.concatenate([a_ref[h] for h in range(a_ref.shape[0])], axis=1)
    else:
        a = a_ref[...]
    y = ALPHA * x_ref[...] + _dot(a, w_ref[...])
    _layer_norm_rows(y, g_ref[...], b_ref[...], out_ref, outb_ref)


def _proj_res_ln(a, weight, x, ln_g, ln_b, ln_idx, *, tm):
    m, d = x.shape
    w = weight[0]
    assert w.shape == (d, d) and w.dtype == BF16
    if a.ndim == 3:
        a_spec = pl.BlockSpec((N_HEADS, tm, HEAD_DIM), lambda i: (0, i, 0))
    else:
        a_spec = pl.BlockSpec((tm, d), lambda i: (i, 0))
    layer, sub = ln_idx
    ln_spec = pl.BlockSpec((None, None, 1, d), lambda i: (layer, sub, 0, 0))
    return pl.pallas_call(
        _proj_res_ln_kernel,
        grid=(m // tm,),
        in_specs=[
            a_spec,
            pl.BlockSpec((d, d), lambda i: (0, 0)),
            pl.BlockSpec((tm, d), lambda i: (i, 0)),
            ln_spec,
            ln_spec,
        ],
        out_specs=[
            pl.BlockSpec((tm, d), lambda i: (i, 0)),
            pl.BlockSpec((tm, d), lambda i: (i, 0)),
        ],
        out_shape=[
            jax.ShapeDtypeStruct((m, d), F32),
            jax.ShapeDtypeStruct((m, d), BF16),
        ],
        compiler_params=_params(1),
        name="proj_res_ln",
    )(a, w, x, ln_g.reshape(DEPTH, 2, 1, d), ln_b.reshape(DEPTH, 2, 1, d))


def _gate_up_kernel(xb_ref, wg_ref, wu_ref, h_ref, *wbo):
    wbo = wbo or (None, None)
    xb = xb_ref[...]
    g = _dot(xb, _wload(wg_ref, wbo[0]))
    u = _dot(xb, _wload(wu_ref, wbo[1]))
    h_ref[...] = (jax.nn.silu(g) * u).astype(BF16)


def _gate_up(xb, weights, *, tm, tf, emit):
    m, d = xb.shape
    nf = D_FF // tf
    assert not emit or m == tm
    wmap = lambda i, j: (0, j)
    wb_specs, wb_shapes = _wb_outputs(weights, emit, (d, tf), wmap, (d, D_FF))
    h, *wb = pl.pallas_call(
        _gate_up_kernel,
        grid=(m // tm, nf),
        in_specs=[pl.BlockSpec((tm, d), lambda i, j: (i, 0))]
        + [_wspec(w, (d, tf), wmap) for w in weights],
        out_specs=[pl.BlockSpec((tm, tf), lambda i, j: (i, j))] + wb_specs,
        out_shape=[jax.ShapeDtypeStruct((m, D_FF), BF16)] + wb_shapes,
        compiler_params=_params(2),
        name="gate_up",
    )(xb, *[w[0] for w in weights])
    return h, _as_weights(wb)


def _ffn_kernel(x_ref, wg_ref, wu_ref, wd_ref, g_ref, b_ref, out_ref, *rest):
    *outb_ref, xb_ref = rest
    j = pl.program_id(1)
    tf = wg_ref.shape[1]

    @pl.when(j == 0)
    def _():
        x = x_ref[...]
        xb_ref[...] = x.astype(BF16)
        out_ref[...] = ALPHA * x

    gu = _dot(xb_ref[...], jnp.concatenate([wg_ref[...], wu_ref[...]], axis=1))
    h = (jax.nn.silu(gu[:, :tf]) * gu[:, tf:]).astype(BF16)
    out_ref[...] += _dot(h, wd_ref[...])

    @pl.when(j == pl.num_programs(1) - 1)
    def _():
        _layer_norm_rows(out_ref, g_ref[...], b_ref[...], out_ref,
                         outb_ref[0] if outb_ref else None)


def _ffn_fused(x, w_gate_up, w_down, ln_g, ln_b, layer, *, tm, tf, want_bf16):
    m, d = x.shape
    nf = D_FF // tf
    ln_spec = pl.BlockSpec((None, None, 1, d), lambda i, j: (layer, 1, 0, 0))
    row_spec = pl.BlockSpec((tm, d), lambda i, j: (i, 0))
    res = pl.pallas_call(
        _ffn_kernel,
        grid=(m // tm, nf),
        in_specs=[row_spec]
        + [_wspec(w, (d, tf), lambda i, j: (0, j)) for w in w_gate_up]
        + [_wspec(w_down, (tf, d), lambda i, j: (j, 0)), ln_spec, ln_spec],
        out_specs=[row_spec] * (2 if want_bf16 else 1),
        out_shape=[jax.ShapeDtypeStruct((m, d), F32)]
        + ([jax.ShapeDtypeStruct((m, d), BF16)] if want_bf16 else []),
        scratch_shapes=[pltpu.VMEM((tm, d), BF16)],
        compiler_params=_params(2),
        name="ffn",
    )(x, *[w[0] for w in w_gate_up], w_down[0],
      ln_g.reshape(DEPTH, 2, 1, d), ln_b.reshape(DEPTH, 2, 1, d))
    return (res[0], res[1]) if want_bf16 else (res[0], None)


def _qkv_kernel(xb_ref, wq_ref, wk_ref, wv_ref, q_ref, kb_ref, vb_ref, kf_ref, vf_ref, *wbo,
                transposed_v):
    wbo = wbo or (None, None, None)
    xb = xb_ref[...]
    q = _dot(xb, _wload(wq_ref, wbo[0]))
    k = _dot(xb, _wload(wk_ref, wbo[1]))
    v = _dot(xb, _wload(wv_ref, wbo[2]))
    kf_ref[...] = k
    vf_ref[...] = v
    for hh in range(q_ref.shape[0]):
        cols = slice(hh * HEAD_DIM, (hh + 1) * HEAD_DIM)
        q_ref[hh] = q[:, cols].astype(BF16)
        kb_ref[hh] = k[:, cols].astype(BF16)
        if transposed_v:
            vb_ref[hh] = v[:, cols].T.astype(BF16)
        else:
            vb_ref[hh] = v[:, cols].astype(BF16)


def _qkv(xb, weights, *, tm, tn, transposed_v, emit):
    m, d = xb.shape
    nj = d // tn
    hb = tn // HEAD_DIM
    assert not emit or m == tm
    hm_spec = pl.BlockSpec((hb, tm, HEAD_DIM), lambda i, j: (j, i, 0))
    hm_shape = jax.ShapeDtypeStruct((N_HEADS, m, HEAD_DIM), BF16)
    if transposed_v:
        v_spec = pl.BlockSpec((hb, HEAD_DIM, tm), lambda i, j: (j, 0, i))
        v_shape = jax.ShapeDtypeStruct((N_HEADS, HEAD_DIM, m), BF16)
    else:
        v_spec, v_shape = hm_spec, hm_shape
    wmap = lambda i, j: (0, j)
    wb_specs, wb_shapes = _wb_outputs(weights, emit, (d, tn), wmap, (d, d))
    q, kb, vb, k, v, *wb = pl.pallas_call(
        functools.partial(_qkv_kernel, transposed_v=transposed_v),
        grid=(m // tm, nj),
        in_specs=[pl.BlockSpec((tm, d), lambda i, j: (i, 0))]
        + [_wspec(w, (d, tn), wmap) for w in weights],
        out_specs=[hm_spec, hm_spec, v_spec,
                   pl.BlockSpec((tm, tn), lambda i, j: (i, j)),
                   pl.BlockSpec((tm, tn), lambda i, j: (i, j))] + wb_specs,
        out_shape=[hm_shape, hm_shape, v_shape,
                   jax.ShapeDtypeStruct((m, d), F32),
                   jax.ShapeDtypeStruct((m, d), F32)] + wb_shapes,
        compiler_params=_params(2),
        name="qkv",
    )(xb, *[w[0] for w in weights])
    return q, kb, vb, k, v, _as_weights(wb)


def _split_heads_kernel(x_ref, o_ref):
    for h in range(N_HEADS):
        o_ref[:, h, :] = x_ref[:, h * HEAD_DIM:(h + 1) * HEAD_DIM]


def _split_heads(x, *, rows, tr):
    m, d = x.shape
    first = (m - rows) // tr
    return pl.pallas_call(
        _split_heads_kernel,
        grid=(rows // tr,),
        in_specs=[pl.BlockSpec((tr, d), lambda i: (first + i, 0))],
        out_specs=pl.BlockSpec((tr, N_HEADS, HEAD_DIM), lambda i: (i, 0, 0)),
        out_shape=jax.ShapeDtypeStruct((rows, N_HEADS, HEAD_DIM), x.dtype),
        compiler_params=_params(1),
        name="split_heads",
    )(x)


def _bias_kernel(rb_ref, heads_ref, pair_ref):
    rb = rb_ref[...]
    hi = rb.astype(BF16)
    r1 = rb - hi.astype(F32)
    mid = r1.astype(BF16)
    lo = (r1 - mid.astype(F32)).astype(BF16)
    src = lax.broadcasted_iota(jnp.int32, (N_REL_PAD, T_PAD), 0)
    m = lax.broadcasted_iota(jnp.int32, (N_REL_PAD, T_PAD), 1)
    idx = jnp.clip(LEFT_ROWS + CHUNK - 1 - m, -MAX_REL, MAX_REL) + MAX_REL
    onehot = (src == idx).astype(BF16)
    t = (_dot(hi, onehot) + _dot(mid, onehot)) + _dot(lo, onehot)

    key = lax.broadcasted_iota(jnp.int32, (PAIR_BAND, PAIR), 0)
    qry = lax.broadcasted_iota(jnp.int32, (PAIR_BAND, PAIR), 1)
    in_band = ((qry < CHUNK) & (key < BAND)) | ((qry >= CHUNK) & (key >= CHUNK))
    chunk_rows = []
    for h in range(N_HEADS):
        rows = jnp.broadcast_to(t[h:h + 1, :], (PAIR, T_PAD))
        rows = pltpu.roll(rows, T_PAD - (CHUNK - 1), 1, stride=1, stride_axis=0)
        chunk_rows.append(rows[:CHUNK, :PAIR_BAND])
        pair_ref[h] = jnp.where(in_band, rows[:, :PAIR_BAND].T, -jnp.inf)
    for hp in range(N_HEADS // 2):
        both = jnp.concatenate(chunk_rows[2 * hp:2 * hp + 2], axis=0)
        heads_ref[hp] = both.T[:BAND]


def _bias_tables(rel_bias):
    rb = jnp.pad(rel_bias, ((0, 0), (0, N_REL_PAD - N_REL)))
    return pl.pallas_call(
        _bias_kernel,
        out_shape=[jax.ShapeDtypeStruct((N_HEADS // 2, BAND, 2 * CHUNK), F32),
                   jax.ShapeDtypeStruct((N_HEADS, PAIR_BAND, PAIR), F32)],
        name="rel_bias_tables",
    )(rb)


PAIRS = LEFT_ROWS // PAIR


def _attn_prompt_kernel(q_ref, kp_ref, kc_ref, vp_ref, vc_ref, bias_ref, o_ref):
    g = pl.program_id(0)
    key = lax.broadcasted_iota(jnp.int32, (PAIR_BAND, PAIR), 0)

    def head(h, first):
        bias = bias_ref[h]
        for p in range(PAIRS):
            rows = slice(p * PAIR, (p + 1) * PAIR)
            lo = slice(p * PAIR, LEFT_ROWS)
            hi = slice(0, (p + 1) * PAIR)
            kband = jnp.concatenate([kp_ref[h, lo, :], kc_ref[h, hi, :]], axis=0)
            vband = jnp.concatenate([vp_ref[h, :, lo], vc_ref[h, :, hi]], axis=1)
            s = _dot_t(kband, q_ref[h, rows, :]) * SCALE + bias
            if first:
                s = jnp.where(key >= LEFT_ROWS - p * PAIR, s, -jnp.inf)
            e = jnp.exp(s - jnp.max(s, axis=0, keepdims=True))
            denom = jnp.sum(e, axis=0, keepdims=True)
            o = _dot(vband, e.astype(BF16)) * (1.0 / denom)
            o_ref[h, rows, :] = o.T.astype(BF16)

    def heads(first):
        def body(h, carry):
            head(h, first)
            return carry
        lax.fori_loop(0, N_HEADS, body, 0, unroll=8)

    pl.when(g == 0)(functools.partial(heads, True))
    pl.when(g > 0)(functools.partial(heads, False))


def _attn_prompt(q, kb, vt, bias):
    _, m, _ = q.shape
    blk = (N_HEADS, LEFT_ROWS, HEAD_DIM)
    cur = pl.BlockSpec(blk, lambda g: (0, g, 0))
    prev = pl.BlockSpec(blk, lambda g: (0, jnp.maximum(g - 1, 0), 0))
    blk_t = (N_HEADS, HEAD_DIM, LEFT_ROWS)
    cur_t = pl.BlockSpec(blk_t, lambda g: (0, 0, g))
    prev_t = pl.BlockSpec(blk_t, lambda g: (0, 0, jnp.maximum(g - 1, 0)))
    return pl.pallas_call(
        _attn_prompt_kernel,
        grid=(m // LEFT_ROWS,),
        in_specs=[cur, prev, cur, prev_t, cur_t,
                  pl.BlockSpec((N_HEADS, PAIR_BAND, PAIR), lambda g: (0, 0, 0))],
        out_specs=cur,
        out_shape=jax.ShapeDtypeStruct(q.shape, BF16),
        compiler_params=_params(1),
        name="attn_prompt",
    )(q, kb, kb, vt, vt, bias)


def _cache_copies(ck_hbm, cv_hbm, kbuf, vbuf, sem, stream, slot):
    return [pltpu.make_async_copy(src.at[stream, :, h, :],
                                  buf.at[slot, :, h * HEAD_DIM:(h + 1) * HEAD_DIM],
                                  sem.at[slot])
            for src, buf in ((ck_hbm, kbuf), (cv_hbm, vbuf)) for h in range(N_HEADS)]


def _attn_sample_kernel(q_ref, kc_ref, vc_ref, ck_hbm, cv_hbm, bias_ref, o_ref, kbuf, vbuf, sem):
    n = pl.program_id(0)
    slot = n % 2
    copies = functools.partial(_cache_copies, ck_hbm, cv_hbm, kbuf, vbuf, sem)

    @pl.when(n == 0)
    def _():
        for cp in copies(0, 0):
            cp.start()

    @pl.when(n + 1 < pl.num_programs(0))
    def _():
        for cp in copies(n + 1, 1 - slot):
            cp.start()

    for cp in copies(n, slot):
        cp.wait()

    zeros = jnp.zeros((CHUNK, HEAD_DIM), BF16)
    for hp in range(N_HEADS // 2):
        h0, h1 = 2 * hp, 2 * hp + 1
        cols = slice(h0 * HEAD_DIM, (h1 + 1) * HEAD_DIM)
        knew = jnp.concatenate([kc_ref[h0], kc_ref[h1]], axis=1)
        vnew = jnp.concatenate([vc_ref[h0], vc_ref[h1]], axis=1)
        kband = jnp.concatenate([kbuf[slot, :, cols].astype(BF16), knew], axis=0)
        vband = jnp.concatenate([vbuf[slot, :, cols].astype(BF16), vnew], axis=0)
        qbd = jnp.concatenate([jnp.concatenate([q_ref[h0], zeros], axis=1),
                               jnp.concatenate([zeros, q_ref[h1]], axis=1)], axis=0)
        s = _dot_t(kband, qbd) * SCALE + bias_ref[hp]
        e = jnp.exp(s - jnp.max(s, axis=0, keepdims=True))
        denom = jnp.sum(e, axis=0, keepdims=True)
        o = lax.dot_general(vband, e.astype(BF16), (((0,), (0,)), ((), ())),
                            preferred_element_type=F32) * (1.0 / denom)
        o_ref[h0] = o[:HEAD_DIM].T[:CHUNK].astype(BF16)
        o_ref[h1] = o[HEAD_DIM:].T[CHUNK:].astype(BF16)


def _attn_sample(q, kb, vb, cache_k, cache_v, bias):
    n = cache_k.shape[0]
    new = pl.BlockSpec((N_HEADS, CHUNK, HEAD_DIM), lambda s: (0, s, 0))
    cache = pl.BlockSpec(memory_space=pl.ANY)
    return pl.pallas_call(
        _attn_sample_kernel,
        grid=(n,),
        in_specs=[new, new, new, cache, cache,
                  pl.BlockSpec((N_HEADS // 2, BAND, 2 * CHUNK), lambda s: (0, 0, 0))],
        out_specs=new,
        out_shape=jax.ShapeDtypeStruct(q.shape, BF16),
        scratch_shapes=[pltpu.VMEM((2, LEFT_ROWS, D_MODEL), F32),
                        pltpu.VMEM((2, LEFT_ROWS, D_MODEL), F32),
                        pltpu.SemaphoreType.DMA((2,))],
        compiler_params=_params(1),
        name="attn_sample",
    )(q, kb, vb, cache_k, cache_v, bias)


TM = 1024
TK_OUT = 512
TM_PROJ = 512
TF = 512
TF_FUSED = 256
TR_SPLIT = 256


def _tn_three(emit):
    return 256 if emit else 512


def _f32_weights(w_in_a, w_out_a, w_kv, w_q, w_o, w_gate_up, w_down):
    return dict(
        w_in=tuple((w_in_a, 0, t * D_MODEL) for t in range(3)),
        w_out=(w_out_a, 0, 0),
        w_qkv=((w_q, 0, 0), (w_kv, 0, 0), (w_kv, 0, D_MODEL)),
        w_o=(w_o, 0, 0),
        w_gate_up=tuple(tuple((w_gate_up, l, t * D_FF) for t in range(2)) for l in range(DEPTH)),
        w_down=tuple((w_down, l, 0) for l in range(DEPTH)),
    )


def _trunk(x, hist, seq_len, cache, bias_tables, kv_rows, w, conv_w, ln_g, ln_b, *, emit):
    wb = dict(w_gate_up=[None] * DEPTH, w_down=[None] * DEPTH)

    def ffn(x, xb, layer):
        if not emit:
            return _ffn_fused(x, w["w_gate_up"][layer], w["w_down"][layer], ln_g, ln_b, layer,
                              tm=TM, tf=TF_FUSED, want_bf16=layer + 1 < DEPTH)
        h, wb["w_gate_up"][layer] = _gate_up(xb, w["w_gate_up"][layer], tm=TM, tf=TF, emit=emit)
        x, xb, (wb["w_down"][layer],) = _pad1(_mm_res_ln(
            h, w["w_down"][layer], x, ln_g, ln_b, (layer, 1), tm=TM, tk=TK_OUT, emit=emit))
        return x, xb

    g, conv_state, wb["w_in"] = _conv_gate(x, w["w_in"], hist, conv_w, seq_len=seq_len,
                                           tm=TM, tn=_tn_three(emit), emit=emit)
    def proj(a, name, ln_idx):
        if emit:
            x_new, xb_new, (wb[name],) = _pad1(_mm_res_ln(a, w[name], x, ln_g, ln_b, ln_idx,
                                                          tm=TM, tk=TK_OUT, emit=True))
            return x_new, xb_new
        return _proj_res_ln(a, w[name], x, ln_g, ln_b, ln_idx, tm=TM_PROJ)

    x, xb = proj(g, "w_out", (0, 0))
    x, xb = ffn(x, xb, 0)

    bias_heads, bias_pair = bias_tables
    q, kb, vb, k, v, wb["w_qkv"] = _qkv(xb, w["w_qkv"], tm=TM, tn=_tn_three(emit),
                                        transposed_v=cache is None, emit=emit)
    if cache is None:
        att = _attn_prompt(q, kb, vb, bias_pair)
    else:
        att = _attn_sample(q, kb, vb, cache[0], cache[1], bias_heads)
    x, xb = proj(att, "w_o", (1, 0))
    x, _ = ffn(x, xb, 1)
    k = _split_heads(k, rows=kv_rows, tr=TR_SPLIT)
    v = _split_heads(v, rows=kv_rows, tr=TR_SPLIT)
    return (x, conv_state, k, v), wb


def _pad1(res):
    x, xb, wts = res
    return x, xb, (wts + (None,))[:1]


def kernel(x_prompt, x_sample, state_conv, cache_k, cache_v, w_in_a, conv_w, w_out_a, w_kv, w_q,
           w_o, rel_bias, ln_g, ln_b, w_gate_up, w_down):
    batch, seq, d = x_prompt.shape
    dec_batch, dec_seq, _ = x_sample.shape
    assert batch == 1 and DEPTH == 2 and dec_seq == CHUNK and cache_k.shape[1] == LEFT_ROWS
    assert dec_batch * dec_seq == TM
    bias_tables = _bias_tables(rel_bias[0])
    w_f32 = _f32_weights(w_in_a, w_out_a, w_kv, w_q, w_o, w_gate_up, w_down)

    (y_s, conv_s, k_s, v_s), w_bf16 = _trunk(
        x_sample.reshape(dec_batch * dec_seq, d), state_conv[0], dec_seq, (cache_k, cache_v),
        bias_tables, dec_batch * dec_seq, w_f32, conv_w, ln_g, ln_b, emit=True)
    conv_zero = jnp.zeros((batch, CONV_W - 1, d), x_prompt.dtype)
    (y_p, conv_p, k_p, v_p), _ = _trunk(
        x_prompt.reshape(seq, d), conv_zero, seq, None, bias_tables, LEFT_ROWS,
        w_bf16, conv_w, ln_g, ln_b, emit=False)

    kv_prompt = (batch, LEFT_ROWS, N_HEADS, HEAD_DIM)
    kv_sample = (dec_batch, dec_seq, N_HEADS, HEAD_DIM)
    return (y_p.reshape(batch, seq, d),
            y_s.reshape(dec_batch, dec_seq, d),
            conv_p.reshape(1, batch, CONV_W - 1, d),
            conv_s.reshape(1, dec_batch, CONV_W - 1, d),
            k_p.reshape(kv_prompt),
            v_p.reshape(kv_prompt),
            k_s.reshape(kv_sample),
            v_s.reshape(kv_sample))
```

```python
import functools

import jax
import jax.numpy as jnp
from jax import lax
from jax.experimental import pallas as pl
from jax.experimental.pallas import tpu as pltpu

D_MODEL = 2048
DEPTH = 2
CHUNK = 64
N_LEFT_CHUNKS = 8
LEFT_ROWS = N_LEFT_CHUNKS * CHUNK
BAND = LEFT_ROWS + CHUNK
N_HEADS = 16
HEAD_DIM = D_MODEL // N_HEADS
MAX_REL = 256
N_REL = 2 * MAX_REL + 1
CONV_W = 3
D_FF = 5632
ALPHA = (2.0 * DEPTH) ** 0.25
LN_EPS = 1e-5
SCALE = HEAD_DIM ** -0.5
LOG2E = 1.4426950408889634
SCALE_LOG2 = SCALE * LOG2E

LANES = 128
VMEM_LIMIT_BYTES = 56 * 1024 * 1024

PAIR = 2 * CHUNK
PAIR_BAND = BAND + CHUNK
N_REL_PAD = 640
T_PAD = 768

BF16 = jnp.bfloat16
F32 = jnp.float32


def _params(n_axes):
    return pltpu.CompilerParams(dimension_semantics=("arbitrary",) * n_axes,
                                vmem_limit_bytes=VMEM_LIMIT_BYTES)


def _dot(a, b):
    return jnp.dot(a, b, preferred_element_type=F32)


def _dot_t(a, b):
    return lax.dot_general(a, b, (((1,), (1,)), ((), ())), preferred_element_type=F32)


def _wspec(weight, blk, imap):
    arr, layer, first_col = weight
    first = first_col // blk[1]
    if arr.ndim == 3:
        return pl.BlockSpec((None,) + blk, lambda *g: (layer,) + _shift(imap(*g), first))
    return pl.BlockSpec(blk, lambda *g: _shift(imap(*g), first))


def _shift(idx, first):
    return (idx[0], idx[1] + first)


def _wload(w_ref, wb_ref):
    w = w_ref[...]
    if w.dtype != BF16:
        w = w.astype(BF16)
    if wb_ref is not None:
        wb_ref[...] = w
    return w


def _wb_outputs(weights, emit, blk, imap, shape):
    if not emit:
        return [], []
    return ([pl.BlockSpec(blk, imap)] * len(weights),
            [jax.ShapeDtypeStruct(shape, BF16)] * len(weights))


def _as_weights(arrs):
    return tuple((a, 0, 0) for a in arrs)


def _conv_gate_kernel(x_ref, wb_ref, wc_ref, wh_ref, hist_ref, cw_ref, g_ref, state_ref, *rest,
                      nseq, emit, carried):
    rest = list(rest)
    wbo = [rest.pop(0) for _ in range(3)] if emit else [None] * 3
    xb_ref = rest.pop(0)
    i = pl.program_id(0)
    j = pl.program_id(1)
    tm, tn = g_ref.shape
    ls = tm // nseq

    @pl.when(j == 0)
    def _():
        xb_ref[...] = x_ref[...].astype(BF16)

    xb = xb_ref[...]
    b = _dot(xb, _wload(wb_ref, wbo[0]))
    c = _dot(xb, _wload(wc_ref, wbo[1]))
    h = _dot(xb, _wload(wh_ref, wbo[2]))
    u = c * h

    if carried:
        carry_ref = rest.pop(0)

        @pl.when(i == 0)
        def _():
            carry_ref[j] = hist_ref[0]

        prev = carry_ref[j][None]
    else:
        prev = hist_ref[...]

    shape3 = (nseq, ls, tn)
    pos = lax.broadcasted_iota(jnp.int32, shape3, 1)
    u3 = u.reshape(shape3)
    p1 = pltpu.roll(u, 1, 0).reshape(shape3)
    p2 = pltpu.roll(u, 2, 0).reshape(shape3)
    h0 = prev[:, 0:1, :]
    h1 = prev[:, 1:2, :]
    p1 = jnp.where(pos == 0, h1, p1)
    p2 = jnp.where(pos == 0, h0, jnp.where(pos == 1, h1, p2))
    conv = cw_ref[0:1, :] * p2 + cw_ref[1:2, :] * p1 + cw_ref[2:3, :] * u3
    g_ref[...] = (b * conv.reshape(tm, tn)).astype(BF16)

    new_state = u3[:, ls - 2:ls, :]
    state_ref[...] = new_state
    if carried:
        carry_ref[j] = new_state[0]


def _conv_gate(x, weights, hist, conv_w, *, seq_len, tm, tn, emit):
    m, d = x.shape
    nj = d // tn
    carried = seq_len > tm
    nseq = 1 if carried else tm // seq_len
    assert not emit or m == tm
    hist_map = (lambda i, j: (0, 0, j)) if carried else (lambda i, j: (i, 0, j))
    scratch = [pltpu.VMEM((tm, d), BF16)]
    if carried:
        scratch.append(pltpu.VMEM((nj, CONV_W - 1, tn), F32))
    wmap = lambda i, j: (0, j)
    wb_specs, wb_shapes = _wb_outputs(weights, emit, (d, tn), wmap, (d, d))
    n_states = (m // tm) * nseq
    g, states, *wb = pl.pallas_call(
        functools.partial(_conv_gate_kernel, nseq=nseq, emit=emit, carried=carried),
        grid=(m // tm, nj),
        in_specs=[pl.BlockSpec((tm, d), lambda i, j: (i, 0))]
        + [_wspec(w, (d, tn), wmap) for w in weights]
        + [pl.BlockSpec((nseq, CONV_W - 1, tn), hist_map),
           pl.BlockSpec((None, CONV_W, tn), lambda i, j: (0, 0, j))],
        out_specs=[
            pl.BlockSpec((tm, tn), lambda i, j: (i, j)),
            pl.BlockSpec((nseq, CONV_W - 1, tn), lambda i, j: (i, 0, j)),
        ] + wb_specs,
        out_shape=[
            jax.ShapeDtypeStruct((m, d), BF16),
            jax.ShapeDtypeStruct((n_states, CONV_W - 1, d), F32),
        ] + wb_shapes,
        scratch_shapes=scratch,
        compiler_params=_params(2),
        name="conv_gate",
    )(x, *[w[0] for w in weights], hist, conv_w)
    return g, states[-hist.shape[0]:], _as_weights(wb)


LN_ROWS = 64


def _layer_norm_rows(y, gain, bias, out_ref, outb_ref):
    for r in range(0, y.shape[0], LN_ROWS):
        rows = slice(r, r + LN_ROWS)
        yr = y[rows, :]
        mu = jnp.mean(yr, axis=-1, keepdims=True)
        yc = yr - mu
        var = jnp.mean(yc * yc, axis=-1, keepdims=True)
        z = yc * lax.rsqrt(var + LN_EPS) * gain + bias
        out_ref[rows, :] = z
        if outb_ref is not None:
            outb_ref[rows, :] = z.astype(BF16)


def _mm_res_ln_kernel(a_ref, w_ref, x_ref, g_ref, b_ref, out_ref, outb_ref, *wbo):
    k = pl.program_id(1)
    tm, d = out_ref.shape

    @pl.when(k == 0)
    def _():
        out_ref[...] = ALPHA * x_ref[...]

    if len(a_ref.shape) == 3:
        a = jnp.concatenate([a_ref[h] for h in range(a_ref.shape[0])], axis=1)
    else:
        a = a_ref[...]
    out_ref[...] += _dot(a, _wload(w_ref, wbo[0] if wbo else None))

    @pl.when(k == pl.num_programs(1) - 1)
    def _():
        _layer_norm_rows(out_ref, g_ref[...], b_ref[...], out_ref, outb_ref)


def _mm_res_ln(a, weight, x, ln_g, ln_b, ln_idx, *, tm, tk, emit):
    m, d = x.shape
    kdim = weight[0].shape[-2]
    assert not emit or m == tm
    if a.ndim == 3:
        a_spec = pl.BlockSpec((tk // HEAD_DIM, tm, HEAD_DIM), lambda i, k: (k, i, 0))
    else:
        a_spec = pl.BlockSpec((tm, tk), lambda i, k: (i, k))
    wmap = lambda i, k: (k, 0)
    wb_specs, wb_shapes = _wb_outputs((weight,), emit, (tk, d), wmap, (kdim, d))
    layer, sub = ln_idx
    ln_spec = pl.BlockSpec((None, None, 1, d), lambda i, k: (layer, sub, 0, 0))
    x_new, xb_new, *wb = pl.pallas_call(
        _mm_res_ln_kernel,
        grid=(m // tm, kdim // tk),
        in_specs=[
            a_spec,
            _wspec(weight, (tk, d), wmap),
            pl.BlockSpec((tm, d), lambda i, k: (i, 0)),
            ln_spec,
            ln_spec,
        ],
        out_specs=[
            pl.BlockSpec((tm, d), lambda i, k: (i, 0)),
            pl.BlockSpec((tm, d), lambda i, k: (i, 0)),
        ] + wb_specs,
        out_shape=[
            jax.ShapeDtypeStruct((m, d), F32),
            jax.ShapeDtypeStruct((m, d), BF16),
        ] + wb_shapes,
        compiler_params=_params(2),
        name="mm_res_ln",
    )(a, weight[0], x, ln_g.reshape(DEPTH, 2, 1, d), ln_b.reshape(DEPTH, 2, 1, d))
    return x_new, xb_new, _as_weights(wb)


def _proj_res_ln_kernel(a_ref, w_ref, x_ref, g_ref, b_ref, out_ref, outb_ref):
    if len(a_ref.shape) == 3:
        a = jnp.concatenate([a_ref[h] for h in range(a_ref.shape[0])], axis=1)
    else:
        a = a_ref[...]
    y = ALPHA * x_ref[...] + _dot(a, w_ref[...])
    _layer_norm_rows(y, g_ref[...], b_ref[...], out_ref, outb_ref)


def _proj_res_ln(a, weight, x, ln_g, ln_b, ln_idx, *, tm):
    m, d = x.shape
    w = weight[0]
    assert w.shape == (d, d) and w.dtype == BF16
    if a.ndim == 3:
        a_spec = pl.BlockSpec((N_HEADS, tm, HEAD_DIM), lambda i: (0, i, 0))
    else:
        a_spec = pl.BlockSpec((tm, d), lambda i: (i, 0))
    layer, sub = ln_idx
    ln_spec = pl.BlockSpec((None, None, 1, d), lambda i: (layer, sub, 0, 0))
    return pl.pallas_call(
        _proj_res_ln_kernel,
        grid=(m // tm,),
        in_specs=[
            a_spec,
            pl.BlockSpec((d, d), lambda i: (0, 0)),
            pl.BlockSpec((tm, d), lambda i: (i, 0)),
            ln_spec,
            ln_spec,
        ],
        out_specs=[
            pl.BlockSpec((tm, d), lambda i: (i, 0)),
            pl.BlockSpec((tm, d), lambda i: (i, 0)),
        ],
        out_shape=[
            jax.ShapeDtypeStruct((m, d), F32),
            jax.ShapeDtypeStruct((m, d), BF16),
        ],
        compiler_params=_params(1),
        name="proj_res_ln",
    )(a, w, x, ln_g.reshape(DEPTH, 2, 1, d), ln_b.reshape(DEPTH, 2, 1, d))


def _gate_up_kernel(xb_ref, wg_ref, wu_ref, h_ref, *wbo):
    wbo = wbo or (None, None)
    xb = xb_ref[...]
    g = _dot(xb, _wload(wg_ref, wbo[0]))
    u = _dot(xb, _wload(wu_ref, wbo[1]))
    h_ref[...] = (jax.nn.silu(g) * u).astype(BF16)


def _gate_up(xb, weights, *, tm, tf, emit):
    m, d = xb.shape
    nf = D_FF // tf
    assert not emit or m == tm
    wmap = lambda i, j: (0, j)
    wb_specs, wb_shapes = _wb_outputs(weights, emit, (d, tf), wmap, (d, D_FF))
    h, *wb = pl.pallas_call(
        _gate_up_kernel,
        grid=(m // tm, nf),
        in_specs=[pl.BlockSpec((tm, d), lambda i, j: (i, 0))]
        + [_wspec(w, (d, tf), wmap) for w in weights],
        out_specs=[pl.BlockSpec((tm, tf), lambda i, j: (i, j))] + wb_specs,
        out_shape=[jax.ShapeDtypeStruct((m, D_FF), BF16)] + wb_shapes,
        compiler_params=_params(2),
        name="gate_up",
    )(xb, *[w[0] for w in weights])
    return h, _as_weights(wb)


def _ffn_kernel(x_ref, wg_ref, wu_ref, wd_ref, g_ref, b_ref, out_ref, *rest):
    *outb_ref, xb_ref = rest
    j = pl.program_id(1)
    tf = wg_ref.shape[1]

    @pl.when(j == 0)
    def _():
        x = x_ref[...]
        xb_ref[...] = x.astype(BF16)
        out_ref[...] = ALPHA * x

    gu = _dot(xb_ref[...], jnp.concatenate([wg_ref[...], wu_ref[...]], axis=1))
    h = (jax.nn.silu(gu[:, :tf]) * gu[:, tf:]).astype(BF16)
    out_ref[...] += _dot(h, wd_ref[...])

    @pl.when(j == pl.num_programs(1) - 1)
    def _():
        _layer_norm_rows(out_ref, g_ref[...], b_ref[...], out_ref,
                         outb_ref[0] if outb_ref else None)


def _ffn_fused(x, w_gate_up, w_down, ln_g, ln_b, layer, *, tm, tf, want_bf16):
    m, d = x.shape
    nf = D_FF // tf
    ln_spec = pl.BlockSpec((None, None, 1, d), lambda i, j: (layer, 1, 0, 0))
    row_spec = pl.BlockSpec((tm, d), lambda i, j: (i, 0))
    res = pl.pallas_call(
        _ffn_kernel,
        grid=(m // tm, nf),
        in_specs=[row_spec]
        + [_wspec(w, (d, tf), lambda i, j: (0, j)) for w in w_gate_up]
        + [_wspec(w_down, (tf, d), lambda i, j: (j, 0)), ln_spec, ln_spec],
        out_specs=[row_spec] * (2 if want_bf16 else 1),
        out_shape=[jax.ShapeDtypeStruct((m, d), F32)]
        + ([jax.ShapeDtypeStruct((m, d), BF16)] if want_bf16 else []),
        scratch_shapes=[pltpu.VMEM((tm, d), BF16)],
        compiler_params=_params(2),
        name="ffn",
    )(x, *[w[0] for w in w_gate_up], w_down[0],
      ln_g.reshape(DEPTH, 2, 1, d), ln_b.reshape(DEPTH, 2, 1, d))
    return (res[0], res[1]) if want_bf16 else (res[0], None)


def _qkv_kernel(xb_ref, wq_ref, wk_ref, wv_ref, q_ref, kb_ref, vb_ref, kf_ref, vf_ref, *wbo,
                transposed_v):
    wbo = wbo or (None, None, None)
    xb = xb_ref[...]
    q = _dot(xb, _wload(wq_ref, wbo[0]))
    k = _dot(xb, _wload(wk_ref, wbo[1]))
    v = _dot(xb, _wload(wv_ref, wbo[2]))
    kf_ref[...] = k
    vf_ref[...] = v
    for hh in range(q_ref.shape[0]):
        cols = slice(hh * HEAD_DIM, (hh + 1) * HEAD_DIM)
        q_ref[hh] = q[:, cols].astype(BF16)
        kb_ref[hh] = k[:, cols].astype(BF16)
        if transposed_v:
            vb_ref[hh] = v[:, cols].T.astype(BF16)
        else:
            vb_ref[hh] = v[:, cols].astype(BF16)


def _qkv(xb, weights, *, tm, tn, transposed_v, emit):
    m, d = xb.shape
    nj = d // tn
    hb = tn // HEAD_DIM
    assert not emit or m == tm
    hm_spec = pl.BlockSpec((hb, tm, HEAD_DIM), lambda i, j: (j, i, 0))
    hm_shape = jax.ShapeDtypeStruct((N_HEADS, m, HEAD_DIM), BF16)
    if transposed_v:
        v_spec = pl.BlockSpec((hb, HEAD_DIM, tm), lambda i, j: (j, 0, i))
        v_shape = jax.ShapeDtypeStruct((N_HEADS, HEAD_DIM, m), BF16)
    else:
        v_spec, v_shape = hm_spec, hm_shape
    wmap = lambda i, j: (0, j)
    wb_specs, wb_shapes = _wb_outputs(weights, emit, (d, tn), wmap, (d, d))
    q, kb, vb, k, v, *wb = pl.pallas_call(
        functools.partial(_qkv_kernel, transposed_v=transposed_v),
        grid=(m // tm, nj),
        in_specs=[pl.BlockSpec((tm, d), lambda i, j: (i, 0))]
        + [_wspec(w, (d, tn), wmap) for w in weights],
        out_specs=[hm_spec, hm_spec, v_spec,
                   pl.BlockSpec((tm, tn), lambda i, j: (i, j)),
                   pl.BlockSpec((tm, tn), lambda i, j: (i, j))] + wb_specs,
        out_shape=[hm_shape, hm_shape, v_shape,
                   jax.ShapeDtypeStruct((m, d), F32),
                   jax.ShapeDtypeStruct((m, d), F32)] + wb_shapes,
        compiler_params=_params(2),
        name="qkv",
    )(xb, *[w[0] for w in weights])
    return q, kb, vb, k, v, _as_weights(wb)


def _split_heads_kernel(x_ref, o_ref):
    for h in range(N_HEADS):
        o_ref[:, h, :] = x_ref[:, h * HEAD_DIM:(h + 1) * HEAD_DIM]


def _split_heads(x, *, rows, tr):
    m, d = x.shape
    first = (m - rows) // tr
    return pl.pallas_call(
        _split_heads_kernel,
        grid=(rows // tr,),
        in_specs=[pl.BlockSpec((tr, d), lambda i: (first + i, 0))],
        out_specs=pl.BlockSpec((tr, N_HEADS, HEAD_DIM), lambda i: (i, 0, 0)),
        out_shape=jax.ShapeDtypeStruct((rows, N_HEADS, HEAD_DIM), x.dtype),
        compiler_params=_params(1),
        name="split_heads",
    )(x)


def _bias_kernel(rb_ref, heads_ref, pair_ref):
    rb = rb_ref[...]
    hi = rb.astype(BF16)
    r1 = rb - hi.astype(F32)
    mid = r1.astype(BF16)
    lo = (r1 - mid.astype(F32)).astype(BF16)
    src = lax.broadcasted_iota(jnp.int32, (N_REL_PAD, T_PAD), 0)
    m = lax.broadcasted_iota(jnp.int32, (N_REL_PAD, T_PAD), 1)
    idx = jnp.clip(LEFT_ROWS + CHUNK - 1 - m, -MAX_REL, MAX_REL) + MAX_REL
    onehot = (src == idx).astype(BF16)
    t = (_dot(hi, onehot) + _dot(mid, onehot)) + _dot(lo, onehot)

    key = lax.broadcasted_iota(jnp.int32, (PAIR_BAND, PAIR), 0)
    qry = lax.broadcasted_iota(jnp.int32, (PAIR_BAND, PAIR), 1)
    in_band = ((qry < CHUNK) & (key < BAND)) | ((qry >= CHUNK) & (key >= CHUNK))
    chunk_rows = []
    for h in range(N_HEADS):
        rows = jnp.broadcast_to(t[h:h + 1, :], (PAIR, T_PAD))
        rows = pltpu.roll(rows, T_PAD - (CHUNK - 1), 1, stride=1, stride_axis=0)
        chunk_rows.append(rows[:CHUNK, :PAIR_BAND])
        pair_ref[h] = jnp.where(in_band, rows[:, :PAIR_BAND].T * LOG2E, -jnp.inf)
    for hp in range(N_HEADS // 2):
        both = jnp.concatenate(chunk_rows[2 * hp:2 * hp + 2], axis=0)
        heads_ref[hp] = both.T[:BAND] * LOG2E


def _bias_tables(rel_bias):
    rb = jnp.pad(rel_bias, ((0, 0), (0, N_REL_PAD - N_REL)))
    return pl.pallas_call(
        _bias_kernel,
        out_shape=[jax.ShapeDtypeStruct((N_HEADS // 2, BAND, 2 * CHUNK), F32),
                   jax.ShapeDtypeStruct((N_HEADS, PAIR_BAND, PAIR), F32)],
        name="rel_bias_tables",
    )(rb)


PAIRS = LEFT_ROWS // PAIR


def _attn_prompt_kernel(q_ref, kp_ref, kc_ref, vp_ref, vc_ref, bias_ref, o_ref):
    g = pl.program_id(0)
    key = lax.broadcasted_iota(jnp.int32, (PAIR_BAND, PAIR), 0)

    def head(h, first):
        bias = bias_ref[h]
        for p in range(PAIRS):
            rows = slice(p * PAIR, (p + 1) * PAIR)
            lo = slice(p * PAIR, LEFT_ROWS)
            hi = slice(0, (p + 1) * PAIR)
            kband = jnp.concatenate([kp_ref[h, lo, :], kc_ref[h, hi, :]], axis=0)
            vband = jnp.concatenate([vp_ref[h, :, lo], vc_ref[h, :, hi]], axis=1)
            s = _dot_t(kband, q_ref[h, rows, :]) * SCALE_LOG2 + bias
            if first:
                s = jnp.where(key >= LEFT_ROWS - p * PAIR, s, -jnp.inf)
            e = jnp.exp2(s - jnp.max(s, axis=0, keepdims=True))
            denom = jnp.sum(e, axis=0, keepdims=True)
            o = _dot(vband, e.astype(BF16)) * (1.0 / denom)
            o_ref[h, rows, :] = o.T.astype(BF16)

    def heads(first):
        def body(h, carry):
            head(h, first)
            return carry
        lax.fori_loop(0, N_HEADS, body, 0, unroll=8)

    pl.when(g == 0)(functools.partial(heads, True))
    pl.when(g > 0)(functools.partial(heads, False))


def _attn_prompt(q, kb, vt, bias):
    _, m, _ = q.shape
    blk = (N_HEADS, LEFT_ROWS, HEAD_DIM)
    cur = pl.BlockSpec(blk, lambda g: (0, g, 0))
    prev = pl.BlockSpec(blk, lambda g: (0, jnp.maximum(g - 1, 0), 0))
    blk_t = (N_HEADS, HEAD_DIM, LEFT_ROWS)
    cur_t = pl.BlockSpec(blk_t, lambda g: (0, 0, g))
    prev_t = pl.BlockSpec(blk_t, lambda g: (0, 0, jnp.maximum(g - 1, 0)))
    return pl.pallas_call(
        _attn_prompt_kernel,
        grid=(m // LEFT_ROWS,),
        in_specs=[cur, prev, cur, prev_t, cur_t,
                  pl.BlockSpec((N_HEADS, PAIR_BAND, PAIR), lambda g: (0, 0, 0))],
        out_specs=cur,
        out_shape=jax.ShapeDtypeStruct(q.shape, BF16),
        compiler_params=_params(1),
        name="attn_prompt",
    )(q, kb, kb, vt, vt, bias)


def _cache_copies(ck_hbm, cv_hbm, kbuf, vbuf, sem, stream, slot):
    return [pltpu.make_async_copy(src.at[stream, :, h, :],
                                  buf.at[slot, :, h * HEAD_DIM:(h + 1) * HEAD_DIM],
                                  sem.at[slot])
            for src, buf in ((ck_hbm, kbuf), (cv_hbm, vbuf)) for h in range(N_HEADS)]


def _attn_sample_kernel(q_ref, kc_ref, vc_ref, ck_hbm, cv_hbm, bias_ref, o_ref, kbuf, vbuf, sem):
    n = pl.program_id(0)
    slot = n % 2
    copies = functools.partial(_cache_copies, ck_hbm, cv_hbm, kbuf, vbuf, sem)

    @pl.when(n == 0)
    def _():
        for cp in copies(0, 0):
            cp.start()

    @pl.when(n + 1 < pl.num_programs(0))
    def _():
        for cp in copies(n + 1, 1 - slot):
            cp.start()

    for cp in copies(n, slot):
        cp.wait()

    zeros = jnp.zeros((CHUNK, HEAD_DIM), BF16)
    for hp in range(N_HEADS // 2):
        h0, h1 = 2 * hp, 2 * hp + 1
        cols = slice(h0 * HEAD_DIM, (h1 + 1) * HEAD_DIM)
        knew = jnp.concatenate([kc_ref[h0], kc_ref[h1]], axis=1)
        vnew = jnp.concatenate([vc_ref[h0], vc_ref[h1]], axis=1)
        kband = jnp.concatenate([kbuf[slot, :, cols].astype(BF16), knew], axis=0)
        vband = jnp.concatenate([vbuf[slot, :, cols].astype(BF16), vnew], axis=0)
        qbd = jnp.concatenate([jnp.concatenate([q_ref[h0], zeros], axis=1),
                               jnp.concatenate([zeros, q_ref[h1]], axis=1)], axis=0)
        s = _dot_t(kband, qbd) * SCALE_LOG2 + bias_ref[hp]
        e = jnp.exp2(s - jnp.max(s, axis=0, keepdims=True))
        denom = jnp.sum(e, axis=0, keepdims=True)
        o = lax.dot_general(vband, e.astype(BF16), (((0,), (0,)), ((), ())),
                            preferred_element_type=F32) * (1.0 / denom)
        o_ref[h0] = o[:HEAD_DIM].T[:CHUNK].astype(BF16)
        o_ref[h1] = o[HEAD_DIM:].T[CHUNK:].astype(BF16)


def _attn_sample(q, kb, vb, cache_k, cache_v, bias):
    n = cache_k.shape[0]
    new = pl.BlockSpec((N_HEADS, CHUNK, HEAD_DIM), lambda s: (0, s, 0))
    cache = pl.BlockSpec(memory_space=pl.ANY)
    return pl.pallas_call(
        _attn_sample_kernel,
        grid=(n,),
        in_specs=[new, new, new, cache, cache,
                  pl.BlockSpec((N_HEADS // 2, BAND, 2 * CHUNK), lambda s: (0, 0, 0))],
        out_specs=new,
        out_shape=jax.ShapeDtypeStruct(q.shape, BF16),
        scratch_shapes=[pltpu.VMEM((2, LEFT_ROWS, D_MODEL), F32),
                        pltpu.VMEM((2, LEFT_ROWS, D_MODEL), F32),
                        pltpu.SemaphoreType.DMA((2,))],
        compiler_params=_params(1),
        name="attn_sample",
    )(q, kb, vb, cache_k, cache_v, bias)


TM = 1024
TK_OUT = 512
TM_PROJ = 512
TF = 512
TF_FUSED = 256
TR_SPLIT = 256


def _tn_three(emit):
    return 256 if emit else 512


def _f32_weights(w_in_a, w_out_a, w_kv, w_q, w_o, w_gate_up, w_down):
    return dict(
        w_in=tuple((w_in_a, 0, t * D_MODEL) for t in range(3)),
        w_out=(w_out_a, 0, 0),
        w_qkv=((w_q, 0, 0), (w_kv, 0, 0), (w_kv, 0, D_MODEL)),
        w_o=(w_o, 0, 0),
        w_gate_up=tuple(tuple((w_gate_up, l, t * D_FF) for t in range(2)) for l in range(DEPTH)),
        w_down=tuple((w_down, l, 0) for l in range(DEPTH)),
    )


def _trunk(x, hist, seq_len, cache, bias_tables, kv_rows, w, conv_w, ln_g, ln_b, *, emit):
    wb = dict(w_gate_up=[None] * DEPTH, w_down=[None] * DEPTH)

    def ffn(x, xb, layer):
        if not emit:
            return _ffn_fused(x, w["w_gate_up"][layer], w["w_down"][layer], ln_g, ln_b, layer,
                              tm=TM, tf=TF_FUSED, want_bf16=layer + 1 < DEPTH)
        h, wb["w_gate_up"][layer] = _gate_up(xb, w["w_gate_up"][layer], tm=TM, tf=TF, emit=emit)
        x, xb, (wb["w_down"][layer],) = _pad1(_mm_res_ln(
            h, w["w_down"][layer], x, ln_g, ln_b, (layer, 1), tm=TM, tk=TK_OUT, emit=emit))
        return x, xb

    g, conv_state, wb["w_in"] = _conv_gate(x, w["w_in"], hist, conv_w, seq_len=seq_len,
                                           tm=TM, tn=_tn_three(emit), emit=emit)
    def proj(a, name, ln_idx):
        if emit:
            x_new, xb_new, (wb[name],) = _pad1(_mm_res_ln(a, w[name], x, ln_g, ln_b, ln_idx,
                                                          tm=TM, tk=TK_OUT, emit=True))
            return x_new, xb_new
        return _proj_res_ln(a, w[name], x, ln_g, ln_b, ln_idx, tm=TM_PROJ)

    x, xb = proj(g, "w_out", (0, 0))
    x, xb = ffn(x, xb, 0)

    bias_heads, bias_pair = bias_tables
    q, kb, vb, k, v, wb["w_qkv"] = _qkv(xb, w["w_qkv"], tm=TM, tn=_tn_three(emit),
                                        transposed_v=cache is None, emit=emit)
    if cache is None:
        att = _attn_prompt(q, kb, vb, bias_pair)
    else:
        att = _attn_sample(q, kb, vb, cache[0], cache[1], bias_heads)
    x, xb = proj(att, "w_o", (1, 0))
    x, _ = ffn(x, xb, 1)
    k = _split_heads(k, rows=kv_rows, tr=TR_SPLIT)
    v = _split_heads(v, rows=kv_rows, tr=TR_SPLIT)
    return (x, conv_state, k, v), wb


def _pad1(res):
    x, xb, wts = res
    return x, xb, (wts + (None,))[:1]


def kernel(x_prompt, x_sample, state_conv, cache_k, cache_v, w_in_a, conv_w, w_out_a, w_kv, w_q,
           w_o, rel_bias, ln_g, ln_b, w_gate_up, w_down):
    batch, seq, d = x_prompt.shape
    dec_batch, dec_seq, _ = x_sample.shape
    assert batch == 1 and DEPTH == 2 and dec_seq == CHUNK and cache_k.shape[1] == LEFT_ROWS
    assert dec_batch * dec_seq == TM
    bias_tables = _bias_tables(rel_bias[0])
    w_f32 = _f32_weights(w_in_a, w_out_a, w_kv, w_q, w_o, w_gate_up, w_down)

    (y_s, conv_s, k_s, v_s), w_bf16 = _trunk(
        x_sample.reshape(dec_batch * dec_seq, d), state_conv[0], dec_seq, (cache_k, cache_v),
        bias_tables, dec_batch * dec_seq, w_f32, conv_w, ln_g, ln_b, emit=True)
    conv_zero = jnp.zeros((batch, CONV_W - 1, d), x_prompt.dtype)
    (y_p, conv_p, k_p, v_p), _ = _trunk(
        x_prompt.reshape(seq, d), conv_zero, seq, None, bias_tables, LEFT_ROWS,
        w_bf16, conv_w, ln_g, ln_b, emit=False)

    kv_prompt = (batch, LEFT_ROWS, N_HEADS, HEAD_DIM)
    kv_sample = (dec_batch, dec_seq, N_HEADS, HEAD_DIM)
    return (y_p.reshape(batch, seq, d),
            y_s.reshape(dec_batch, dec_seq, d),
            conv_p.reshape(1, batch, CONV_W - 1, d),
            conv_s.reshape(1, dec_batch, CONV_W - 1, d),
            k_p.reshape(kv_prompt),
            v_p.reshape(kv_prompt),
            k_s.reshape(kv_sample),
            v_s.reshape(kv_sample))
```

```python
import functools

import jax
import jax.numpy as jnp
from jax import lax
from jax.experimental import pallas as pl
from jax.experimental.pallas import tpu as pltpu

D_MODEL = 2048
DEPTH = 2
CHUNK = 64
N_LEFT_CHUNKS = 8
LEFT_ROWS = N_LEFT_CHUNKS * CHUNK
BAND = LEFT_ROWS + CHUNK
N_HEADS = 16
HEAD_DIM = D_MODEL // N_HEADS
MAX_REL = 256
N_REL = 2 * MAX_REL + 1
CONV_W = 3
D_FF = 5632
ALPHA = (2.0 * DEPTH) ** 0.25
LN_EPS = 1e-5
SCALE = HEAD_DIM ** -0.5
LOG2E = 1.4426950408889634
SCALE_LOG2 = SCALE * LOG2E

LANES = 128
VMEM_LIMIT_BYTES = 56 * 1024 * 1024

PAIR = 2 * CHUNK
PAIR_BAND = BAND + CHUNK
N_REL_PAD = 640
T_PAD = 768

BF16 = jnp.bfloat16
F32 = jnp.float32


def _params(n_axes):
    return pltpu.CompilerParams(dimension_semantics=("arbitrary",) * n_axes,
                                vmem_limit_bytes=VMEM_LIMIT_BYTES)


def _dot(a, b):
    return jnp.dot(a, b, preferred_element_type=F32)


def _dot_t(a, b):
    return lax.dot_general(a, b, (((1,), (1,)), ((), ())), preferred_element_type=F32)


def _wspec(weight, blk, imap):
    arr, layer, first_col = weight
    first = first_col // blk[1]
    if arr.ndim == 3:
        return pl.BlockSpec((None,) + blk, lambda *g: (layer,) + _shift(imap(*g), first))
    return pl.BlockSpec(blk, lambda *g: _shift(imap(*g), first))


def _shift(idx, first):
    return (idx[0], idx[1] + first)


def _wload(w_ref, wb_ref):
    w = w_ref[...]
    if w.dtype != BF16:
        w = w.astype(BF16)
    if wb_ref is not None:
        wb_ref[...] = w
    return w


def _dot_side_by_side(x, w_refs, wb_refs):
    tn = w_refs[0].shape[1]
    w = jnp.concatenate([_wload(w_ref, wb_ref) for w_ref, wb_ref in zip(w_refs, wb_refs)], axis=1)
    y = _dot(x, w)
    return [y[:, t * tn:(t + 1) * tn] for t in range(len(w_refs))]


def _wb_outputs(weights, emit, blk, imap, shape):
    if not emit:
        return [], []
    return ([pl.BlockSpec(blk, imap)] * len(weights),
            [jax.ShapeDtypeStruct(shape, BF16)] * len(weights))


def _as_weights(arrs):
    return tuple((a, 0, 0) for a in arrs)


def _conv_gate_kernel(x_ref, wb_ref, wc_ref, wh_ref, hist_ref, cw_ref, g_ref, state_ref, *rest,
                      nseq, emit, carried):
    rest = list(rest)
    wbo = [rest.pop(0) for _ in range(3)] if emit else [None] * 3
    xb_ref = rest.pop(0)
    i = pl.program_id(0)
    j = pl.program_id(1)
    tm, tn = g_ref.shape
    ls = tm // nseq

    @pl.when(j == 0)
    def _():
        xb_ref[...] = x_ref[...].astype(BF16)

    b, c, h = _dot_side_by_side(xb_ref[...], (wb_ref, wc_ref, wh_ref), wbo)
    u = c * h

    if carried:
        carry_ref = rest.pop(0)

        @pl.when(i == 0)
        def _():
            carry_ref[j] = hist_ref[0]

        prev = carry_ref[j][None]
    else:
        prev = hist_ref[...]

    shape3 = (nseq, ls, tn)
    pos = lax.broadcasted_iota(jnp.int32, shape3, 1)
    u3 = u.reshape(shape3)
    p1 = pltpu.roll(u, 1, 0).reshape(shape3)
    p2 = pltpu.roll(u, 2, 0).reshape(shape3)
    h0 = prev[:, 0:1, :]
    h1 = prev[:, 1:2, :]
    p1 = jnp.where(pos == 0, h1, p1)
    p2 = jnp.where(pos == 0, h0, jnp.where(pos == 1, h1, p2))
    conv = cw_ref[0:1, :] * p2 + cw_ref[1:2, :] * p1 + cw_ref[2:3, :] * u3
    g_ref[...] = (b * conv.reshape(tm, tn)).astype(BF16)

    new_state = u3[:, ls - 2:ls, :]
    state_ref[...] = new_state
    if carried:
        carry_ref[j] = new_state[0]


def _conv_gate(x, weights, hist, conv_w, *, seq_len, tm, tn, emit):
    m, d = x.shape
    nj = d // tn
    carried = seq_len > tm
    nseq = 1 if carried else tm // seq_len
    assert not emit or m == tm
    hist_map = (lambda i, j: (0, 0, j)) if carried else (lambda i, j: (i, 0, j))
    scratch = [pltpu.VMEM((tm, d), BF16)]
    if carried:
        scratch.append(pltpu.VMEM((nj, CONV_W - 1, tn), F32))
    wmap = lambda i, j: (0, j)
    wb_specs, wb_shapes = _wb_outputs(weights, emit, (d, tn), wmap, (d, d))
    n_states = (m // tm) * nseq
    g, states, *wb = pl.pallas_call(
        functools.partial(_conv_gate_kernel, nseq=nseq, emit=emit, carried=carried),
        grid=(m // tm, nj),
        in_specs=[pl.BlockSpec((tm, d), lambda i, j: (i, 0))]
        + [_wspec(w, (d, tn), wmap) for w in weights]
        + [pl.BlockSpec((nseq, CONV_W - 1, tn), hist_map),
           pl.BlockSpec((None, CONV_W, tn), lambda i, j: (0, 0, j))],
        out_specs=[
            pl.BlockSpec((tm, tn), lambda i, j: (i, j)),
            pl.BlockSpec((nseq, CONV_W - 1, tn), lambda i, j: (i, 0, j)),
        ] + wb_specs,
        out_shape=[
            jax.ShapeDtypeStruct((m, d), BF16),
            jax.ShapeDtypeStruct((n_states, CONV_W - 1, d), F32),
        ] + wb_shapes,
        scratch_shapes=scratch,
        compiler_params=_params(2),
        name="conv_gate",
    )(x, *[w[0] for w in weights], hist, conv_w)
    return g, states[-hist.shape[0]:], _as_weights(wb)


LN_ROWS = 64


def _layer_norm_rows(y, gain, bias, out_ref, outb_ref, out_scale=1.0):
    for r in range(0, y.shape[0], LN_ROWS):
        rows = slice(r, r + LN_ROWS)
        yr = y[rows, :]
        mu = jnp.mean(yr, axis=-1, keepdims=True)
        yc = yr - mu
        var = jnp.mean(yc * yc, axis=-1, keepdims=True)
        z = yc * lax.rsqrt(var + LN_EPS) * gain + bias
        out_ref[rows, :] = z if out_scale == 1.0 else z * out_scale
        if outb_ref is not None:
            outb_ref[rows, :] = z.astype(BF16)


def _mm_res_ln_kernel(a_ref, w_ref, x_ref, g_ref, b_ref, out_ref, outb_ref, *wbo):
    k = pl.program_id(1)
    tm, d = out_ref.shape

    @pl.when(k == 0)
    def _():
        out_ref[...] = ALPHA * x_ref[...]

    if len(a_ref.shape) == 3:
        a = jnp.concatenate([a_ref[h] for h in range(a_ref.shape[0])], axis=1)
    else:
        a = a_ref[...]
    out_ref[...] += _dot(a, _wload(w_ref, wbo[0] if wbo else None))

    @pl.when(k == pl.num_programs(1) - 1)
    def _():
        _layer_norm_rows(out_ref, g_ref[...], b_ref[...], out_ref, outb_ref)


def _mm_res_ln(a, weight, x, ln_g, ln_b, ln_idx, *, tm, tk, emit):
    m, d = x.shape
    kdim = weight[0].shape[-2]
    assert not emit or m == tm
    if a.ndim == 3:
        a_spec = pl.BlockSpec((tk // HEAD_DIM, tm, HEAD_DIM), lambda i, k: (k, i, 0))
    else:
        a_spec = pl.BlockSpec((tm, tk), lambda i, k: (i, k))
    wmap = lambda i, k: (k, 0)
    wb_specs, wb_shapes = _wb_outputs((weight,), emit, (tk, d), wmap, (kdim, d))
    layer, sub = ln_idx
    ln_spec = pl.BlockSpec((None, None, 1, d), lambda i, k: (layer, sub, 0, 0))
    x_new, xb_new, *wb = pl.pallas_call(
        _mm_res_ln_kernel,
        grid=(m // tm, kdim // tk),
        in_specs=[
            a_spec,
            _wspec(weight, (tk, d), wmap),
            pl.BlockSpec((tm, d), lambda i, k: (i, 0)),
            ln_spec,
            ln_spec,
        ],
        out_specs=[
            pl.BlockSpec((tm, d), lambda i, k: (i, 0)),
            pl.BlockSpec((tm, d), lambda i, k: (i, 0)),
        ] + wb_specs,
        out_shape=[
            jax.ShapeDtypeStruct((m, d), F32),
            jax.ShapeDtypeStruct((m, d), BF16),
        ] + wb_shapes,
        compiler_params=_params(2),
        name="mm_res_ln",
    )(a, weight[0], x, ln_g.reshape(DEPTH, 2, 1, d), ln_b.reshape(DEPTH, 2, 1, d))
    return x_new, xb_new, _as_weights(wb)


def _proj_res_ln_kernel(a_ref, w_ref, x_ref, g_ref, b_ref, out_ref, outb_ref, *, prescaled):
    if len(a_ref.shape) == 3:
        a = jnp.concatenate([a_ref[h] for h in range(a_ref.shape[0])], axis=1)
    else:
        a = a_ref[...]
    residual = x_ref[...] if prescaled else ALPHA * x_ref[...]
    y = residual + _dot(a, w_ref[...])
    _layer_norm_rows(y, g_ref[...], b_ref[...], out_ref, outb_ref, out_scale=ALPHA)


def _proj_res_ln(a, weight, x, ln_g, ln_b, ln_idx, *, tm, prescaled):
    m, d = x.shape
    w = weight[0]
    assert w.shape == (d, d) and w.dtype == BF16
    if a.ndim == 3:
        a_spec = pl.BlockSpec((N_HEADS, tm, HEAD_DIM), lambda i: (0, i, 0))
    else:
        a_spec = pl.BlockSpec((tm, d), lambda i: (i, 0))
    layer, sub = ln_idx
    ln_spec = pl.BlockSpec((None, None, 1, d), lambda i: (layer, sub, 0, 0))
    return pl.pallas_call(
        functools.partial(_proj_res_ln_kernel, prescaled=prescaled),
        grid=(m // tm,),
        in_specs=[
            a_spec,
            pl.BlockSpec((d, d), lambda i: (0, 0)),
            pl.BlockSpec((tm, d), lambda i: (i, 0)),
            ln_spec,
            ln_spec,
        ],
        out_specs=[
            pl.BlockSpec((tm, d), lambda i: (i, 0)),
            pl.BlockSpec((tm, d), lambda i: (i, 0)),
        ],
        out_shape=[
            jax.ShapeDtypeStruct((m, d), F32),
            jax.ShapeDtypeStruct((m, d), BF16),
        ],
        compiler_params=_params(1),
        name="proj_res_ln",
    )(a, w, x, ln_g.reshape(DEPTH, 2, 1, d), ln_b.reshape(DEPTH, 2, 1, d))


def _gate_up_kernel(xb_ref, wg_ref, wu_ref, h_ref, *wbo):
    wbo = wbo or (None, None)
    xb = xb_ref[...]
    g = _dot(xb, _wload(wg_ref, wbo[0]))
    u = _dot(xb, _wload(wu_ref, wbo[1]))
    h_ref[...] = (jax.nn.silu(g) * u).astype(BF16)


def _gate_up(xb, weights, *, tm, tf, emit):
    m, d = xb.shape
    nf = D_FF // tf
    assert not emit or m == tm
    wmap = lambda i, j: (0, j)
    wb_specs, wb_shapes = _wb_outputs(weights, emit, (d, tf), wmap, (d, D_FF))
    h, *wb = pl.pallas_call(
        _gate_up_kernel,
        grid=(m // tm, nf),
        in_specs=[pl.BlockSpec((tm, d), lambda i, j: (i, 0))]
        + [_wspec(w, (d, tf), wmap) for w in weights],
        out_specs=[pl.BlockSpec((tm, tf), lambda i, j: (i, j))] + wb_specs,
        out_shape=[jax.ShapeDtypeStruct((m, D_FF), BF16)] + wb_shapes,
        compiler_params=_params(2),
        name="gate_up",
    )(xb, *[w[0] for w in weights])
    return h, _as_weights(wb)


SWAP_STEP = 2


def _ffn_kernel(xs_hbm, xb_ref, wg_ref, wu_ref, wd_ref, g_ref, b_ref, *rest,
                n_tiles, out_scale, want_bf16):
    if want_bf16:
        y_hbm, yb_hbm, acc, ybbuf, sem_x, sem_y, sem_yb = rest
    else:
        y_hbm, acc, sem_x, sem_y = rest
        ybbuf = None
    i = pl.program_id(0)
    j = pl.program_id(1)
    last_j = pl.num_programs(1) - 1
    tm = acc.shape[1]
    tf = wg_ref.shape[1]
    slot = i % 2
    other = 1 - slot

    def x_copy(tile, to_slot):
        return pltpu.make_async_copy(xs_hbm.at[pl.ds(tile * tm, tm), :], acc.at[to_slot],
                                     sem_x.at[to_slot])

    def y_copies(tile, from_slot):
        rows = pl.ds(tile * tm, tm)
        copies = [pltpu.make_async_copy(acc.at[from_slot], y_hbm.at[rows, :], sem_y.at[from_slot])]
        if want_bf16:
            copies.append(pltpu.make_async_copy(ybbuf, yb_hbm.at[rows, :], sem_yb))
        return copies

    @pl.when((i == 0) & (j == 0))
    def _():
        x_copy(0, 0).start()

    @pl.when(j == 0)
    def _():
        x_copy(i, slot).wait()

    gu = _dot(xb_ref[...], jnp.concatenate([wg_ref[...], wu_ref[...]], axis=1))
    h = (jax.nn.silu(gu[:, :tf]) * gu[:, tf:]).astype(BF16)
    acc[slot] += _dot(h, wd_ref[...])

    @pl.when((j == SWAP_STEP) & (i >= 1))
    def _():
        for cp in y_copies(i - 1, other):
            cp.wait()

    @pl.when((j == SWAP_STEP) & (i + 1 < n_tiles))
    def _():
        x_copy(i + 1, other).start()

    @pl.when(j == last_j)
    def _():
        _layer_norm_rows(acc.at[slot], g_ref[...], b_ref[...], acc.at[slot], ybbuf, out_scale)
        for cp in y_copies(i, slot):
            cp.start()

    @pl.when((j == last_j) & (i == n_tiles - 1))
    def _():
        for cp in y_copies(i, slot):
            cp.wait()


def _ffn_fused(xs, xb, w_gate_up, w_down, ln_g, ln_b, layer, *, tm, tf, out_scale, want_bf16):
    m, d = xs.shape
    nf = D_FF // tf
    n_tiles = m // tm
    assert nf > SWAP_STEP
    ln_spec = pl.BlockSpec((None, None, 1, d), lambda i, j: (layer, 1, 0, 0))
    any_spec = pl.BlockSpec(memory_space=pl.ANY)
    n_out = 2 if want_bf16 else 1
    res = pl.pallas_call(
        functools.partial(_ffn_kernel, n_tiles=n_tiles, out_scale=out_scale, want_bf16=want_bf16),
        grid=(n_tiles, nf),
        in_specs=[any_spec, pl.BlockSpec((tm, d), lambda i, j: (i, 0))]
        + [_wspec(w, (d, tf), lambda i, j: (0, j)) for w in w_gate_up]
        + [_wspec(w_down, (tf, d), lambda i, j: (j, 0)), ln_spec, ln_spec],
        out_specs=[any_spec] * n_out,
        out_shape=[jax.ShapeDtypeStruct((m, d), F32), jax.ShapeDtypeStruct((m, d), BF16)][:n_out],
        scratch_shapes=[pltpu.VMEM((2, tm, d), F32)]
        + ([pltpu.VMEM((tm, d), BF16)] if want_bf16 else [])
        + [pltpu.SemaphoreType.DMA((2,)), pltpu.SemaphoreType.DMA((2,))]
        + ([pltpu.SemaphoreType.DMA] if want_bf16 else []),
        compiler_params=_params(2),
        name="ffn",
    )(xs, xb, *[w[0] for w in w_gate_up], w_down[0],
      ln_g.reshape(DEPTH, 2, 1, d), ln_b.reshape(DEPTH, 2, 1, d))
    return (res[0], res[1]) if want_bf16 else (res[0], None)


def _qkv_kernel(xb_ref, wq_ref, wk_ref, wv_ref, q_ref, kb_ref, vb_ref, kf_ref, vf_ref, *wbo,
                transposed_v):
    wbo = wbo or (None, None, None)
    q, k, v = _dot_side_by_side(xb_ref[...], (wq_ref, wk_ref, wv_ref), wbo)
    kf_ref[...] = k
    vf_ref[...] = v
    for hh in range(q_ref.shape[0]):
        cols = slice(hh * HEAD_DIM, (hh + 1) * HEAD_DIM)
        q_ref[hh] = q[:, cols].astype(BF16)
        kb_ref[hh] = k[:, cols].astype(BF16)
        if transposed_v:
            vb_ref[hh] = v[:, cols].T.astype(BF16)
        else:
            vb_ref[hh] = v[:, cols].astype(BF16)


def _qkv(xb, weights, *, tm, tn, transposed_v, emit):
    m, d = xb.shape
    nj = d // tn
    hb = tn // HEAD_DIM
    assert not emit or m == tm
    hm_spec = pl.BlockSpec((hb, tm, HEAD_DIM), lambda i, j: (j, i, 0))
    hm_shape = jax.ShapeDtypeStruct((N_HEADS, m, HEAD_DIM), BF16)
    if transposed_v:
        v_spec = pl.BlockSpec((hb, HEAD_DIM, tm), lambda i, j: (j, 0, i))
        v_shape = jax.ShapeDtypeStruct((N_HEADS, HEAD_DIM, m), BF16)
    else:
        v_spec, v_shape = hm_spec, hm_shape
    wmap = lambda i, j: (0, j)
    wb_specs, wb_shapes = _wb_outputs(weights, emit, (d, tn), wmap, (d, d))
    q, kb, vb, k, v, *wb = pl.pallas_call(
        functools.partial(_qkv_kernel, transposed_v=transposed_v),
        grid=(m // tm, nj),
        in_specs=[pl.BlockSpec((tm, d), lambda i, j: (i, 0))]
        + [_wspec(w, (d, tn), wmap) for w in weights],
        out_specs=[hm_spec, hm_spec, v_spec,
                   pl.BlockSpec((tm, tn), lambda i, j: (i, j)),
                   pl.BlockSpec((tm, tn), lambda i, j: (i, j))] + wb_specs,
        out_shape=[hm_shape, hm_shape, v_shape,
                   jax.ShapeDtypeStruct((m, d), F32),
                   jax.ShapeDtypeStruct((m, d), F32)] + wb_shapes,
        compiler_params=_params(2),
        name="qkv",
    )(xb, *[w[0] for w in weights])
    return q, kb, vb, k, v, _as_weights(wb)


def _split_heads_kernel(x_ref, o_ref):
    for h in range(N_HEADS):
        o_ref[:, h, :] = x_ref[:, h * HEAD_DIM:(h + 1) * HEAD_DIM]


def _split_heads(x, *, rows, tr):
    m, d = x.shape
    first = (m - rows) // tr
    return pl.pallas_call(
        _split_heads_kernel,
        grid=(rows // tr,),
        in_specs=[pl.BlockSpec((tr, d), lambda i: (first + i, 0))],
        out_specs=pl.BlockSpec((tr, N_HEADS, HEAD_DIM), lambda i: (i, 0, 0)),
        out_shape=jax.ShapeDtypeStruct((rows, N_HEADS, HEAD_DIM), x.dtype),
        compiler_params=_params(1),
        name="split_heads",
    )(x)


def _bias_kernel(rb_ref, heads_ref, pair_ref):
    rb = rb_ref[...]
    hi = rb.astype(BF16)
    r1 = rb - hi.astype(F32)
    mid = r1.astype(BF16)
    lo = (r1 - mid.astype(F32)).astype(BF16)
    src = lax.broadcasted_iota(jnp.int32, (N_REL_PAD, T_PAD), 0)
    m = lax.broadcasted_iota(jnp.int32, (N_REL_PAD, T_PAD), 1)
    idx = jnp.clip(LEFT_ROWS + CHUNK - 1 - m, -MAX_REL, MAX_REL) + MAX_REL
    onehot = (src == idx).astype(BF16)
    t = (_dot(hi, onehot) + _dot(mid, onehot)) + _dot(lo, onehot)

    key = lax.broadcasted_iota(jnp.int32, (PAIR_BAND, PAIR), 0)
    qry = lax.broadcasted_iota(jnp.int32, (PAIR_BAND, PAIR), 1)
    in_band = ((qry < CHUNK) & (key < BAND)) | ((qry >= CHUNK) & (key >= CHUNK))
    chunk_rows = []
    for h in range(N_HEADS):
        rows = jnp.broadcast_to(t[h:h + 1, :], (PAIR, T_PAD))
        rows = pltpu.roll(rows, T_PAD - (CHUNK - 1), 1, stride=1, stride_axis=0)
        chunk_rows.append(rows[:CHUNK, :PAIR_BAND])
        pair_ref[h] = jnp.where(in_band, rows[:, :PAIR_BAND].T * LOG2E, -jnp.inf)
    for hp in range(N_HEADS // 2):
        both = jnp.concatenate(chunk_rows[2 * hp:2 * hp + 2], axis=0)
        heads_ref[hp] = both.T[:BAND] * LOG2E


def _bias_tables(rel_bias):
    rb = jnp.pad(rel_bias, ((0, 0), (0, N_REL_PAD - N_REL)))
    return pl.pallas_call(
        _bias_kernel,
        out_shape=[jax.ShapeDtypeStruct((N_HEADS // 2, BAND, 2 * CHUNK), F32),
                   jax.ShapeDtypeStruct((N_HEADS, PAIR_BAND, PAIR), F32)],
        name="rel_bias_tables",
    )(rb)


PAIRS = LEFT_ROWS // PAIR


def _attn_prompt_kernel(q_ref, kp_ref, kc_ref, vp_ref, vc_ref, bias_ref, o_ref):
    g = pl.program_id(0)
    key = lax.broadcasted_iota(jnp.int32, (PAIR_BAND, PAIR), 0)

    def head(h, first):
        bias = bias_ref[h]
        for p in range(PAIRS):
            rows = slice(p * PAIR, (p + 1) * PAIR)
            lo = slice(p * PAIR, LEFT_ROWS)
            hi = slice(0, (p + 1) * PAIR)
            kband = jnp.concatenate([kp_ref[h, lo, :], kc_ref[h, hi, :]], axis=0)
            vband = jnp.concatenate([vp_ref[h, :, lo], vc_ref[h, :, hi]], axis=1)
            s = _dot_t(kband, q_ref[h, rows, :]) * SCALE_LOG2 + bias
            if first:
                s = jnp.where(key >= LEFT_ROWS - p * PAIR, s, -jnp.inf)
            e = jnp.exp2(s - jnp.max(s, axis=0, keepdims=True))
            denom = jnp.sum(e, axis=0, keepdims=True)
            o = _dot(vband, e.astype(BF16)) * (1.0 / denom)
            o_ref[h, rows, :] = o.T.astype(BF16)

    def heads(first):
        def body(h, carry):
            head(h, first)
            return carry
        lax.fori_loop(0, N_HEADS, body, 0, unroll=8)

    pl.when(g == 0)(functools.partial(heads, True))
    pl.when(g > 0)(functools.partial(heads, False))


def _attn_prompt(q, kb, vt, bias):
    _, m, _ = q.shape
    blk = (N_HEADS, LEFT_ROWS, HEAD_DIM)
    cur = pl.BlockSpec(blk, lambda g: (0, g, 0))
    prev = pl.BlockSpec(blk, lambda g: (0, jnp.maximum(g - 1, 0), 0))
    blk_t = (N_HEADS, HEAD_DIM, LEFT_ROWS)
    cur_t = pl.BlockSpec(blk_t, lambda g: (0, 0, g))
    prev_t = pl.BlockSpec(blk_t, lambda g: (0, 0, jnp.maximum(g - 1, 0)))
    return pl.pallas_call(
        _attn_prompt_kernel,
        grid=(m // LEFT_ROWS,),
        in_specs=[cur, prev, cur, prev_t, cur_t,
                  pl.BlockSpec((N_HEADS, PAIR_BAND, PAIR), lambda g: (0, 0, 0))],
        out_specs=cur,
        out_shape=jax.ShapeDtypeStruct(q.shape, BF16),
        compiler_params=_params(1),
        name="attn_prompt",
    )(q, kb, kb, vt, vt, bias)


def _cache_copies(ck_hbm, cv_hbm, kbuf, vbuf, sem, stream, slot):
    return [pltpu.make_async_copy(src.at[stream, :, h, :],
                                  buf.at[slot, :, h * HEAD_DIM:(h + 1) * HEAD_DIM],
                                  sem.at[slot])
            for src, buf in ((ck_hbm, kbuf), (cv_hbm, vbuf)) for h in range(N_HEADS)]


def _attn_sample_kernel(q_ref, kc_ref, vc_ref, ck_hbm, cv_hbm, bias_ref, o_ref, kbuf, vbuf, sem):
    n = pl.program_id(0)
    slot = n % 2
    copies = functools.partial(_cache_copies, ck_hbm, cv_hbm, kbuf, vbuf, sem)

    @pl.when(n == 0)
    def _():
        for cp in copies(0, 0):
            cp.start()

    @pl.when(n + 1 < pl.num_programs(0))
    def _():
        for cp in copies(n + 1, 1 - slot):
            cp.start()

    for cp in copies(n, slot):
        cp.wait()

    zeros = jnp.zeros((CHUNK, HEAD_DIM), BF16)
    for hp in range(N_HEADS // 2):
        h0, h1 = 2 * hp, 2 * hp + 1
        cols = slice(h0 * HEAD_DIM, (h1 + 1) * HEAD_DIM)
        knew = jnp.concatenate([kc_ref[h0], kc_ref[h1]], axis=1)
        vnew = jnp.concatenate([vc_ref[h0], vc_ref[h1]], axis=1)
        kband = jnp.concatenate([kbuf[slot, :, cols].astype(BF16), knew], axis=0)
        vband = jnp.concatenate([vbuf[slot, :, cols].astype(BF16), vnew], axis=0)
        qbd = jnp.concatenate([jnp.concatenate([q_ref[h0], zeros], axis=1),
                               jnp.concatenate([zeros, q_ref[h1]], axis=1)], axis=0)
        s = _dot_t(kband, qbd) * SCALE_LOG2 + bias_ref[hp]
        e = jnp.exp2(s - jnp.max(s, axis=0, keepdims=True))
        denom = jnp.sum(e, axis=0, keepdims=True)
        o = lax.dot_general(vband, e.astype(BF16), (((0,), (0,)), ((), ())),
                            preferred_element_type=F32) * (1.0 / denom)
        o_ref[h0] = o[:HEAD_DIM].T[:CHUNK].astype(BF16)
        o_ref[h1] = o[HEAD_DIM:].T[CHUNK:].astype(BF16)


def _attn_sample(q, kb, vb, cache_k, cache_v, bias):
    n = cache_k.shape[0]
    new = pl.BlockSpec((N_HEADS, CHUNK, HEAD_DIM), lambda s: (0, s, 0))
    cache = pl.BlockSpec(memory_space=pl.ANY)
    return pl.pallas_call(
        _attn_sample_kernel,
        grid=(n,),
        in_specs=[new, new, new, cache, cache,
                  pl.BlockSpec((N_HEADS // 2, BAND, 2 * CHUNK), lambda s: (0, 0, 0))],
        out_specs=new,
        out_shape=jax.ShapeDtypeStruct(q.shape, BF16),
        scratch_shapes=[pltpu.VMEM((2, LEFT_ROWS, D_MODEL), F32),
                        pltpu.VMEM((2, LEFT_ROWS, D_MODEL), F32),
                        pltpu.SemaphoreType.DMA((2,))],
        compiler_params=_params(1),
        name="attn_sample",
    )(q, kb, vb, cache_k, cache_v, bias)


TM = 1024
TK_OUT = 512
TM_PROJ = 512
TF = 512
TF_FUSED = 512
TR_SPLIT = 256


def _tn_three(emit):
    return 256 if emit else 512


def _f32_weights(w_in_a, w_out_a, w_kv, w_q, w_o, w_gate_up, w_down):
    return dict(
        w_in=tuple((w_in_a, 0, t * D_MODEL) for t in range(3)),
        w_out=(w_out_a, 0, 0),
        w_qkv=((w_q, 0, 0), (w_kv, 0, 0), (w_kv, 0, D_MODEL)),
        w_o=(w_o, 0, 0),
        w_gate_up=tuple(tuple((w_gate_up, l, t * D_FF) for t in range(2)) for l in range(DEPTH)),
        w_down=tuple((w_down, l, 0) for l in range(DEPTH)),
    )


def _trunk(x, hist, seq_len, cache, bias_tables, kv_rows, w, conv_w, ln_g, ln_b, *, emit):
    wb = dict(w_gate_up=[None] * DEPTH, w_down=[None] * DEPTH)

    def ffn(x, xb, layer):
        if not emit:
            more = layer + 1 < DEPTH
            return _ffn_fused(x, xb, w["w_gate_up"][layer], w["w_down"][layer], ln_g, ln_b, layer,
                              tm=TM, tf=TF_FUSED, out_scale=ALPHA if more else 1.0,
                              want_bf16=more)
        h, wb["w_gate_up"][layer] = _gate_up(xb, w["w_gate_up"][layer], tm=TM, tf=TF, emit=emit)
        x, xb, (wb["w_down"][layer],) = _pad1(_mm_res_ln(
            h, w["w_down"][layer], x, ln_g, ln_b, (layer, 1), tm=TM, tk=TK_OUT, emit=emit))
        return x, xb

    def proj(a, name, ln_idx, prescaled):
        if emit:
            x_new, xb_new, (wb[name],) = _pad1(_mm_res_ln(a, w[name], x, ln_g, ln_b, ln_idx,
                                                          tm=TM, tk=TK_OUT, emit=True))
            return x_new, xb_new
        return _proj_res_ln(a, w[name], x, ln_g, ln_b, ln_idx, tm=TM_PROJ, prescaled=prescaled)

    g, conv_state, wb["w_in"] = _conv_gate(x, w["w_in"], hist, conv_w, seq_len=seq_len,
                                           tm=TM, tn=_tn_three(emit), emit=emit)
    x, xb = proj(g, "w_out", (0, 0), prescaled=False)
    x, xb = ffn(x, xb, 0)

    bias_heads, bias_pair = bias_tables
    q, kb, vb, k, v, wb["w_qkv"] = _qkv(xb, w["w_qkv"], tm=TM, tn=_tn_three(emit),
                                        transposed_v=cache is None, emit=emit)
    if cache is None:
        att = _attn_prompt(q, kb, vb, bias_pair)
    else:
        att = _attn_sample(q, kb, vb, cache[0], cache[1], bias_heads)
    x, xb = proj(att, "w_o", (1, 0), prescaled=True)
    x, _ = ffn(x, xb, 1)
    k = _split_heads(k, rows=kv_rows, tr=TR_SPLIT)
    v = _split_heads(v, rows=kv_rows, tr=TR_SPLIT)
    return (x, conv_state, k, v), wb


def _pad1(res):
    x, xb, wts = res
    return x, xb, (wts + (None,))[:1]


def kernel(x_prompt, x_sample, state_conv, cache_k, cache_v, w_in_a, conv_w, w_out_a, w_kv, w_q,
           w_o, rel_bias, ln_g, ln_b, w_gate_up, w_down):
    batch, seq, d = x_prompt.shape
    dec_batch, dec_seq, _ = x_sample.shape
    assert batch == 1 and DEPTH == 2 and dec_seq == CHUNK and cache_k.shape[1] == LEFT_ROWS
    assert dec_batch * dec_seq == TM
    bias_tables = _bias_tables(rel_bias[0])
    w_f32 = _f32_weights(w_in_a, w_out_a, w_kv, w_q, w_o, w_gate_up, w_down)

    (y_s, conv_s, k_s, v_s), w_bf16 = _trunk(
        x_sample.reshape(dec_batch * dec_seq, d), state_conv[0], dec_seq, (cache_k, cache_v),
        bias_tables, dec_batch * dec_seq, w_f32, conv_w, ln_g, ln_b, emit=True)
    conv_zero = jnp.zeros((batch, CONV_W - 1, d), x_prompt.dtype)
    (y_p, conv_p, k_p, v_p), _ = _trunk(
        x_prompt.reshape(seq, d), conv_zero, seq, None, bias_tables, LEFT_ROWS,
        w_bf16, conv_w, ln_g, ln_b, emit=False)

    kv_prompt = (batch, LEFT_ROWS, N_HEADS, HEAD_DIM)
    kv_sample = (dec_batch, dec_seq, N_HEADS, HEAD_DIM)
    return (y_p.reshape(batch, seq, d),
            y_s.reshape(dec_batch, dec_seq, d),
            conv_p.reshape(1, batch, CONV_W - 1, d),
            conv_s.reshape(1, dec_batch, CONV_W - 1, d),
            k_p.reshape(kv_prompt),
            v_p.reshape(kv_prompt),
            k_s.reshape(kv_sample),
            v_s.reshape(kv_sample))
```

```python
import functools

import jax
import jax.numpy as jnp
from jax import lax
from jax.experimental import pallas as pl
from jax.experimental.pallas import tpu as pltpu

D_MODEL = 2048
DEPTH = 2
CHUNK = 64
N_LEFT_CHUNKS = 8
LEFT_ROWS = N_LEFT_CHUNKS * CHUNK
BAND = LEFT_ROWS + CHUNK
N_HEADS = 16
HEAD_DIM = D_MODEL // N_HEADS
MAX_REL = 256
N_REL = 2 * MAX_REL + 1
CONV_W = 3
D_FF = 5632
ALPHA = (2.0 * DEPTH) ** 0.25
LN_EPS = 1e-5
SCALE = HEAD_DIM ** -0.5
LOG2E = 1.4426950408889634
SCALE_LOG2 = SCALE * LOG2E

LANES = 128
VMEM_LIMIT_BYTES = 56 * 1024 * 1024

PAIR = 2 * CHUNK
PAIR_BAND = BAND + CHUNK
N_REL_PAD = 640
T_PAD = 768

BF16 = jnp.bfloat16
F32 = jnp.float32


def _params(n_axes):
    return pltpu.CompilerParams(dimension_semantics=("arbitrary",) * n_axes,
                                vmem_limit_bytes=VMEM_LIMIT_BYTES)


def _dot(a, b):
    return jnp.dot(a, b, preferred_element_type=F32)


def _dot_t(a, b):
    return lax.dot_general(a, b, (((1,), (1,)), ((), ())), preferred_element_type=F32)


def _wspec(weight, blk, imap):
    arr, layer, first_col = weight
    first = first_col // blk[1]
    if arr.ndim == 3:
        return pl.BlockSpec((None,) + blk, lambda *g: (layer,) + _shift(imap(*g), first))
    return pl.BlockSpec(blk, lambda *g: _shift(imap(*g), first))


def _shift(idx, first):
    return (idx[0], idx[1] + first)


def _wload(w_ref, wb_ref):
    w = w_ref[...]
    if w.dtype != BF16:
        w = w.astype(BF16)
    if wb_ref is not None:
        wb_ref[...] = w
    return w


def _dot_side_by_side(x, w_refs, wb_refs):
    tn = w_refs[0].shape[1]
    w = jnp.concatenate([_wload(w_ref, wb_ref) for w_ref, wb_ref in zip(w_refs, wb_refs)], axis=1)
    y = _dot(x, w)
    return [y[:, t * tn:(t + 1) * tn] for t in range(len(w_refs))]


def _wb_outputs(weights, emit, blk, imap, shape):
    if not emit:
        return [], []
    return ([pl.BlockSpec(blk, imap)] * len(weights),
            [jax.ShapeDtypeStruct(shape, BF16)] * len(weights))


def _as_weights(arrs):
    return tuple((a, 0, 0) for a in arrs)


def _conv_gate_kernel(x_ref, wb_ref, wc_ref, wh_ref, hist_ref, cw_ref, g_ref, state_ref, *rest,
                      nseq, emit, carried):
    rest = list(rest)
    wbo = [rest.pop(0) for _ in range(3)] if emit else [None] * 3
    xb_ref = rest.pop(0)
    i = pl.program_id(0)
    j = pl.program_id(1)
    tm, tn = g_ref.shape
    ls = tm // nseq

    @pl.when(j == 0)
    def _():
        xb_ref[...] = x_ref[...].astype(BF16)

    b, c, h = _dot_side_by_side(xb_ref[...], (wb_ref, wc_ref, wh_ref), wbo)
    u = c * h

    if carried:
        carry_ref = rest.pop(0)

        @pl.when(i == 0)
        def _():
            carry_ref[j] = hist_ref[0]

        prev = carry_ref[j][None]
    else:
        prev = hist_ref[...]

    shape3 = (nseq, ls, tn)
    pos = lax.broadcasted_iota(jnp.int32, shape3, 1)
    u3 = u.reshape(shape3)
    p1 = pltpu.roll(u, 1, 0).reshape(shape3)
    p2 = pltpu.roll(u, 2, 0).reshape(shape3)
    h0 = prev[:, 0:1, :]
    h1 = prev[:, 1:2, :]
    p1 = jnp.where(pos == 0, h1, p1)
    p2 = jnp.where(pos == 0, h0, jnp.where(pos == 1, h1, p2))
    conv = cw_ref[0:1, :] * p2 + cw_ref[1:2, :] * p1 + cw_ref[2:3, :] * u3
    g_ref[...] = (b * conv.reshape(tm, tn)).astype(BF16)

    new_state = u3[:, ls - 2:ls, :]
    state_ref[...] = new_state
    if carried:
        carry_ref[j] = new_state[0]


def _conv_gate(x, weights, hist, conv_w, *, seq_len, tm, tn, emit):
    m, d = x.shape
    nj = d // tn
    carried = seq_len > tm
    nseq = 1 if carried else tm // seq_len
    assert not emit or m == tm
    hist_map = (lambda i, j: (0, 0, j)) if carried else (lambda i, j: (i, 0, j))
    scratch = [pltpu.VMEM((tm, d), BF16)]
    if carried:
        scratch.append(pltpu.VMEM((nj, CONV_W - 1, tn), F32))
    wmap = lambda i, j: (0, j)
    wb_specs, wb_shapes = _wb_outputs(weights, emit, (d, tn), wmap, (d, d))
    n_states = (m // tm) * nseq
    g, states, *wb = pl.pallas_call(
        functools.partial(_conv_gate_kernel, nseq=nseq, emit=emit, carried=carried),
        grid=(m // tm, nj),
        in_specs=[pl.BlockSpec((tm, d), lambda i, j: (i, 0))]
        + [_wspec(w, (d, tn), wmap) for w in weights]
        + [pl.BlockSpec((nseq, CONV_W - 1, tn), hist_map),
           pl.BlockSpec((None, CONV_W, tn), lambda i, j: (0, 0, j))],
        out_specs=[
            pl.BlockSpec((tm, tn), lambda i, j: (i, j)),
            pl.BlockSpec((nseq, CONV_W - 1, tn), lambda i, j: (i, 0, j)),
        ] + wb_specs,
        out_shape=[
            jax.ShapeDtypeStruct((m, d), BF16),
            jax.ShapeDtypeStruct((n_states, CONV_W - 1, d), F32),
        ] + wb_shapes,
        scratch_shapes=scratch,
        compiler_params=_params(2),
        name="conv_gate",
    )(x, *[w[0] for w in weights], hist, conv_w)
    return g, states[-hist.shape[0]:], _as_weights(wb)


LN_ROWS = 64


def _layer_norm_rows(y, gain, bias, out_ref, outb_ref, out_scale=1.0):
    for r in range(0, y.shape[0], LN_ROWS):
        rows = slice(r, r + LN_ROWS)
        yr = y[rows, :]
        mu = jnp.mean(yr, axis=-1, keepdims=True)
        yc = yr - mu
        var = jnp.mean(yc * yc, axis=-1, keepdims=True)
        z = yc * lax.rsqrt(var + LN_EPS) * gain + bias
        out_ref[rows, :] = z if out_scale == 1.0 else z * out_scale
        if outb_ref is not None:
            outb_ref[rows, :] = z.astype(BF16)


def _mm_res_ln_kernel(a_ref, w_ref, x_ref, g_ref, b_ref, out_ref, outb_ref, *wbo, prescaled):
    k = pl.program_id(1)

    @pl.when(k == 0)
    def _():
        out_ref[...] = x_ref[...] if prescaled else ALPHA * x_ref[...]

    if len(a_ref.shape) == 3:
        a = jnp.concatenate([a_ref[h] for h in range(a_ref.shape[0])], axis=1)
    else:
        a = a_ref[...]
    out_ref[...] += _dot(a, _wload(w_ref, wbo[0] if wbo else None))

    @pl.when(k == pl.num_programs(1) - 1)
    def _():
        _layer_norm_rows(out_ref, g_ref[...], b_ref[...], out_ref, outb_ref, out_scale=ALPHA)


def _mm_res_ln(a, weight, x, ln_g, ln_b, ln_idx, *, tm, tk, emit, prescaled):
    m, d = x.shape
    kdim = weight[0].shape[-2]
    assert not emit or m == tm
    if a.ndim == 3:
        a_spec = pl.BlockSpec((tk // HEAD_DIM, tm, HEAD_DIM), lambda i, k: (k, i, 0))
    else:
        a_spec = pl.BlockSpec((tm, tk), lambda i, k: (i, k))
    wmap = lambda i, k: (k, 0)
    wb_specs, wb_shapes = _wb_outputs((weight,), emit, (tk, d), wmap, (kdim, d))
    layer, sub = ln_idx
    ln_spec = pl.BlockSpec((None, None, 1, d), lambda i, k: (layer, sub, 0, 0))
    x_new, xb_new, *wb = pl.pallas_call(
        functools.partial(_mm_res_ln_kernel, prescaled=prescaled),
        grid=(m // tm, kdim // tk),
        in_specs=[
            a_spec,
            _wspec(weight, (tk, d), wmap),
            pl.BlockSpec((tm, d), lambda i, k: (i, 0)),
            ln_spec,
            ln_spec,
        ],
        out_specs=[
            pl.BlockSpec((tm, d), lambda i, k: (i, 0)),
            pl.BlockSpec((tm, d), lambda i, k: (i, 0)),
        ] + wb_specs,
        out_shape=[
            jax.ShapeDtypeStruct((m, d), F32),
            jax.ShapeDtypeStruct((m, d), BF16),
        ] + wb_shapes,
        compiler_params=_params(2),
        name="mm_res_ln",
    )(a, weight[0], x, ln_g.reshape(DEPTH, 2, 1, d), ln_b.reshape(DEPTH, 2, 1, d))
    return x_new, xb_new, _as_weights(wb)


def _proj_res_ln_kernel(a_ref, w_ref, x_ref, g_ref, b_ref, out_ref, outb_ref, *, prescaled):
    if len(a_ref.shape) == 3:
        a = jnp.concatenate([a_ref[h] for h in range(a_ref.shape[0])], axis=1)
    else:
        a = a_ref[...]
    residual = x_ref[...] if prescaled else ALPHA * x_ref[...]
    y = residual + _dot(a, w_ref[...])
    _layer_norm_rows(y, g_ref[...], b_ref[...], out_ref, outb_ref, out_scale=ALPHA)


def _proj_res_ln(a, weight, x, ln_g, ln_b, ln_idx, *, tm, prescaled):
    m, d = x.shape
    w = weight[0]
    assert w.shape == (d, d) and w.dtype == BF16
    if a.ndim == 3:
        a_spec = pl.BlockSpec((N_HEADS, tm, HEAD_DIM), lambda i: (0, i, 0))
    else:
        a_spec = pl.BlockSpec((tm, d), lambda i: (i, 0))
    layer, sub = ln_idx
    ln_spec = pl.BlockSpec((None, None, 1, d), lambda i: (layer, sub, 0, 0))
    return pl.pallas_call(
        functools.partial(_proj_res_ln_kernel, prescaled=prescaled),
        grid=(m // tm,),
        in_specs=[
            a_spec,
            pl.BlockSpec((d, d), lambda i: (0, 0)),
            pl.BlockSpec((tm, d), lambda i: (i, 0)),
            ln_spec,
            ln_spec,
        ],
        out_specs=[
            pl.BlockSpec((tm, d), lambda i: (i, 0)),
            pl.BlockSpec((tm, d), lambda i: (i, 0)),
        ],
        out_shape=[
            jax.ShapeDtypeStruct((m, d), F32),
            jax.ShapeDtypeStruct((m, d), BF16),
        ],
        compiler_params=_params(1),
        name="proj_res_ln",
    )(a, w, x, ln_g.reshape(DEPTH, 2, 1, d), ln_b.reshape(DEPTH, 2, 1, d))


SWAP_STEP = 2


def _ffn_kernel(xs_hbm, xb_ref, wg_ref, wu_ref, wd_ref, g_ref, b_ref, *rest,
                n_tiles, out_scale, want_bf16, emit):
    rest = list(rest)
    y_hbm = rest.pop(0)
    yb_hbm = rest.pop(0) if want_bf16 else None
    wbo = [rest.pop(0) for _ in range(3)] if emit else [None] * 3
    acc = rest.pop(0)
    ybbuf = rest.pop(0) if want_bf16 else None
    sem_x, sem_y = rest.pop(0), rest.pop(0)
    sem_yb = rest.pop(0) if want_bf16 else None
    i = pl.program_id(0)
    j = pl.program_id(1)
    last_j = pl.num_programs(1) - 1
    n_slots, tm = acc.shape[:2]
    tf = wg_ref.shape[1]
    slot = i % n_slots
    other = (i + 1) % n_slots

    def x_copy(tile, to_slot):
        return pltpu.make_async_copy(xs_hbm.at[pl.ds(tile * tm, tm), :], acc.at[to_slot],
                                     sem_x.at[to_slot])

    def y_copies(tile, from_slot):
        rows = pl.ds(tile * tm, tm)
        copies = [pltpu.make_async_copy(acc.at[from_slot], y_hbm.at[rows, :], sem_y.at[from_slot])]
        if want_bf16:
            copies.append(pltpu.make_async_copy(ybbuf, yb_hbm.at[rows, :], sem_yb))
        return copies

    @pl.when((i == 0) & (j == 0))
    def _():
        x_copy(0, 0).start()

    @pl.when(j == 0)
    def _():
        x_copy(i, slot).wait()

    w_gate_up = jnp.concatenate([_wload(wg_ref, wbo[0]), _wload(wu_ref, wbo[1])], axis=1)
    gu = _dot(xb_ref[...], w_gate_up)
    h = (jax.nn.silu(gu[:, :tf]) * gu[:, tf:]).astype(BF16)
    acc[slot] += _dot(h, _wload(wd_ref, wbo[2]))

    @pl.when((j == SWAP_STEP) & (i >= 1))
    def _():
        for cp in y_copies(i - 1, other):
            cp.wait()

    @pl.when((j == SWAP_STEP) & (i + 1 < n_tiles))
    def _():
        x_copy(i + 1, other).start()

    @pl.when(j == last_j)
    def _():
        _layer_norm_rows(acc.at[slot], g_ref[...], b_ref[...], acc.at[slot], ybbuf, out_scale)
        for cp in y_copies(i, slot):
            cp.start()

    @pl.when((j == last_j) & (i == n_tiles - 1))
    def _():
        for cp in y_copies(i, slot):
            cp.wait()


def _ffn_fused(xs, xb, w_gate_up, w_down, ln_g, ln_b, layer, *, tm, tf, out_scale, want_bf16,
               emit):
    m, d = xs.shape
    nf = D_FF // tf
    n_tiles = m // tm
    n_slots = min(2, n_tiles)
    assert nf > SWAP_STEP and (not emit or n_tiles == 1)
    ln_spec = pl.BlockSpec((None, None, 1, d), lambda i, j: (layer, 1, 0, 0))
    any_spec = pl.BlockSpec(memory_space=pl.ANY)
    n_out = 2 if want_bf16 else 1
    up_map = lambda i, j: (0, j)
    down_map = lambda i, j: (j, 0)
    up_specs, up_shapes = _wb_outputs(w_gate_up, emit, (d, tf), up_map, (d, D_FF))
    down_specs, down_shapes = _wb_outputs((w_down,), emit, (tf, d), down_map, (D_FF, d))
    res = pl.pallas_call(
        functools.partial(_ffn_kernel, n_tiles=n_tiles, out_scale=out_scale, want_bf16=want_bf16,
                          emit=emit),
        grid=(n_tiles, nf),
        in_specs=[any_spec, pl.BlockSpec((tm, d), lambda i, j: (i, 0))]
        + [_wspec(w, (d, tf), up_map) for w in w_gate_up]
        + [_wspec(w_down, (tf, d), down_map), ln_spec, ln_spec],
        out_specs=[any_spec] * n_out + up_specs + down_specs,
        out_shape=[jax.ShapeDtypeStruct((m, d), F32), jax.ShapeDtypeStruct((m, d), BF16)][:n_out]
        + up_shapes + down_shapes,
        scratch_shapes=[pltpu.VMEM((n_slots, tm, d), F32)]
        + ([pltpu.VMEM((tm, d), BF16)] if want_bf16 else [])
        + [pltpu.SemaphoreType.DMA((n_slots,)), pltpu.SemaphoreType.DMA((n_slots,))]
        + ([pltpu.SemaphoreType.DMA] if want_bf16 else []),
        compiler_params=_params(2),
        name="ffn",
    )(xs, xb, *[w[0] for w in w_gate_up], w_down[0],
      ln_g.reshape(DEPTH, 2, 1, d), ln_b.reshape(DEPTH, 2, 1, d))
    wb = _as_weights(res[n_out:])
    return res[0], (res[1] if want_bf16 else None), wb[:2], wb[2:]


def _qkv_kernel(xb_ref, wq_ref, wk_ref, wv_ref, q_ref, kb_ref, vb_ref, kf_ref, vf_ref, *wbo,
                transposed_v):
    wbo = wbo or (None, None, None)
    q, k, v = _dot_side_by_side(xb_ref[...], (wq_ref, wk_ref, wv_ref), wbo)
    kf_ref[...] = k
    vf_ref[...] = v
    for hh in range(q_ref.shape[0]):
        cols = slice(hh * HEAD_DIM, (hh + 1) * HEAD_DIM)
        q_ref[hh] = (q[:, cols] * SCALE_LOG2).astype(BF16)
        kb_ref[hh] = k[:, cols].astype(BF16)
        if transposed_v:
            vb_ref[hh] = v[:, cols].T.astype(BF16)
        else:
            vb_ref[hh] = v[:, cols].astype(BF16)


def _qkv(xb, weights, *, tm, tn, transposed_v, emit):
    m, d = xb.shape
    nj = d // tn
    hb = tn // HEAD_DIM
    assert not emit or m == tm
    hm_spec = pl.BlockSpec((hb, tm, HEAD_DIM), lambda i, j: (j, i, 0))
    hm_shape = jax.ShapeDtypeStruct((N_HEADS, m, HEAD_DIM), BF16)
    if transposed_v:
        v_spec = pl.BlockSpec((hb, HEAD_DIM, tm), lambda i, j: (j, 0, i))
        v_shape = jax.ShapeDtypeStruct((N_HEADS, HEAD_DIM, m), BF16)
    else:
        v_spec, v_shape = hm_spec, hm_shape
    wmap = lambda i, j: (0, j)
    wb_specs, wb_shapes = _wb_outputs(weights, emit, (d, tn), wmap, (d, d))
    q, kb, vb, k, v, *wb = pl.pallas_call(
        functools.partial(_qkv_kernel, transposed_v=transposed_v),
        grid=(m // tm, nj),
        in_specs=[pl.BlockSpec((tm, d), lambda i, j: (i, 0))]
        + [_wspec(w, (d, tn), wmap) for w in weights],
        out_specs=[hm_spec, hm_spec, v_spec,
                   pl.BlockSpec((tm, tn), lambda i, j: (i, j)),
                   pl.BlockSpec((tm, tn), lambda i, j: (i, j))] + wb_specs,
        out_shape=[hm_shape, hm_shape, v_shape,
                   jax.ShapeDtypeStruct((m, d), F32),
                   jax.ShapeDtypeStruct((m, d), F32)] + wb_shapes,
        compiler_params=_params(2),
        name="qkv",
    )(xb, *[w[0] for w in weights])
    return q, kb, vb, k, v, _as_weights(wb)


def _split_heads_kernel(x_ref, o_ref):
    for h in range(N_HEADS):
        o_ref[:, h, :] = x_ref[:, h * HEAD_DIM:(h + 1) * HEAD_DIM]


def _split_heads(x, *, rows, tr):
    m, d = x.shape
    first = (m - rows) // tr
    return pl.pallas_call(
        _split_heads_kernel,
        grid=(rows // tr,),
        in_specs=[pl.BlockSpec((tr, d), lambda i: (first + i, 0))],
        out_specs=pl.BlockSpec((tr, N_HEADS, HEAD_DIM), lambda i: (i, 0, 0)),
        out_shape=jax.ShapeDtypeStruct((rows, N_HEADS, HEAD_DIM), x.dtype),
        compiler_params=_params(1),
        name="split_heads",
    )(x)


def _bias_kernel(rb_ref, heads_ref, pair_ref):
    rb = rb_ref[...]
    hi = rb.astype(BF16)
    r1 = rb - hi.astype(F32)
    mid = r1.astype(BF16)
    lo = (r1 - mid.astype(F32)).astype(BF16)
    src = lax.broadcasted_iota(jnp.int32, (N_REL_PAD, T_PAD), 0)
    m = lax.broadcasted_iota(jnp.int32, (N_REL_PAD, T_PAD), 1)
    idx = jnp.clip(LEFT_ROWS + CHUNK - 1 - m, -MAX_REL, MAX_REL) + MAX_REL
    onehot = (src == idx).astype(BF16)
    t = (_dot(hi, onehot) + _dot(mid, onehot)) + _dot(lo, onehot)

    key = lax.broadcasted_iota(jnp.int32, (PAIR_BAND, PAIR), 0)
    qry = lax.broadcasted_iota(jnp.int32, (PAIR_BAND, PAIR), 1)
    in_band = ((qry < CHUNK) & (key < BAND)) | ((qry >= CHUNK) & (key >= CHUNK))
    chunk_rows = []
    for h in range(N_HEADS):
        rows = jnp.broadcast_to(t[h:h + 1, :], (PAIR, T_PAD))
        rows = pltpu.roll(rows, T_PAD - (CHUNK - 1), 1, stride=1, stride_axis=0)
        chunk_rows.append(rows[:CHUNK, :PAIR_BAND])
        pair_ref[h] = jnp.where(in_band, rows[:, :PAIR_BAND].T * LOG2E, -jnp.inf)
    for hp in range(N_HEADS // 2):
        both = jnp.concatenate(chunk_rows[2 * hp:2 * hp + 2], axis=0)
        heads_ref[hp] = both.T[:BAND] * LOG2E


def _bias_tables(rel_bias):
    rb = jnp.pad(rel_bias, ((0, 0), (0, N_REL_PAD - N_REL)))
    return pl.pallas_call(
        _bias_kernel,
        out_shape=[jax.ShapeDtypeStruct((N_HEADS // 2, BAND, 2 * CHUNK), F32),
                   jax.ShapeDtypeStruct((N_HEADS, PAIR_BAND, PAIR), F32)],
        name="rel_bias_tables",
    )(rb)


PAIRS = LEFT_ROWS // PAIR


def _attn_prompt_kernel(q_ref, kp_ref, kc_ref, vp_ref, vc_ref, bias_ref, o_ref):
    g = pl.program_id(0)
    key = lax.broadcasted_iota(jnp.int32, (PAIR_BAND, PAIR), 0)

    def head(h, first):
        bias = bias_ref[h]
        for p in range(PAIRS):
            rows = slice(p * PAIR, (p + 1) * PAIR)
            lo = slice(p * PAIR, LEFT_ROWS)
            hi = slice(0, (p + 1) * PAIR)
            kband = jnp.concatenate([kp_ref[h, lo, :], kc_ref[h, hi, :]], axis=0)
            vband = jnp.concatenate([vp_ref[h, :, lo], vc_ref[h, :, hi]], axis=1)
            s = _dot_t(kband, q_ref[h, rows, :]) + bias
            if first:
                s = jnp.where(key >= LEFT_ROWS - p * PAIR, s, -jnp.inf)
            e = jnp.exp2(s - jnp.max(s, axis=0, keepdims=True))
            denom = jnp.sum(e, axis=0, keepdims=True)
            o = _dot(vband, e.astype(BF16)) * (1.0 / denom)
            o_ref[h, rows, :] = o.T.astype(BF16)

    def heads(first):
        def body(h, carry):
            head(h, first)
            return carry
        lax.fori_loop(0, N_HEADS, body, 0, unroll=8)

    pl.when(g == 0)(functools.partial(heads, True))
    pl.when(g > 0)(functools.partial(heads, False))


def _attn_prompt(q, kb, vt, bias):
    _, m, _ = q.shape
    blk = (N_HEADS, LEFT_ROWS, HEAD_DIM)
    cur = pl.BlockSpec(blk, lambda g: (0, g, 0))
    prev = pl.BlockSpec(blk, lambda g: (0, jnp.maximum(g - 1, 0), 0))
    blk_t = (N_HEADS, HEAD_DIM, LEFT_ROWS)
    cur_t = pl.BlockSpec(blk_t, lambda g: (0, 0, g))
    prev_t = pl.BlockSpec(blk_t, lambda g: (0, 0, jnp.maximum(g - 1, 0)))
    return pl.pallas_call(
        _attn_prompt_kernel,
        grid=(m // LEFT_ROWS,),
        in_specs=[cur, prev, cur, prev_t, cur_t,
                  pl.BlockSpec((N_HEADS, PAIR_BAND, PAIR), lambda g: (0, 0, 0))],
        out_specs=cur,
        out_shape=jax.ShapeDtypeStruct(q.shape, BF16),
        compiler_params=_params(1),
        name="attn_prompt",
    )(q, kb, kb, vt, vt, bias)


def _cache_copies(ck_hbm, cv_hbm, kbuf, vbuf, sem, stream, slot):
    return [pltpu.make_async_copy(src.at[stream, :, h, :],
                                  buf.at[slot, :, h * HEAD_DIM:(h + 1) * HEAD_DIM],
                                  sem.at[slot])
            for src, buf in ((ck_hbm, kbuf), (cv_hbm, vbuf)) for h in range(N_HEADS)]


def _attn_sample_kernel(q_ref, kc_ref, vc_ref, ck_hbm, cv_hbm, bias_ref, o_ref, kbuf, vbuf, sem):
    n = pl.program_id(0)
    slot = n % 2
    copies = functools.partial(_cache_copies, ck_hbm, cv_hbm, kbuf, vbuf, sem)

    @pl.when(n == 0)
    def _():
        for cp in copies(0, 0):
            cp.start()

    @pl.when(n + 1 < pl.num_programs(0))
    def _():
        for cp in copies(n + 1, 1 - slot):
            cp.start()

    for cp in copies(n, slot):
        cp.wait()

    zeros = jnp.zeros((CHUNK, HEAD_DIM), BF16)
    for hp in range(N_HEADS // 2):
        h0, h1 = 2 * hp, 2 * hp + 1
        cols = slice(h0 * HEAD_DIM, (h1 + 1) * HEAD_DIM)
        knew = jnp.concatenate([kc_ref[h0], kc_ref[h1]], axis=1)
        vnew = jnp.concatenate([vc_ref[h0], vc_ref[h1]], axis=1)
        kband = jnp.concatenate([kbuf[slot, :, cols].astype(BF16), knew], axis=0)
        vband = jnp.concatenate([vbuf[slot, :, cols].astype(BF16), vnew], axis=0)
        qbd = jnp.concatenate([jnp.concatenate([q_ref[h0], zeros], axis=1),
                               jnp.concatenate([zeros, q_ref[h1]], axis=1)], axis=0)
        s = _dot_t(kband, qbd) + bias_ref[hp]
        e = jnp.exp2(s - jnp.max(s, axis=0, keepdims=True))
        denom = jnp.sum(e, axis=0, keepdims=True)
        o = lax.dot_general(vband, e.astype(BF16), (((0,), (0,)), ((), ())),
                            preferred_element_type=F32) * (1.0 / denom)
        o_ref[h0] = o[:HEAD_DIM].T[:CHUNK].astype(BF16)
        o_ref[h1] = o[HEAD_DIM:].T[CHUNK:].astype(BF16)


def _attn_sample(q, kb, vb, cache_k, cache_v, bias):
    n = cache_k.shape[0]
    new = pl.BlockSpec((N_HEADS, CHUNK, HEAD_DIM), lambda s: (0, s, 0))
    cache = pl.BlockSpec(memory_space=pl.ANY)
    return pl.pallas_call(
        _attn_sample_kernel,
        grid=(n,),
        in_specs=[new, new, new, cache, cache,
                  pl.BlockSpec((N_HEADS // 2, BAND, 2 * CHUNK), lambda s: (0, 0, 0))],
        out_specs=new,
        out_shape=jax.ShapeDtypeStruct(q.shape, BF16),
        scratch_shapes=[pltpu.VMEM((2, LEFT_ROWS, D_MODEL), F32),
                        pltpu.VMEM((2, LEFT_ROWS, D_MODEL), F32),
                        pltpu.SemaphoreType.DMA((2,))],
        compiler_params=_params(1),
        name="attn_sample",
    )(q, kb, vb, cache_k, cache_v, bias)


TM = 1024
TK_OUT = 512
TM_PROJ = 512
TR_SPLIT = 256


def _tn_three(emit):
    return 256 if emit else 512


def _tf_fused(emit):
    return 256 if emit else 512


def _f32_weights(w_in_a, w_out_a, w_kv, w_q, w_o, w_gate_up, w_down):
    return dict(
        w_in=tuple((w_in_a, 0, t * D_MODEL) for t in range(3)),
        w_out=(w_out_a, 0, 0),
        w_qkv=((w_q, 0, 0), (w_kv, 0, 0), (w_kv, 0, D_MODEL)),
        w_o=(w_o, 0, 0),
        w_gate_up=tuple(tuple((w_gate_up, l, t * D_FF) for t in range(2)) for l in range(DEPTH)),
        w_down=tuple((w_down, l, 0) for l in range(DEPTH)),
    )


def _trunk(x, hist, seq_len, cache, bias_tables, kv_rows, w, conv_w, ln_g, ln_b, *, emit):
    wb = dict(w_gate_up=[None] * DEPTH, w_down=[None] * DEPTH)

    def ffn(x, xb, layer):
        more = layer + 1 < DEPTH
        x, xb, wb["w_gate_up"][layer], down = _ffn_fused(
            x, xb, w["w_gate_up"][layer], w["w_down"][layer], ln_g, ln_b, layer, tm=TM,
            tf=_tf_fused(emit), out_scale=ALPHA if more else 1.0, want_bf16=more, emit=emit)
        wb["w_down"][layer] = down[0] if emit else None
        return x, xb

    def proj(a, name, ln_idx, prescaled):
        if emit:
            x_new, xb_new, (wb[name],) = _mm_res_ln(a, w[name], x, ln_g, ln_b, ln_idx, tm=TM,
                                                    tk=TK_OUT, emit=True, prescaled=prescaled)
            return x_new, xb_new
        return _proj_res_ln(a, w[name], x, ln_g, ln_b, ln_idx, tm=TM_PROJ, prescaled=prescaled)

    g, conv_state, wb["w_in"] = _conv_gate(x, w["w_in"], hist, conv_w, seq_len=seq_len,
                                           tm=TM, tn=_tn_three(emit), emit=emit)
    x, xb = proj(g, "w_out", (0, 0), prescaled=False)
    x, xb = ffn(x, xb, 0)

    bias_heads, bias_pair = bias_tables
    q, kb, vb, k, v, wb["w_qkv"] = _qkv(xb, w["w_qkv"], tm=TM, tn=_tn_three(emit),
                                        transposed_v=cache is None, emit=emit)
    if cache is None:
        att = _attn_prompt(q, kb, vb, bias_pair)
    else:
        att = _attn_sample(q, kb, vb, cache[0], cache[1], bias_heads)
    x, xb = proj(att, "w_o", (1, 0), prescaled=True)
    x, _ = ffn(x, xb, 1)
    k = _split_heads(k, rows=kv_rows, tr=TR_SPLIT)
    v = _split_heads(v, rows=kv_rows, tr=TR_SPLIT)
    return (x, conv_state, k, v), wb


def kernel(x_prompt, x_sample, state_conv, cache_k, cache_v, w_in_a, conv_w, w_out_a, w_kv, w_q,
           w_o, rel_bias, ln_g, ln_b, w_gate_up, w_down):
    batch, seq, d = x_prompt.shape
    dec_batch, dec_seq, _ = x_sample.shape
    assert batch == 1 and DEPTH == 2 and dec_seq == CHUNK and cache_k.shape[1] == LEFT_ROWS
    assert dec_batch * dec_seq == TM
    bias_tables = _bias_tables(rel_bias[0])
    w_f32 = _f32_weights(w_in_a, w_out_a, w_kv, w_q, w_o, w_gate_up, w_down)

    (y_s, conv_s, k_s, v_s), w_bf16 = _trunk(
        x_sample.reshape(dec_batch * dec_seq, d), state_conv[0], dec_seq, (cache_k, cache_v),
        bias_tables, dec_batch * dec_seq, w_f32, conv_w, ln_g, ln_b, emit=True)
    conv_zero = jnp.zeros((batch, CONV_W - 1, d), x_prompt.dtype)
    (y_p, conv_p, k_p, v_p), _ = _trunk(
        x_prompt.reshape(seq, d), conv_zero, seq, None, bias_tables, LEFT_ROWS,
        w_bf16, conv_w, ln_g, ln_b, emit=False)

    kv_prompt = (batch, LEFT_ROWS, N_HEADS, HEAD_DIM)
    kv_sample = (dec_batch, dec_seq, N_HEADS, HEAD_DIM)
    return (y_p.reshape(batch, seq, d),
            y_s.reshape(dec_batch, dec_seq, d),
            conv_p.reshape(1, batch, CONV_W - 1, d),
            conv_s.reshape(1, dec_batch, CONV_W - 1, d),
            k_p.reshape(kv_prompt),
            v_p.reshape(kv_prompt),
            k_s.reshape(kv_sample),
            v_s.reshape(kv_sample))
```

```python
import functools

import jax
import jax.numpy as jnp
from jax import lax
from jax.experimental import pallas as pl
from jax.experimental.pallas import tpu as pltpu

D_MODEL = 2048
DEPTH = 2
CHUNK = 64
N_LEFT_CHUNKS = 8
LEFT_ROWS = N_LEFT_CHUNKS * CHUNK
BAND = LEFT_ROWS + CHUNK
N_HEADS = 16
HEAD_DIM = D_MODEL // N_HEADS
MAX_REL = 256
N_REL = 2 * MAX_REL + 1
CONV_W = 3
D_FF = 5632
ALPHA = (2.0 * DEPTH) ** 0.25
LN_EPS = 1e-5
SCALE = HEAD_DIM ** -0.5
LOG2E = 1.4426950408889634
SCALE_LOG2 = SCALE * LOG2E

LANES = 128
VMEM_LIMIT_BYTES = 56 * 1024 * 1024

PAIR = 2 * CHUNK
PAIR_BAND = BAND + CHUNK
N_REL_PAD = 640
T_PAD = 768

BF16 = jnp.bfloat16
F32 = jnp.float32


def _params(n_axes):
    return pltpu.CompilerParams(dimension_semantics=("arbitrary",) * n_axes,
                                vmem_limit_bytes=VMEM_LIMIT_BYTES)


def _dot(a, b):
    return jnp.dot(a, b, preferred_element_type=F32)


def _dot_t(a, b):
    return lax.dot_general(a, b, (((1,), (1,)), ((), ())), preferred_element_type=F32)


def _wspec(weight, blk, imap):
    arr, layer, first_col = weight
    first = first_col // blk[1]
    if arr.ndim == 3:
        return pl.BlockSpec((None,) + blk, lambda *g: (layer,) + _shift(imap(*g), first))
    return pl.BlockSpec(blk, lambda *g: _shift(imap(*g), first))


def _shift(idx, first):
    return (idx[0], idx[1] + first)


def _wload(w_ref, wb_ref):
    w = w_ref[...]
    if w.dtype != BF16:
        w = w.astype(BF16)
    if wb_ref is not None:
        wb_ref[...] = w
    return w


def _dot_side_by_side(x, w_refs, wb_refs):
    tn = w_refs[0].shape[1]
    w = jnp.concatenate([_wload(w_ref, wb_ref) for w_ref, wb_ref in zip(w_refs, wb_refs)], axis=1)
    y = _dot(x, w)
    return [y[:, t * tn:(t + 1) * tn] for t in range(len(w_refs))]


def _wb_outputs(weights, emit, blk, imap, shape):
    if not emit:
        return [], []
    return ([pl.BlockSpec(blk, imap)] * len(weights),
            [jax.ShapeDtypeStruct(shape, BF16)] * len(weights))


def _as_weights(arrs):
    return tuple((a, 0, 0) for a in arrs)


def _conv_gate_kernel(x_ref, wb_ref, wc_ref, wh_ref, hist_ref, cw_ref, g_ref, state_ref, *rest,
                      nseq, emit, carried):
    rest = list(rest)
    wbo = [rest.pop(0) for _ in range(3)] if emit else [None] * 3
    xb_ref = rest.pop(0)
    i = pl.program_id(0)
    j = pl.program_id(1)
    tm, tn = g_ref.shape
    ls = tm // nseq

    @pl.when(j == 0)
    def _():
        xb_ref[...] = x_ref[...].astype(BF16)

    b, c, h = _dot_side_by_side(xb_ref[...], (wb_ref, wc_ref, wh_ref), wbo)
    u = c * h

    if carried:
        carry_ref = rest.pop(0)

        @pl.when(i == 0)
        def _():
            carry_ref[j] = hist_ref[0]

        prev = carry_ref[j][None]
    else:
        prev = hist_ref[...]

    shape3 = (nseq, ls, tn)
    pos = lax.broadcasted_iota(jnp.int32, shape3, 1)
    u3 = u.reshape(shape3)
    p1 = pltpu.roll(u, 1, 0).reshape(shape3)
    p2 = pltpu.roll(u, 2, 0).reshape(shape3)
    h0 = prev[:, 0:1, :]
    h1 = prev[:, 1:2, :]
    p1 = jnp.where(pos == 0, h1, p1)
    p2 = jnp.where(pos == 0, h0, jnp.where(pos == 1, h1, p2))
    conv = cw_ref[0:1, :] * p2 + cw_ref[1:2, :] * p1 + cw_ref[2:3, :] * u3
    g_ref[...] = (b * conv.reshape(tm, tn)).astype(BF16)

    new_state = u3[:, ls - 2:ls, :]
    state_ref[...] = new_state
    if carried:
        carry_ref[j] = new_state[0]


def _conv_gate(x, weights, hist, conv_w, *, seq_len, tm, tn, emit):
    m, d = x.shape
    nj = d // tn
    carried = seq_len > tm
    nseq = 1 if carried else tm // seq_len
    assert not emit or m == tm
    hist_map = (lambda i, j: (0, 0, j)) if carried else (lambda i, j: (i, 0, j))
    scratch = [pltpu.VMEM((tm, d), BF16)]
    if carried:
        scratch.append(pltpu.VMEM((nj, CONV_W - 1, tn), F32))
    wmap = lambda i, j: (0, j)
    wb_specs, wb_shapes = _wb_outputs(weights, emit, (d, tn), wmap, (d, d))
    n_states = (m // tm) * nseq
    g, states, *wb = pl.pallas_call(
        functools.partial(_conv_gate_kernel, nseq=nseq, emit=emit, carried=carried),
        grid=(m // tm, nj),
        in_specs=[pl.BlockSpec((tm, d), lambda i, j: (i, 0))]
        + [_wspec(w, (d, tn), wmap) for w in weights]
        + [pl.BlockSpec((nseq, CONV_W - 1, tn), hist_map),
           pl.BlockSpec((None, CONV_W, tn), lambda i, j: (0, 0, j))],
        out_specs=[
            pl.BlockSpec((tm, tn), lambda i, j: (i, j)),
            pl.BlockSpec((nseq, CONV_W - 1, tn), lambda i, j: (i, 0, j)),
        ] + wb_specs,
        out_shape=[
            jax.ShapeDtypeStruct((m, d), BF16),
            jax.ShapeDtypeStruct((n_states, CONV_W - 1, d), F32),
        ] + wb_shapes,
        scratch_shapes=scratch,
        compiler_params=_params(2),
        name="conv_gate",
    )(x, *[w[0] for w in weights], hist, conv_w)
    return g, states[-hist.shape[0]:], _as_weights(wb)


LN_ROWS = 64


def _layer_norm_rows(y, gain, bias, out_ref, outb_ref, out_scale=1.0):
    for r in range(0, y.shape[0], LN_ROWS):
        rows = slice(r, r + LN_ROWS)
        yr = y[rows, :]
        mu = jnp.mean(yr, axis=-1, keepdims=True)
        yc = yr - mu
        var = jnp.mean(yc * yc, axis=-1, keepdims=True)
        z = yc * lax.rsqrt(var + LN_EPS) * gain + bias
        out_ref[rows, :] = z if out_scale == 1.0 else z * out_scale
        if outb_ref is not None:
            outb_ref[rows, :] = z.astype(BF16)


def _mm_res_ln_kernel(a_ref, w_ref, x_ref, g_ref, b_ref, out_ref, outb_ref, *wbo, prescaled):
    k = pl.program_id(1)

    @pl.when(k == 0)
    def _():
        out_ref[...] = x_ref[...] if prescaled else ALPHA * x_ref[...]

    if len(a_ref.shape) == 3:
        a = jnp.concatenate([a_ref[h] for h in range(a_ref.shape[0])], axis=1)
    else:
        a = a_ref[...]
    out_ref[...] += _dot(a, _wload(w_ref, wbo[0] if wbo else None))

    @pl.when(k == pl.num_programs(1) - 1)
    def _():
        _layer_norm_rows(out_ref, g_ref[...], b_ref[...], out_ref, outb_ref, out_scale=ALPHA)


def _mm_res_ln(a, weight, x, ln_g, ln_b, ln_idx, *, tm, tk, emit, prescaled):
    m, d = x.shape
    kdim = weight[0].shape[-2]
    assert not emit or m == tm
    if a.ndim == 3:
        a_spec = pl.BlockSpec((tk // HEAD_DIM, tm, HEAD_DIM), lambda i, k: (k, i, 0))
    else:
        a_spec = pl.BlockSpec((tm, tk), lambda i, k: (i, k))
    wmap = lambda i, k: (k, 0)
    wb_specs, wb_shapes = _wb_outputs((weight,), emit, (tk, d), wmap, (kdim, d))
    layer, sub = ln_idx
    ln_spec = pl.BlockSpec((None, None, 1, d), lambda i, k: (layer, sub, 0, 0))
    x_new, xb_new, *wb = pl.pallas_call(
        functools.partial(_mm_res_ln_kernel, prescaled=prescaled),
        grid=(m // tm, kdim // tk),
        in_specs=[
            a_spec,
            _wspec(weight, (tk, d), wmap),
            pl.BlockSpec((tm, d), lambda i, k: (i, 0)),
            ln_spec,
            ln_spec,
        ],
        out_specs=[
            pl.BlockSpec((tm, d), lambda i, k: (i, 0)),
            pl.BlockSpec((tm, d), lambda i, k: (i, 0)),
        ] + wb_specs,
        out_shape=[
            jax.ShapeDtypeStruct((m, d), F32),
            jax.ShapeDtypeStruct((m, d), BF16),
        ] + wb_shapes,
        compiler_params=_params(2),
        name="mm_res_ln",
    )(a, weight[0], x, ln_g.reshape(DEPTH, 2, 1, d), ln_b.reshape(DEPTH, 2, 1, d))
    return x_new, xb_new, _as_weights(wb)


def _proj_res_ln_kernel(a_ref, w_ref, x_ref, g_ref, b_ref, out_ref, outb_ref, *, prescaled):
    if len(a_ref.shape) == 3:
        a = jnp.concatenate([a_ref[h] for h in range(a_ref.shape[0])], axis=1)
    else:
        a = a_ref[...]
    residual = x_ref[...] if prescaled else ALPHA * x_ref[...]
    y = residual + _dot(a, w_ref[...])
    _layer_norm_rows(y, g_ref[...], b_ref[...], out_ref, outb_ref, out_scale=ALPHA)


def _proj_res_ln(a, weight, x, ln_g, ln_b, ln_idx, *, tm, prescaled):
    m, d = x.shape
    w = weight[0]
    assert w.shape == (d, d) and w.dtype == BF16
    if a.ndim == 3:
        a_spec = pl.BlockSpec((N_HEADS, tm, HEAD_DIM), lambda i: (0, i, 0))
    else:
        a_spec = pl.BlockSpec((tm, d), lambda i: (i, 0))
    layer, sub = ln_idx
    ln_spec = pl.BlockSpec((None, None, 1, d), lambda i: (layer, sub, 0, 0))
    return pl.pallas_call(
        functools.partial(_proj_res_ln_kernel, prescaled=prescaled),
        grid=(m // tm,),
        in_specs=[
            a_spec,
            pl.BlockSpec((d, d), lambda i: (0, 0)),
            pl.BlockSpec((tm, d), lambda i: (i, 0)),
            ln_spec,
            ln_spec,
        ],
        out_specs=[
            pl.BlockSpec((tm, d), lambda i: (i, 0)),
            pl.BlockSpec((tm, d), lambda i: (i, 0)),
        ],
        out_shape=[
            jax.ShapeDtypeStruct((m, d), F32),
            jax.ShapeDtypeStruct((m, d), BF16),
        ],
        compiler_params=_params(1),
        name="proj_res_ln",
    )(a, w, x, ln_g.reshape(DEPTH, 2, 1, d), ln_b.reshape(DEPTH, 2, 1, d))


SWAP_STEP = 2


def _ffn_kernel(xs_hbm, xb_ref, wg_ref, wu_ref, wd_ref, g_ref, b_ref, *rest,
                n_tiles, out_scale, want_bf16, emit):
    rest = list(rest)
    y_hbm = rest.pop(0)
    yb_hbm = rest.pop(0) if want_bf16 else None
    wbo = [rest.pop(0) for _ in range(3)] if emit else [None] * 3
    acc = rest.pop(0)
    ybbuf = rest.pop(0) if want_bf16 else None
    sem_x, sem_y = rest.pop(0), rest.pop(0)
    sem_yb = rest.pop(0) if want_bf16 else None
    i = pl.program_id(0)
    j = pl.program_id(1)
    last_j = pl.num_programs(1) - 1
    n_slots, tm = acc.shape[:2]
    tf = wg_ref.shape[1]
    slot = i % n_slots
    other = (i + 1) % n_slots

    def x_copy(tile, to_slot):
        return pltpu.make_async_copy(xs_hbm.at[pl.ds(tile * tm, tm), :], acc.at[to_slot],
                                     sem_x.at[to_slot])

    def y_copies(tile, from_slot):
        rows = pl.ds(tile * tm, tm)
        copies = [pltpu.make_async_copy(acc.at[from_slot], y_hbm.at[rows, :], sem_y.at[from_slot])]
        if want_bf16:
            copies.append(pltpu.make_async_copy(ybbuf, yb_hbm.at[rows, :], sem_yb))
        return copies

    @pl.when((i == 0) & (j == 0))
    def _():
        x_copy(0, 0).start()

    @pl.when(j == 0)
    def _():
        x_copy(i, slot).wait()

    w_gate_up = jnp.concatenate([_wload(wg_ref, wbo[0]), _wload(wu_ref, wbo[1])], axis=1)
    gu = _dot(xb_ref[...], w_gate_up)
    h = (jax.nn.silu(gu[:, :tf]) * gu[:, tf:]).astype(BF16)
    acc[slot] += _dot(h, _wload(wd_ref, wbo[2]))

    @pl.when((j == SWAP_STEP) & (i >= 1))
    def _():
        for cp in y_copies(i - 1, other):
            cp.wait()

    @pl.when((j == SWAP_STEP) & (i + 1 < n_tiles))
    def _():
        x_copy(i + 1, other).start()

    @pl.when(j == last_j)
    def _():
        _layer_norm_rows(acc.at[slot], g_ref[...], b_ref[...], acc.at[slot], ybbuf, out_scale)
        for cp in y_copies(i, slot):
            cp.start()

    @pl.when((j == last_j) & (i == n_tiles - 1))
    def _():
        for cp in y_copies(i, slot):
            cp.wait()


def _ffn_fused(xs, xb, w_gate_up, w_down, ln_g, ln_b, layer, *, tm, tf, out_scale, want_bf16,
               emit):
    m, d = xs.shape
    nf = D_FF // tf
    n_tiles = m // tm
    n_slots = min(2, n_tiles)
    assert nf > SWAP_STEP and (not emit or n_tiles == 1)
    ln_spec = pl.BlockSpec((None, None, 1, d), lambda i, j: (layer, 1, 0, 0))
    any_spec = pl.BlockSpec(memory_space=pl.ANY)
    n_out = 2 if want_bf16 else 1
    up_map = lambda i, j: (0, j)
    down_map = lambda i, j: (j, 0)
    up_specs, up_shapes = _wb_outputs(w_gate_up, emit, (d, tf), up_map, (d, D_FF))
    down_specs, down_shapes = _wb_outputs((w_down,), emit, (tf, d), down_map, (D_FF, d))
    res = pl.pallas_call(
        functools.partial(_ffn_kernel, n_tiles=n_tiles, out_scale=out_scale, want_bf16=want_bf16,
                          emit=emit),
        grid=(n_tiles, nf),
        in_specs=[any_spec, pl.BlockSpec((tm, d), lambda i, j: (i, 0))]
        + [_wspec(w, (d, tf), up_map) for w in w_gate_up]
        + [_wspec(w_down, (tf, d), down_map), ln_spec, ln_spec],
        out_specs=[any_spec] * n_out + up_specs + down_specs,
        out_shape=[jax.ShapeDtypeStruct((m, d), F32), jax.ShapeDtypeStruct((m, d), BF16)][:n_out]
        + up_shapes + down_shapes,
        scratch_shapes=[pltpu.VMEM((n_slots, tm, d), F32)]
        + ([pltpu.VMEM((tm, d), BF16)] if want_bf16 else [])
        + [pltpu.SemaphoreType.DMA((n_slots,)), pltpu.SemaphoreType.DMA((n_slots,))]
        + ([pltpu.SemaphoreType.DMA] if want_bf16 else []),
        compiler_params=_params(2),
        name="ffn",
    )(xs, xb, *[w[0] for w in w_gate_up], w_down[0],
      ln_g.reshape(DEPTH, 2, 1, d), ln_b.reshape(DEPTH, 2, 1, d))
    wb = _as_weights(res[n_out:])
    return res[0], (res[1] if want_bf16 else None), wb[:2], wb[2:]


def _qkv_kernel(xb_ref, wq_ref, wk_ref, wv_ref, q_ref, kb_ref, vb_ref, k_hbm, v_hbm, *rest,
                transposed_v, emit):
    rest = list(rest)
    wbo = [rest.pop(0) for _ in range(3)] if emit else [None] * 3
    kstage, vstage, sem = rest
    i = pl.program_id(0)
    j = pl.program_id(1)
    nj = pl.num_programs(1)
    hb = q_ref.shape[0]
    tm = xb_ref.shape[0]
    kv_rows = kstage.shape[1]

    q, k, v = _dot_side_by_side(xb_ref[...], (wq_ref, wk_ref, wv_ref), wbo)

    def kv_copies(step):
        slot = step % 2
        return [pltpu.make_async_copy(stage.at[slot, :, hh * HEAD_DIM:(hh + 1) * HEAD_DIM],
                                      out.at[:, step * hb + hh, :], sem.at[slot])
                for stage, out in ((kstage, k_hbm), (vstage, v_hbm)) for hh in range(hb)]

    @pl.when(i == pl.num_programs(0) - 1)
    def _():
        @pl.when(j >= 2)
        def _():
            for cp in kv_copies(j - 2):
                cp.wait()

        kstage[j % 2] = k[tm - kv_rows:, :]
        vstage[j % 2] = v[tm - kv_rows:, :]
        for cp in kv_copies(j):
            cp.start()

        @pl.when(j == nj - 1)
        def _():
            for cp in kv_copies(j - 1) + kv_copies(j):
                cp.wait()

    for hh in range(hb):
        cols = slice(hh * HEAD_DIM, (hh + 1) * HEAD_DIM)
        q_ref[hh] = (q[:, cols] * SCALE_LOG2).astype(BF16)
        kb_ref[hh] = k[:, cols].astype(BF16)
        if transposed_v:
            vb_ref[hh] = v[:, cols].T.astype(BF16)
        else:
            vb_ref[hh] = v[:, cols].astype(BF16)


def _qkv(xb, weights, *, tm, tn, kv_rows, transposed_v, emit):
    m, d = xb.shape
    nj = d // tn
    hb = tn // HEAD_DIM
    assert (not emit or m == tm) and kv_rows <= tm and nj >= 2
    kv_spec = pl.BlockSpec(memory_space=pl.ANY)
    kv_shape = jax.ShapeDtypeStruct((kv_rows, N_HEADS, HEAD_DIM), F32)
    hm_spec = pl.BlockSpec((hb, tm, HEAD_DIM), lambda i, j: (j, i, 0))
    hm_shape = jax.ShapeDtypeStruct((N_HEADS, m, HEAD_DIM), BF16)
    if transposed_v:
        v_spec = pl.BlockSpec((hb, HEAD_DIM, tm), lambda i, j: (j, 0, i))
        v_shape = jax.ShapeDtypeStruct((N_HEADS, HEAD_DIM, m), BF16)
    else:
        v_spec, v_shape = hm_spec, hm_shape
    wmap = lambda i, j: (0, j)
    wb_specs, wb_shapes = _wb_outputs(weights, emit, (d, tn), wmap, (d, d))
    q, kb, vb, k, v, *wb = pl.pallas_call(
        functools.partial(_qkv_kernel, transposed_v=transposed_v, emit=emit),
        grid=(m // tm, nj),
        in_specs=[pl.BlockSpec((tm, d), lambda i, j: (i, 0))]
        + [_wspec(w, (d, tn), wmap) for w in weights],
        out_specs=[hm_spec, hm_spec, v_spec, kv_spec, kv_spec] + wb_specs,
        out_shape=[hm_shape, hm_shape, v_shape, kv_shape, kv_shape] + wb_shapes,
        scratch_shapes=[pltpu.VMEM((2, kv_rows, tn), F32), pltpu.VMEM((2, kv_rows, tn), F32),
                        pltpu.SemaphoreType.DMA((2,))],
        compiler_params=_params(2),
        name="qkv",
    )(xb, *[w[0] for w in weights])
    return q, kb, vb, k, v, _as_weights(wb)


def _bias_kernel(rb_ref, heads_ref, pair_ref):
    rb = rb_ref[...]
    hi = rb.astype(BF16)
    r1 = rb - hi.astype(F32)
    mid = r1.astype(BF16)
    lo = (r1 - mid.astype(F32)).astype(BF16)
    src = lax.broadcasted_iota(jnp.int32, (N_REL_PAD, T_PAD), 0)
    m = lax.broadcasted_iota(jnp.int32, (N_REL_PAD, T_PAD), 1)
    idx = jnp.clip(LEFT_ROWS + CHUNK - 1 - m, -MAX_REL, MAX_REL) + MAX_REL
    onehot = (src == idx).astype(BF16)
    t = (_dot(hi, onehot) + _dot(mid, onehot)) + _dot(lo, onehot)

    key = lax.broadcasted_iota(jnp.int32, (PAIR_BAND, PAIR), 0)
    qry = lax.broadcasted_iota(jnp.int32, (PAIR_BAND, PAIR), 1)
    in_band = ((qry < CHUNK) & (key < BAND)) | ((qry >= CHUNK) & (key >= CHUNK))
    chunk_rows = []
    for h in range(N_HEADS):
        rows = jnp.broadcast_to(t[h:h + 1, :], (PAIR, T_PAD))
        rows = pltpu.roll(rows, T_PAD - (CHUNK - 1), 1, stride=1, stride_axis=0)
        chunk_rows.append(rows[:CHUNK, :PAIR_BAND])
        pair_ref[h] = jnp.where(in_band, rows[:, :PAIR_BAND].T * LOG2E, -jnp.inf)
    for hp in range(N_HEADS // 2):
        both = jnp.concatenate(chunk_rows[2 * hp:2 * hp + 2], axis=0)
        heads_ref[hp] = both.T[:BAND] * LOG2E


def _bias_tables(rel_bias):
    rb = jnp.pad(rel_bias, ((0, 0), (0, N_REL_PAD - N_REL)))
    return pl.pallas_call(
        _bias_kernel,
        out_shape=[jax.ShapeDtypeStruct((N_HEADS // 2, BAND, 2 * CHUNK), F32),
                   jax.ShapeDtypeStruct((N_HEADS, PAIR_BAND, PAIR), F32)],
        name="rel_bias_tables",
    )(rb)


PAIRS = LEFT_ROWS // PAIR


def _attn_prompt_kernel(q_ref, kp_ref, kc_ref, vp_ref, vc_ref, bias_ref, o_ref):
    g = pl.program_id(0)
    key = lax.broadcasted_iota(jnp.int32, (PAIR_BAND, PAIR), 0)

    def head(h, first):
        bias = bias_ref[h]
        for p in range(PAIRS):
            rows = slice(p * PAIR, (p + 1) * PAIR)
            lo = slice(p * PAIR, LEFT_ROWS)
            hi = slice(0, (p + 1) * PAIR)
            kband = jnp.concatenate([kp_ref[h, lo, :], kc_ref[h, hi, :]], axis=0)
            vband = jnp.concatenate([vp_ref[h, :, lo], vc_ref[h, :, hi]], axis=1)
            s = _dot_t(kband, q_ref[h, rows, :]) + bias
            if first:
                s = jnp.where(key >= LEFT_ROWS - p * PAIR, s, -jnp.inf)
            e = jnp.exp2(s - jnp.max(s, axis=0, keepdims=True))
            denom = jnp.sum(e, axis=0, keepdims=True)
            o = _dot(vband, e.astype(BF16)) * (1.0 / denom)
            o_ref[h, rows, :] = o.T.astype(BF16)

    def heads(first):
        def body(h, carry):
            head(h, first)
            return carry
        lax.fori_loop(0, N_HEADS, body, 0, unroll=8)

    pl.when(g == 0)(functools.partial(heads, True))
    pl.when(g > 0)(functools.partial(heads, False))


def _attn_prompt(q, kb, vt, bias):
    _, m, _ = q.shape
    blk = (N_HEADS, LEFT_ROWS, HEAD_DIM)
    cur = pl.BlockSpec(blk, lambda g: (0, g, 0))
    prev = pl.BlockSpec(blk, lambda g: (0, jnp.maximum(g - 1, 0), 0))
    blk_t = (N_HEADS, HEAD_DIM, LEFT_ROWS)
    cur_t = pl.BlockSpec(blk_t, lambda g: (0, 0, g))
    prev_t = pl.BlockSpec(blk_t, lambda g: (0, 0, jnp.maximum(g - 1, 0)))
    return pl.pallas_call(
        _attn_prompt_kernel,
        grid=(m // LEFT_ROWS,),
        in_specs=[cur, prev, cur, prev_t, cur_t,
                  pl.BlockSpec((N_HEADS, PAIR_BAND, PAIR), lambda g: (0, 0, 0))],
        out_specs=cur,
        out_shape=jax.ShapeDtypeStruct(q.shape, BF16),
        compiler_params=_params(1),
        name="attn_prompt",
    )(q, kb, kb, vt, vt, bias)


def _cache_copies(ck_hbm, cv_hbm, kbuf, vbuf, sem, stream, slot):
    return [pltpu.make_async_copy(src.at[stream, :, h, :],
                                  buf.at[slot, :, h * HEAD_DIM:(h + 1) * HEAD_DIM],
                                  sem.at[slot])
            for src, buf in ((ck_hbm, kbuf), (cv_hbm, vbuf)) for h in range(N_HEADS)]


def _attn_sample_kernel(q_ref, kc_ref, vc_ref, ck_hbm, cv_hbm, bias_ref, o_ref, kbuf, vbuf, sem):
    n = pl.program_id(0)
    slot = n % 2
    copies = functools.partial(_cache_copies, ck_hbm, cv_hbm, kbuf, vbuf, sem)

    @pl.when(n == 0)
    def _():
        for cp in copies(0, 0):
            cp.start()

    @pl.when(n + 1 < pl.num_programs(0))
    def _():
        for cp in copies(n + 1, 1 - slot):
            cp.start()

    for cp in copies(n, slot):
        cp.wait()

    zeros = jnp.zeros((CHUNK, HEAD_DIM), BF16)
    for hp in range(N_HEADS // 2):
        h0, h1 = 2 * hp, 2 * hp + 1
        cols = slice(h0 * HEAD_DIM, (h1 + 1) * HEAD_DIM)
        knew = jnp.concatenate([kc_ref[h0], kc_ref[h1]], axis=1)
        vnew = jnp.concatenate([vc_ref[h0], vc_ref[h1]], axis=1)
        kband = jnp.concatenate([kbuf[slot, :, cols].astype(BF16), knew], axis=0)
        vband = jnp.concatenate([vbuf[slot, :, cols].astype(BF16), vnew], axis=0)
        qbd = jnp.concatenate([jnp.concatenate([q_ref[h0], zeros], axis=1),
                               jnp.concatenate([zeros, q_ref[h1]], axis=1)], axis=0)
        s = _dot_t(kband, qbd) + bias_ref[hp]
        e = jnp.exp2(s - jnp.max(s, axis=0, keepdims=True))
        denom = jnp.sum(e, axis=0, keepdims=True)
        o = lax.dot_general(vband, e.astype(BF16), (((0,), (0,)), ((), ())),
                            preferred_element_type=F32) * (1.0 / denom)
        o_ref[h0] = o[:HEAD_DIM].T[:CHUNK].astype(BF16)
        o_ref[h1] = o[HEAD_DIM:].T[CHUNK:].astype(BF16)


def _attn_sample(q, kb, vb, cache_k, cache_v, bias):
    n = cache_k.shape[0]
    new = pl.BlockSpec((N_HEADS, CHUNK, HEAD_DIM), lambda s: (0, s, 0))
    cache = pl.BlockSpec(memory_space=pl.ANY)
    return pl.pallas_call(
        _attn_sample_kernel,
        grid=(n,),
        in_specs=[new, new, new, cache, cache,
                  pl.BlockSpec((N_HEADS // 2, BAND, 2 * CHUNK), lambda s: (0, 0, 0))],
        out_specs=new,
        out_shape=jax.ShapeDtypeStruct(q.shape, BF16),
        scratch_shapes=[pltpu.VMEM((2, LEFT_ROWS, D_MODEL), F32),
                        pltpu.VMEM((2, LEFT_ROWS, D_MODEL), F32),
                        pltpu.SemaphoreType.DMA((2,))],
        compiler_params=_params(1),
        name="attn_sample",
    )(q, kb, vb, cache_k, cache_v, bias)


TM = 1024
TK_OUT = 512
TM_PROJ = 512


def _tn_three(emit):
    return 256 if emit else 512


def _tf_fused(emit):
    return 256 if emit else 512


def _f32_weights(w_in_a, w_out_a, w_kv, w_q, w_o, w_gate_up, w_down):
    return dict(
        w_in=tuple((w_in_a, 0, t * D_MODEL) for t in range(3)),
        w_out=(w_out_a, 0, 0),
        w_qkv=((w_q, 0, 0), (w_kv, 0, 0), (w_kv, 0, D_MODEL)),
        w_o=(w_o, 0, 0),
        w_gate_up=tuple(tuple((w_gate_up, l, t * D_FF) for t in range(2)) for l in range(DEPTH)),
        w_down=tuple((w_down, l, 0) for l in range(DEPTH)),
    )


def _trunk(x, hist, seq_len, cache, bias_tables, kv_rows, w, conv_w, ln_g, ln_b, *, emit):
    wb = dict(w_gate_up=[None] * DEPTH, w_down=[None] * DEPTH)

    def ffn(x, xb, layer):
        more = layer + 1 < DEPTH
        x, xb, wb["w_gate_up"][layer], down = _ffn_fused(
            x, xb, w["w_gate_up"][layer], w["w_down"][layer], ln_g, ln_b, layer, tm=TM,
            tf=_tf_fused(emit), out_scale=ALPHA if more else 1.0, want_bf16=more, emit=emit)
        wb["w_down"][layer] = down[0] if emit else None
        return x, xb

    def proj(a, name, ln_idx, prescaled):
        if emit:
            x_new, xb_new, (wb[name],) = _mm_res_ln(a, w[name], x, ln_g, ln_b, ln_idx, tm=TM,
                                                    tk=TK_OUT, emit=True, prescaled=prescaled)
            return x_new, xb_new
        return _proj_res_ln(a, w[name], x, ln_g, ln_b, ln_idx, tm=TM_PROJ, prescaled=prescaled)

    g, conv_state, wb["w_in"] = _conv_gate(x, w["w_in"], hist, conv_w, seq_len=seq_len,
                                           tm=TM, tn=_tn_three(emit), emit=emit)
    x, xb = proj(g, "w_out", (0, 0), prescaled=False)
    x, xb = ffn(x, xb, 0)

    bias_heads, bias_pair = bias_tables
    q, kb, vb, k, v, wb["w_qkv"] = _qkv(xb, w["w_qkv"], tm=TM, tn=_tn_three(emit),
                                        kv_rows=kv_rows, transposed_v=cache is None, emit=emit)
    if cache is None:
        att = _attn_prompt(q, kb, vb, bias_pair)
    else:
        att = _attn_sample(q, kb, vb, cache[0], cache[1], bias_heads)
    x, xb = proj(att, "w_o", (1, 0), prescaled=True)
    x, _ = ffn(x, xb, 1)
    return (x, conv_state, k, v), wb


def kernel(x_prompt, x_sample, state_conv, cache_k, cache_v, w_in_a, conv_w, w_out_a, w_kv, w_q,
           w_o, rel_bias, ln_g, ln_b, w_gate_up, w_down):
    batch, seq, d = x_prompt.shape
    dec_batch, dec_seq, _ = x_sample.shape
    assert batch == 1 and DEPTH == 2 and dec_seq == CHUNK and cache_k.shape[1] == LEFT_ROWS
    assert dec_batch * dec_seq == TM
    bias_tables = _bias_tables(rel_bias[0])
    w_f32 = _f32_weights(w_in_a, w_out_a, w_kv, w_q, w_o, w_gate_up, w_down)

    (y_s, conv_s, k_s, v_s), w_bf16 = _trunk(
        x_sample.reshape(dec_batch * dec_seq, d), state_conv[0], dec_seq, (cache_k, cache_v),
        bias_tables, dec_batch * dec_seq, w_f32, conv_w, ln_g, ln_b, emit=True)
    conv_zero = jnp.zeros((batch, CONV_W - 1, d), x_prompt.dtype)
    (y_p, conv_p, k_p, v_p), _ = _trunk(
        x_prompt.reshape(seq, d), conv_zero, seq, None, bias_tables, LEFT_ROWS,
        w_bf16, conv_w, ln_g, ln_b, emit=False)

    kv_prompt = (batch, LEFT_ROWS, N_HEADS, HEAD_DIM)
    kv_sample = (dec_batch, dec_seq, N_HEADS, HEAD_DIM)
    return (y_p.reshape(batch, seq, d),
            y_s.reshape(dec_batch, dec_seq, d),
            conv_p.reshape(1, batch, CONV_W - 1, d),
            conv_s.reshape(1, dec_batch, CONV_W - 1, d),
            k_p.reshape(kv_prompt),
            v_p.reshape(kv_prompt),
            k_s.reshape(kv_sample),
            v_s.reshape(kv_sample))
```

```python
import functools

import jax
import jax.numpy as jnp
from jax import lax
from jax.experimental import pallas as pl
from jax.experimental.pallas import tpu as pltpu

D_MODEL = 2048
DEPTH = 2
CHUNK = 64
N_LEFT_CHUNKS = 8
LEFT_ROWS = N_LEFT_CHUNKS * CHUNK
BAND = LEFT_ROWS + CHUNK
N_HEADS = 16
HEAD_DIM = D_MODEL // N_HEADS
MAX_REL = 256
N_REL = 2 * MAX_REL + 1
CONV_W = 3
D_FF = 5632
ALPHA = (2.0 * DEPTH) ** 0.25
LN_EPS = 1e-5
SCALE = HEAD_DIM ** -0.5
LOG2E = 1.4426950408889634
SCALE_LOG2 = SCALE * LOG2E

LANES = 128
VMEM_LIMIT_BYTES = 56 * 1024 * 1024

PAIR = 2 * CHUNK
PAIR_BAND = BAND + CHUNK
N_REL_PAD = 640
T_PAD = 768

BF16 = jnp.bfloat16
F32 = jnp.float32


def _params(n_axes):
    return pltpu.CompilerParams(dimension_semantics=("arbitrary",) * n_axes,
                                vmem_limit_bytes=VMEM_LIMIT_BYTES)


def _dot(a, b):
    return jnp.dot(a, b, preferred_element_type=F32)


def _dot_t(a, b):
    return lax.dot_general(a, b, (((1,), (1,)), ((), ())), preferred_element_type=F32)


def _wspec(weight, blk, imap):
    arr, layer, first_col = weight
    first = first_col // blk[1]
    if arr.ndim == 3:
        return pl.BlockSpec((None,) + blk, lambda *g: (layer,) + _shift(imap(*g), first))
    return pl.BlockSpec(blk, lambda *g: _shift(imap(*g), first))


def _shift(idx, first):
    return (idx[0], idx[1] + first)


def _wload(w_ref, wb_ref):
    w = w_ref[...]
    if w.dtype != BF16:
        w = w.astype(BF16)
    if wb_ref is not None:
        wb_ref[...] = w
    return w


def _dot_side_by_side(x, w_refs, wb_refs):
    tn = w_refs[0].shape[1]
    w = jnp.concatenate([_wload(w_ref, wb_ref) for w_ref, wb_ref in zip(w_refs, wb_refs)], axis=1)
    y = _dot(x, w)
    return [y[:, t * tn:(t + 1) * tn] for t in range(len(w_refs))]


def _wb_outputs(weights, emit, blk, imap, shape):
    if not emit:
        return [], []
    return ([pl.BlockSpec(blk, imap)] * len(weights),
            [jax.ShapeDtypeStruct(shape, BF16)] * len(weights))


def _as_weights(arrs):
    return tuple((a, 0, 0) for a in arrs)


def _conv_gate_kernel(x_ref, wb_ref, wc_ref, wh_ref, hist_ref, cw_ref, g_ref, state_ref, *rest,
                      nseq, emit, carried):
    rest = list(rest)
    wbo = [rest.pop(0) for _ in range(3)] if emit else [None] * 3
    xb_ref = rest.pop(0)
    i = pl.program_id(0)
    j = pl.program_id(1)
    tm, tn = g_ref.shape
    ls = tm // nseq

    @pl.when(j == 0)
    def _():
        xb_ref[...] = x_ref[...].astype(BF16)

    b, c, h = _dot_side_by_side(xb_ref[...], (wb_ref, wc_ref, wh_ref), wbo)
    u = c * h

    if carried:
        carry_ref = rest.pop(0)

        @pl.when(i == 0)
        def _():
            carry_ref[j] = hist_ref[0]

        prev = carry_ref[j][None]
    else:
        prev = hist_ref[...]

    shape3 = (nseq, ls, tn)
    pos = lax.broadcasted_iota(jnp.int32, shape3, 1)
    u3 = u.reshape(shape3)
    p1 = pltpu.roll(u, 1, 0).reshape(shape3)
    p2 = pltpu.roll(u, 2, 0).reshape(shape3)
    h0 = prev[:, 0:1, :]
    h1 = prev[:, 1:2, :]
    p1 = jnp.where(pos == 0, h1, p1)
    p2 = jnp.where(pos == 0, h0, jnp.where(pos == 1, h1, p2))
    conv = cw_ref[0:1, :] * p2 + cw_ref[1:2, :] * p1 + cw_ref[2:3, :] * u3
    g_ref[...] = (b * conv.reshape(tm, tn)).astype(BF16)

    new_state = u3[:, ls - 2:ls, :]
    state_ref[...] = new_state
    if carried:
        carry_ref[j] = new_state[0]


def _conv_gate(x, weights, hist, conv_w, *, seq_len, tm, tn, emit):
    m, d = x.shape
    nj = d // tn
    carried = seq_len > tm
    nseq = 1 if carried else tm // seq_len
    assert not emit or m == tm
    hist_map = (lambda i, j: (0, 0, j)) if carried else (lambda i, j: (i, 0, j))
    scratch = [pltpu.VMEM((tm, d), BF16)]
    if carried:
        scratch.append(pltpu.VMEM((nj, CONV_W - 1, tn), F32))
    wmap = lambda i, j: (0, j)
    wb_specs, wb_shapes = _wb_outputs(weights, emit, (d, tn), wmap, (d, d))
    n_states = (m // tm) * nseq
    g, states, *wb = pl.pallas_call(
        functools.partial(_conv_gate_kernel, nseq=nseq, emit=emit, carried=carried),
        grid=(m // tm, nj),
        in_specs=[pl.BlockSpec((tm, d), lambda i, j: (i, 0))]
        + [_wspec(w, (d, tn), wmap) for w in weights]
        + [pl.BlockSpec((nseq, CONV_W - 1, tn), hist_map),
           pl.BlockSpec((None, CONV_W, tn), lambda i, j: (0, 0, j))],
        out_specs=[
            pl.BlockSpec((tm, tn), lambda i, j: (i, j)),
            pl.BlockSpec((nseq, CONV_W - 1, tn), lambda i, j: (i, 0, j)),
        ] + wb_specs,
        out_shape=[
            jax.ShapeDtypeStruct((m, d), BF16),
            jax.ShapeDtypeStruct((n_states, CONV_W - 1, d), F32),
        ] + wb_shapes,
        scratch_shapes=scratch,
        compiler_params=_params(2),
        name="conv_gate",
    )(x, *[w[0] for w in weights], hist, conv_w)
    return g, states[-hist.shape[0]:], _as_weights(wb)


LN_ROWS = 64


def _layer_norm_rows(y, gain, bias, out_ref, outb_ref, out_scale=1.0):
    for r in range(0, y.shape[0], LN_ROWS):
        rows = slice(r, r + LN_ROWS)
        yr = y[rows, :]
        mu = jnp.mean(yr, axis=-1, keepdims=True)
        yc = yr - mu
        var = jnp.mean(yc * yc, axis=-1, keepdims=True)
        z = yc * lax.rsqrt(var + LN_EPS) * gain + bias
        out_ref[rows, :] = z if out_scale == 1.0 else z * out_scale
        if outb_ref is not None:
            outb_ref[rows, :] = z.astype(BF16)


def _mm_res_ln_kernel(a_ref, w_ref, x_ref, g_ref, b_ref, out_ref, outb_ref, *wbo, prescaled):
    k = pl.program_id(1)

    @pl.when(k == 0)
    def _():
        out_ref[...] = x_ref[...] if prescaled else ALPHA * x_ref[...]

    if len(a_ref.shape) == 3:
        a = jnp.concatenate([a_ref[h] for h in range(a_ref.shape[0])], axis=1)
    else:
        a = a_ref[...]
    out_ref[...] += _dot(a, _wload(w_ref, wbo[0] if wbo else None))

    @pl.when(k == pl.num_programs(1) - 1)
    def _():
        _layer_norm_rows(out_ref, g_ref[...], b_ref[...], out_ref, outb_ref, out_scale=ALPHA)


def _mm_res_ln(a, weight, x, ln_g, ln_b, ln_idx, *, tm, tk, emit, prescaled):
    m, d = x.shape
    kdim = weight[0].shape[-2]
    assert not emit or m == tm
    if a.ndim == 3:
        a_spec = pl.BlockSpec((tk // HEAD_DIM, tm, HEAD_DIM), lambda i, k: (k, i, 0))
    else:
        a_spec = pl.BlockSpec((tm, tk), lambda i, k: (i, k))
    wmap = lambda i, k: (k, 0)
    wb_specs, wb_shapes = _wb_outputs((weight,), emit, (tk, d), wmap, (kdim, d))
    layer, sub = ln_idx
    ln_spec = pl.BlockSpec((None, None, 1, d), lambda i, k: (layer, sub, 0, 0))
    x_new, xb_new, *wb = pl.pallas_call(
        functools.partial(_mm_res_ln_kernel, prescaled=prescaled),
        grid=(m // tm, kdim // tk),
        in_specs=[
            a_spec,
            _wspec(weight, (tk, d), wmap),
            pl.BlockSpec((tm, d), lambda i, k: (i, 0)),
            ln_spec,
            ln_spec,
        ],
        out_specs=[
            pl.BlockSpec((tm, d), lambda i, k: (i, 0)),
            pl.BlockSpec((tm, d), lambda i, k: (i, 0)),
        ] + wb_specs,
        out_shape=[
            jax.ShapeDtypeStruct((m, d), F32),
            jax.ShapeDtypeStruct((m, d), BF16),
        ] + wb_shapes,
        compiler_params=_params(2),
        name="mm_res_ln",
    )(a, weight[0], x, ln_g.reshape(DEPTH, 2, 1, d), ln_b.reshape(DEPTH, 2, 1, d))
    return x_new, xb_new, _as_weights(wb)


def _proj_res_ln_kernel(a_ref, w_ref, x_ref, g_ref, b_ref, out_ref, outb_ref, *, prescaled):
    if len(a_ref.shape) == 3:
        a = jnp.concatenate([a_ref[h] for h in range(a_ref.shape[0])], axis=1)
    else:
        a = a_ref[...]
    residual = x_ref[...] if prescaled else ALPHA * x_ref[...]
    y = residual + _dot(a, w_ref[...])
    _layer_norm_rows(y, g_ref[...], b_ref[...], out_ref, outb_ref, out_scale=ALPHA)


def _proj_res_ln(a, weight, x, ln_g, ln_b, ln_idx, *, tm, prescaled):
    m, d = x.shape
    w = weight[0]
    assert w.shape == (d, d) and w.dtype == BF16
    if a.ndim == 3:
        a_spec = pl.BlockSpec((N_HEADS, tm, HEAD_DIM), lambda i: (0, i, 0))
    else:
        a_spec = pl.BlockSpec((tm, d), lambda i: (i, 0))
    layer, sub = ln_idx
    ln_spec = pl.BlockSpec((None, None, 1, d), lambda i: (layer, sub, 0, 0))
    return pl.pallas_call(
        functools.partial(_proj_res_ln_kernel, prescaled=prescaled),
        grid=(m // tm,),
        in_specs=[
            a_spec,
            pl.BlockSpec((d, d), lambda i: (0, 0)),
            pl.BlockSpec((tm, d), lambda i: (i, 0)),
            ln_spec,
            ln_spec,
        ],
        out_specs=[
            pl.BlockSpec((tm, d), lambda i: (i, 0)),
            pl.BlockSpec((tm, d), lambda i: (i, 0)),
        ],
        out_shape=[
            jax.ShapeDtypeStruct((m, d), F32),
            jax.ShapeDtypeStruct((m, d), BF16),
        ],
        compiler_params=_params(1),
        name="proj_res_ln",
    )(a, w, x, ln_g.reshape(DEPTH, 2, 1, d), ln_b.reshape(DEPTH, 2, 1, d))


SWAP_STEP = 2


def _ffn_kernel(xs_hbm, xb_ref, wg_ref, wu_ref, wd_ref, g_ref, b_ref, *rest,
                n_tiles, out_scale, want_bf16, emit):
    rest = list(rest)
    y_hbm = rest.pop(0)
    yb_hbm = rest.pop(0) if want_bf16 else None
    wbo = [rest.pop(0) for _ in range(3)] if emit else [None] * 3
    acc = rest.pop(0)
    ybbuf = rest.pop(0) if want_bf16 else None
    sem_x, sem_y = rest.pop(0), rest.pop(0)
    sem_yb = rest.pop(0) if want_bf16 else None
    i = pl.program_id(0)
    j = pl.program_id(1)
    last_j = pl.num_programs(1) - 1
    n_slots, tm = acc.shape[:2]
    tf = wg_ref.shape[1]
    slot = i % n_slots
    other = (i + 1) % n_slots

    def x_copy(tile, to_slot):
        return pltpu.make_async_copy(xs_hbm.at[pl.ds(tile * tm, tm), :], acc.at[to_slot],
                                     sem_x.at[to_slot])

    def y_copies(tile, from_slot):
        rows = pl.ds(tile * tm, tm)
        copies = [pltpu.make_async_copy(acc.at[from_slot], y_hbm.at[rows, :], sem_y.at[from_slot])]
        if want_bf16:
            copies.append(pltpu.make_async_copy(ybbuf, yb_hbm.at[rows, :], sem_yb))
        return copies

    @pl.when((i == 0) & (j == 0))
    def _():
        x_copy(0, 0).start()

    @pl.when(j == 0)
    def _():
        x_copy(i, slot).wait()

    w_gate_up = jnp.concatenate([_wload(wg_ref, wbo[0]), _wload(wu_ref, wbo[1])], axis=1)
    gu = _dot(xb_ref[...], w_gate_up)
    h = (jax.nn.silu(gu[:, :tf]) * gu[:, tf:]).astype(BF16)
    acc[slot] += _dot(h, _wload(wd_ref, wbo[2]))

    @pl.when((j == SWAP_STEP) & (i >= 1))
    def _():
        for cp in y_copies(i - 1, other):
            cp.wait()

    @pl.when((j == SWAP_STEP) & (i + 1 < n_tiles))
    def _():
        x_copy(i + 1, other).start()

    @pl.when(j == last_j)
    def _():
        _layer_norm_rows(acc.at[slot], g_ref[...], b_ref[...], acc.at[slot], ybbuf, out_scale)
        for cp in y_copies(i, slot):
            cp.start()

    @pl.when((j == last_j) & (i == n_tiles - 1))
    def _():
        for cp in y_copies(i, slot):
            cp.wait()


def _ffn_fused(xs, xb, w_gate_up, w_down, ln_g, ln_b, layer, *, tm, tf, out_scale, want_bf16,
               emit):
    m, d = xs.shape
    nf = D_FF // tf
    n_tiles = m // tm
    n_slots = min(2, n_tiles)
    assert nf > SWAP_STEP and (not emit or n_tiles == 1)
    ln_spec = pl.BlockSpec((None, None, 1, d), lambda i, j: (layer, 1, 0, 0))
    any_spec = pl.BlockSpec(memory_space=pl.ANY)
    n_out = 2 if want_bf16 else 1
    up_map = lambda i, j: (0, j)
    down_map = lambda i, j: (j, 0)
    up_specs, up_shapes = _wb_outputs(w_gate_up, emit, (d, tf), up_map, (d, D_FF))
    down_specs, down_shapes = _wb_outputs((w_down,), emit, (tf, d), down_map, (D_FF, d))
    res = pl.pallas_call(
        functools.partial(_ffn_kernel, n_tiles=n_tiles, out_scale=out_scale, want_bf16=want_bf16,
                          emit=emit),
        grid=(n_tiles, nf),
        in_specs=[any_spec, pl.BlockSpec((tm, d), lambda i, j: (i, 0))]
        + [_wspec(w, (d, tf), up_map) for w in w_gate_up]
        + [_wspec(w_down, (tf, d), down_map), ln_spec, ln_spec],
        out_specs=[any_spec] * n_out + up_specs + down_specs,
        out_shape=[jax.ShapeDtypeStruct((m, d), F32), jax.ShapeDtypeStruct((m, d), BF16)][:n_out]
        + up_shapes + down_shapes,
        scratch_shapes=[pltpu.VMEM((n_slots, tm, d), F32)]
        + ([pltpu.VMEM((tm, d), BF16)] if want_bf16 else [])
        + [pltpu.SemaphoreType.DMA((n_slots,)), pltpu.SemaphoreType.DMA((n_slots,))]
        + ([pltpu.SemaphoreType.DMA] if want_bf16 else []),
        compiler_params=_params(2),
        name="ffn",
    )(xs, xb, *[w[0] for w in w_gate_up], w_down[0],
      ln_g.reshape(DEPTH, 2, 1, d), ln_b.reshape(DEPTH, 2, 1, d))
    wb = _as_weights(res[n_out:])
    return res[0], (res[1] if want_bf16 else None), wb[:2], wb[2:]


def _qkv_kernel(xb_ref, wq_ref, wk_ref, wv_ref, q_ref, kb_ref, vb_ref, k_hbm, v_hbm, *rest,
                transposed_v, emit):
    rest = list(rest)
    wbo = [rest.pop(0) for _ in range(3)] if emit else [None] * 3
    kstage, vstage, sem = rest
    i = pl.program_id(0)
    j = pl.program_id(1)
    nj = pl.num_programs(1)
    hb = q_ref.shape[0]
    tm = xb_ref.shape[0]
    kv_rows = kstage.shape[1]

    q, k, v = _dot_side_by_side(xb_ref[...], (wq_ref, wk_ref, wv_ref), wbo)

    def kv_copies(step):
        slot = step % 2
        return [pltpu.make_async_copy(stage.at[slot, :, hh * HEAD_DIM:(hh + 1) * HEAD_DIM],
                                      out.at[:, step * hb + hh, :], sem.at[slot])
                for stage, out in ((kstage, k_hbm), (vstage, v_hbm)) for hh in range(hb)]

    @pl.when(i == pl.num_programs(0) - 1)
    def _():
        @pl.when(j >= 2)
        def _():
            for cp in kv_copies(j - 2):
                cp.wait()

        kstage[j % 2] = k[tm - kv_rows:, :]
        vstage[j % 2] = v[tm - kv_rows:, :]
        for cp in kv_copies(j):
            cp.start()

        @pl.when(j == nj - 1)
        def _():
            for cp in kv_copies(j - 1) + kv_copies(j):
                cp.wait()

    for hh in range(hb):
        cols = slice(hh * HEAD_DIM, (hh + 1) * HEAD_DIM)
        q_ref[hh] = (q[:, cols] * SCALE_LOG2).astype(BF16)
        kb_ref[hh] = k[:, cols].astype(BF16)
        if transposed_v:
            vb_ref[hh] = v[:, cols].T.astype(BF16)
        else:
            vb_ref[hh] = v[:, cols].astype(BF16)


def _qkv(xb, weights, *, tm, tn, kv_rows, transposed_v, emit):
    m, d = xb.shape
    nj = d // tn
    hb = tn // HEAD_DIM
    assert (not emit or m == tm) and kv_rows <= tm and nj >= 2
    kv_spec = pl.BlockSpec(memory_space=pl.ANY)
    kv_shape = jax.ShapeDtypeStruct((kv_rows, N_HEADS, HEAD_DIM), F32)
    hm_spec = pl.BlockSpec((hb, tm, HEAD_DIM), lambda i, j: (j, i, 0))
    hm_shape = jax.ShapeDtypeStruct((N_HEADS, m, HEAD_DIM), BF16)
    if transposed_v:
        v_spec = pl.BlockSpec((hb, HEAD_DIM, tm), lambda i, j: (j, 0, i))
        v_shape = jax.ShapeDtypeStruct((N_HEADS, HEAD_DIM, m), BF16)
    else:
        v_spec, v_shape = hm_spec, hm_shape
    wmap = lambda i, j: (0, j)
    wb_specs, wb_shapes = _wb_outputs(weights, emit, (d, tn), wmap, (d, d))
    q, kb, vb, k, v, *wb = pl.pallas_call(
        functools.partial(_qkv_kernel, transposed_v=transposed_v, emit=emit),
        grid=(m // tm, nj),
        in_specs=[pl.BlockSpec((tm, d), lambda i, j: (i, 0))]
        + [_wspec(w, (d, tn), wmap) for w in weights],
        out_specs=[hm_spec, hm_spec, v_spec, kv_spec, kv_spec] + wb_specs,
        out_shape=[hm_shape, hm_shape, v_shape, kv_shape, kv_shape] + wb_shapes,
        scratch_shapes=[pltpu.VMEM((2, kv_rows, tn), F32), pltpu.VMEM((2, kv_rows, tn), F32),
                        pltpu.SemaphoreType.DMA((2,))],
        compiler_params=_params(2),
        name="qkv",
    )(xb, *[w[0] for w in weights])
    return q, kb, vb, k, v, _as_weights(wb)


def _bias_kernel(rb_ref, heads_ref, pair_ref):
    rb = rb_ref[...]
    hi = rb.astype(BF16)
    r1 = rb - hi.astype(F32)
    mid = r1.astype(BF16)
    lo = (r1 - mid.astype(F32)).astype(BF16)
    src = lax.broadcasted_iota(jnp.int32, (N_REL_PAD, T_PAD), 0)
    m = lax.broadcasted_iota(jnp.int32, (N_REL_PAD, T_PAD), 1)
    idx = jnp.clip(LEFT_ROWS + CHUNK - 1 - m, -MAX_REL, MAX_REL) + MAX_REL
    onehot = (src == idx).astype(BF16)
    t = (_dot(hi, onehot) + _dot(mid, onehot)) + _dot(lo, onehot)

    key = lax.broadcasted_iota(jnp.int32, (PAIR_BAND, PAIR), 0)
    qry = lax.broadcasted_iota(jnp.int32, (PAIR_BAND, PAIR), 1)
    in_band = ((qry < CHUNK) & (key < BAND)) | ((qry >= CHUNK) & (key >= CHUNK))
    chunk_rows = []
    for h in range(N_HEADS):
        rows = jnp.broadcast_to(t[h:h + 1, :], (PAIR, T_PAD))
        rows = pltpu.roll(rows, T_PAD - (CHUNK - 1), 1, stride=1, stride_axis=0)
        chunk_rows.append(rows[:CHUNK, :PAIR_BAND])
        pair_ref[h] = jnp.where(in_band, rows[:, :PAIR_BAND].T * LOG2E, -jnp.inf)
    for hp in range(N_HEADS // 2):
        both = jnp.concatenate(chunk_rows[2 * hp:2 * hp + 2], axis=0)
        heads_ref[hp] = both.T[:BAND] * LOG2E


def _bias_tables(rel_bias):
    rb = jnp.pad(rel_bias, ((0, 0), (0, N_REL_PAD - N_REL)))
    return pl.pallas_call(
        _bias_kernel,
        out_shape=[jax.ShapeDtypeStruct((N_HEADS // 2, BAND, 2 * CHUNK), F32),
                   jax.ShapeDtypeStruct((N_HEADS, PAIR_BAND, PAIR), F32)],
        name="rel_bias_tables",
    )(rb)


PAIRS = LEFT_ROWS // PAIR


def _attn_prompt_kernel(q_ref, kp_ref, kc_ref, vp_ref, vc_ref, bias_ref, o_ref):
    g = pl.program_id(0)
    key = lax.broadcasted_iota(jnp.int32, (PAIR_BAND, PAIR), 0)

    def head(h, first):
        bias = bias_ref[h]
        for p in range(PAIRS):
            rows = slice(p * PAIR, (p + 1) * PAIR)
            lo = slice(p * PAIR, LEFT_ROWS)
            hi = slice(0, (p + 1) * PAIR)
            kband = jnp.concatenate([kp_ref[h, lo, :], kc_ref[h, hi, :]], axis=0)
            vband = jnp.concatenate([vp_ref[h, :, lo], vc_ref[h, :, hi]], axis=1)
            s = _dot_t(kband, q_ref[h, rows, :]) + bias
            if first:
                s = jnp.where(key >= LEFT_ROWS - p * PAIR, s, -jnp.inf)
            e = jnp.exp2(s - jnp.max(s, axis=0, keepdims=True))
            denom = jnp.sum(e, axis=0, keepdims=True)
            o = _dot(vband, e.astype(BF16)) * (1.0 / denom)
            o_ref[h, rows, :] = o.T.astype(BF16)

    def heads(first):
        def body(h, carry):
            head(h, first)
            return carry
        lax.fori_loop(0, N_HEADS, body, 0, unroll=True)

    pl.when(g == 0)(functools.partial(heads, True))
    pl.when(g > 0)(functools.partial(heads, False))


def _attn_prompt(q, kb, vt, bias):
    _, m, _ = q.shape
    blk = (N_HEADS, LEFT_ROWS, HEAD_DIM)
    cur = pl.BlockSpec(blk, lambda g: (0, g, 0))
    prev = pl.BlockSpec(blk, lambda g: (0, jnp.maximum(g - 1, 0), 0))
    blk_t = (N_HEADS, HEAD_DIM, LEFT_ROWS)
    cur_t = pl.BlockSpec(blk_t, lambda g: (0, 0, g))
    prev_t = pl.BlockSpec(blk_t, lambda g: (0, 0, jnp.maximum(g - 1, 0)))
    return pl.pallas_call(
        _attn_prompt_kernel,
        grid=(m // LEFT_ROWS,),
        in_specs=[cur, prev, cur, prev_t, cur_t,
                  pl.BlockSpec((N_HEADS, PAIR_BAND, PAIR), lambda g: (0, 0, 0))],
        out_specs=cur,
        out_shape=jax.ShapeDtypeStruct(q.shape, BF16),
        compiler_params=_params(1),
        name="attn_prompt",
    )(q, kb, kb, vt, vt, bias)


def _cache_copies(ck_hbm, cv_hbm, kbuf, vbuf, sem, stream, slot):
    return [pltpu.make_async_copy(src.at[stream, :, h, :],
                                  buf.at[slot, :, h * HEAD_DIM:(h + 1) * HEAD_DIM],
                                  sem.at[slot])
            for src, buf in ((ck_hbm, kbuf), (cv_hbm, vbuf)) for h in range(N_HEADS)]


def _attn_sample_kernel(q_ref, kc_ref, vc_ref, ck_hbm, cv_hbm, bias_ref, o_ref, kbuf, vbuf, sem):
    n = pl.program_id(0)
    slot = n % 2
    copies = functools.partial(_cache_copies, ck_hbm, cv_hbm, kbuf, vbuf, sem)

    def start_all(stream, to_slot):
        for idx, cp in enumerate(copies(stream, to_slot)):
            cp.start(priority=idx % 2)

    @pl.when(n == 0)
    def _():
        start_all(0, 0)

    @pl.when(n + 1 < pl.num_programs(0))
    def _():
        start_all(n + 1, 1 - slot)

    for cp in copies(n, slot):
        cp.wait()

    zeros = jnp.zeros((CHUNK, HEAD_DIM), BF16)
    for hp in range(N_HEADS // 2):
        h0, h1 = 2 * hp, 2 * hp + 1
        cols = slice(h0 * HEAD_DIM, (h1 + 1) * HEAD_DIM)
        knew = jnp.concatenate([kc_ref[h0], kc_ref[h1]], axis=1)
        vnew = jnp.concatenate([vc_ref[h0], vc_ref[h1]], axis=1)
        kband = jnp.concatenate([kbuf[slot, :, cols].astype(BF16), knew], axis=0)
        vband = jnp.concatenate([vbuf[slot, :, cols].astype(BF16), vnew], axis=0)
        qbd = jnp.concatenate([jnp.concatenate([q_ref[h0], zeros], axis=1),
                               jnp.concatenate([zeros, q_ref[h1]], axis=1)], axis=0)
        s = _dot_t(kband, qbd) + bias_ref[hp]
        e = jnp.exp2(s - jnp.max(s, axis=0, keepdims=True))
        denom = jnp.sum(e, axis=0, keepdims=True)
        o = lax.dot_general(vband, e.astype(BF16), (((0,), (0,)), ((), ())),
                            preferred_element_type=F32) * (1.0 / denom)
        o_ref[h0] = o[:HEAD_DIM].T[:CHUNK].astype(BF16)
        o_ref[h1] = o[HEAD_DIM:].T[CHUNK:].astype(BF16)


def _attn_sample(q, kb, vb, cache_k, cache_v, bias):
    n = cache_k.shape[0]
    new = pl.BlockSpec((N_HEADS, CHUNK, HEAD_DIM), lambda s: (0, s, 0))
    cache = pl.BlockSpec(memory_space=pl.ANY)
    return pl.pallas_call(
        _attn_sample_kernel,
        grid=(n,),
        in_specs=[new, new, new, cache, cache,
                  pl.BlockSpec((N_HEADS // 2, BAND, 2 * CHUNK), lambda s: (0, 0, 0))],
        out_specs=new,
        out_shape=jax.ShapeDtypeStruct(q.shape, BF16),
        scratch_shapes=[pltpu.VMEM((2, LEFT_ROWS, D_MODEL), F32),
                        pltpu.VMEM((2, LEFT_ROWS, D_MODEL), F32),
                        pltpu.SemaphoreType.DMA((2,))],
        compiler_params=_params(1),
        name="attn_sample",
    )(q, kb, vb, cache_k, cache_v, bias)


TM = 1024
TK_OUT = 512
TM_PROJ = 512


def _tn_three(emit):
    return 256 if emit else 512


def _tf_fused(emit):
    return 256 if emit else 512


def _f32_weights(w_in_a, w_out_a, w_kv, w_q, w_o, w_gate_up, w_down):
    return dict(
        w_in=tuple((w_in_a, 0, t * D_MODEL) for t in range(3)),
        w_out=(w_out_a, 0, 0),
        w_qkv=((w_q, 0, 0), (w_kv, 0, 0), (w_kv, 0, D_MODEL)),
        w_o=(w_o, 0, 0),
        w_gate_up=tuple(tuple((w_gate_up, l, t * D_FF) for t in range(2)) for l in range(DEPTH)),
        w_down=tuple((w_down, l, 0) for l in range(DEPTH)),
    )


def _trunk(x, hist, seq_len, cache, bias_tables, kv_rows, w, conv_w, ln_g, ln_b, *, emit):
    wb = dict(w_gate_up=[None] * DEPTH, w_down=[None] * DEPTH)

    def ffn(x, xb, layer):
        more = layer + 1 < DEPTH
        x, xb, wb["w_gate_up"][layer], down = _ffn_fused(
            x, xb, w["w_gate_up"][layer], w["w_down"][layer], ln_g, ln_b, layer, tm=TM,
            tf=_tf_fused(emit), out_scale=ALPHA if more else 1.0, want_bf16=more, emit=emit)
        wb["w_down"][layer] = down[0] if emit else None
        return x, xb

    def proj(a, name, ln_idx, prescaled):
        if emit:
            x_new, xb_new, (wb[name],) = _mm_res_ln(a, w[name], x, ln_g, ln_b, ln_idx, tm=TM,
                                                    tk=TK_OUT, emit=True, prescaled=prescaled)
            return x_new, xb_new
        return _proj_res_ln(a, w[name], x, ln_g, ln_b, ln_idx, tm=TM_PROJ, prescaled=prescaled)

    g, conv_state, wb["w_in"] = _conv_gate(x, w["w_in"], hist, conv_w, seq_len=seq_len,
                                           tm=TM, tn=_tn_three(emit), emit=emit)
    x, xb = proj(g, "w_out", (0, 0), prescaled=False)
    x, xb = ffn(x, xb, 0)

    bias_heads, bias_pair = bias_tables
    q, kb, vb, k, v, wb["w_qkv"] = _qkv(xb, w["w_qkv"], tm=TM, tn=_tn_three(emit),
                                        kv_rows=kv_rows, transposed_v=cache is None, emit=emit)
    if cache is None:
        att = _attn_prompt(q, kb, vb, bias_pair)
    else:
        att = _attn_sample(q, kb, vb, cache[0], cache[1], bias_heads)
    x, xb = proj(att, "w_o", (1, 0), prescaled=True)
    x, _ = ffn(x, xb, 1)
    return (x, conv_state, k, v), wb


def kernel(x_prompt, x_sample, state_conv, cache_k, cache_v, w_in_a, conv_w, w_out_a, w_kv, w_q,
           w_o, rel_bias, ln_g, ln_b, w_gate_up, w_down):
    batch, seq, d = x_prompt.shape
    dec_batch, dec_seq, _ = x_sample.shape
    assert batch == 1 and DEPTH == 2 and dec_seq == CHUNK and cache_k.shape[1] == LEFT_ROWS
    assert dec_batch * dec_seq == TM
    bias_tables = _bias_tables(rel_bias[0])
    w_f32 = _f32_weights(w_in_a, w_out_a, w_kv, w_q, w_o, w_gate_up, w_down)

    (y_s, conv_s, k_s, v_s), w_bf16 = _trunk(
        x_sample.reshape(dec_batch * dec_seq, d), state_conv[0], dec_seq, (cache_k, cache_v),
        bias_tables, dec_batch * dec_seq, w_f32, conv_w, ln_g, ln_b, emit=True)
    conv_zero = jnp.zeros((batch, CONV_W - 1, d), x_prompt.dtype)
    (y_p, conv_p, k_p, v_p), _ = _trunk(
        x_prompt.reshape(seq, d), conv_zero, seq, None, bias_tables, LEFT_ROWS,
        w_bf16, conv_w, ln_g, ln_b, emit=False)

    kv_prompt = (batch, LEFT_ROWS, N_HEADS, HEAD_DIM)
    kv_sample = (dec_batch, dec_seq, N_HEADS, HEAD_DIM)
    return (y_p.reshape(batch, seq, d),
            y_s.reshape(dec_batch, dec_seq, d),
            conv_p.reshape(1, batch, CONV_W - 1, d),
            conv_s.reshape(1, dec_batch, CONV_W - 1, d),
            k_p.reshape(kv_prompt),
            v_p.reshape(kv_prompt),
            k_s.reshape(kv_sample),
            v_s.reshape(kv_sample))
```

```python
import functools

import jax
import jax.numpy as jnp
from jax import lax
from jax.experimental import pallas as pl
from jax.experimental.pallas import tpu as pltpu

D_MODEL = 2048
DEPTH = 2
CHUNK = 64
N_LEFT_CHUNKS = 8
LEFT_ROWS = N_LEFT_CHUNKS * CHUNK
BAND = LEFT_ROWS + CHUNK
N_HEADS = 16
HEAD_DIM = D_MODEL // N_HEADS
MAX_REL = 256
N_REL = 2 * MAX_REL + 1
CONV_W = 3
D_FF = -(-8 * D_MODEL // (3 * 256)) * 256
ALPHA = (2.0 * DEPTH) ** 0.25
LN_EPS = 1e-5
SCALE = HEAD_DIM ** -0.5
LOG2E = 1.4426950408889634
SCALE_LOG2 = SCALE * LOG2E

LANES = 128
VMEM_LIMIT_BYTES = 56 * 1024 * 1024

PAIR = 2 * CHUNK
PAIR_BAND = BAND + CHUNK
N_REL_PAD = -(-N_REL // LANES) * LANES
T_PAD = -(-(PAIR_BAND + PAIR) // LANES) * LANES

BF16 = jnp.bfloat16
F32 = jnp.float32


def _params(n_axes):
    return pltpu.CompilerParams(dimension_semantics=("arbitrary",) * n_axes,
                                vmem_limit_bytes=VMEM_LIMIT_BYTES)


def _dot(a, b):
    return jnp.dot(a, b, preferred_element_type=F32)


def _dot_t(a, b):
    return lax.dot_general(a, b, (((1,), (1,)), ((), ())), preferred_element_type=F32)


def _wspec(weight, blk, imap):
    arr, layer, first_col = weight
    first = first_col // blk[1]
    if arr.ndim == 3:
        return pl.BlockSpec((None,) + blk, lambda *g: (layer,) + _shift(imap(*g), first))
    return pl.BlockSpec(blk, lambda *g: _shift(imap(*g), first))


def _shift(idx, first):
    return (idx[0], idx[1] + first)


def _wload(w_ref, wb_ref):
    w = w_ref[...]
    if w.dtype != BF16:
        w = w.astype(BF16)
    if wb_ref is not None:
        wb_ref[...] = w
    return w


def _dot_side_by_side(x, w_refs, wb_refs):
    tn = w_refs[0].shape[1]
    w = jnp.concatenate([_wload(w_ref, wb_ref) for w_ref, wb_ref in zip(w_refs, wb_refs)], axis=1)
    y = _dot(x, w)
    return [y[:, t * tn:(t + 1) * tn] for t in range(len(w_refs))]


def _wb_outputs(weights, emit, blk, imap, shape):
    if not emit:
        return [], []
    return ([pl.BlockSpec(blk, imap)] * len(weights),
            [jax.ShapeDtypeStruct(shape, BF16)] * len(weights))


def _as_weights(arrs):
    return tuple((a, 0, 0) for a in arrs)


def _conv_gate_kernel(x_ref, wb_ref, wc_ref, wh_ref, hist_ref, cw_ref, g_ref, state_ref, *rest,
                      nseq, emit, carried):
    rest = list(rest)
    wbo = [rest.pop(0) for _ in range(3)] if emit else [None] * 3
    xb_ref = rest.pop(0)
    i = pl.program_id(0)
    j = pl.program_id(1)
    tm, tn = g_ref.shape
    ls = tm // nseq

    @pl.when(j == 0)
    def _():
        xb_ref[...] = x_ref[...].astype(BF16)

    b, c, h = _dot_side_by_side(xb_ref[...], (wb_ref, wc_ref, wh_ref), wbo)
    u = c * h

    if carried:
        carry_ref = rest.pop(0)

        @pl.when(i == 0)
        def _():
            carry_ref[j] = hist_ref[0]

        prev = carry_ref[j][None]
    else:
        prev = hist_ref[...]

    shape3 = (nseq, ls, tn)
    pos = lax.broadcasted_iota(jnp.int32, shape3, 1)
    u3 = u.reshape(shape3)
    p1 = pltpu.roll(u, 1, 0).reshape(shape3)
    p2 = pltpu.roll(u, 2, 0).reshape(shape3)
    h0 = prev[:, 0:1, :]
    h1 = prev[:, 1:2, :]
    p1 = jnp.where(pos == 0, h1, p1)
    p2 = jnp.where(pos == 0, h0, jnp.where(pos == 1, h1, p2))
    conv = cw_ref[0:1, :] * p2 + cw_ref[1:2, :] * p1 + cw_ref[2:3, :] * u3
    g_ref[...] = (b * conv.reshape(tm, tn)).astype(BF16)

    new_state = u3[:, ls - 2:ls, :]
    state_ref[...] = new_state
    if carried:
        carry_ref[j] = new_state[0]


def _conv_gate(x, weights, hist, conv_w, *, seq_len, tm, tn, emit):
    m, d = x.shape
    nj = d // tn
    carried = seq_len > tm
    nseq = 1 if carried else tm // seq_len
    assert not emit or m == tm
    hist_map = (lambda i, j: (0, 0, j)) if carried else (lambda i, j: (i, 0, j))
    scratch = [pltpu.VMEM((tm, d), BF16)]
    if carried:
        scratch.append(pltpu.VMEM((nj, CONV_W - 1, tn), F32))
    wmap = lambda i, j: (0, j)
    wb_specs, wb_shapes = _wb_outputs(weights, emit, (d, tn), wmap, (d, d))
    n_states = (m // tm) * nseq
    g, states, *wb = pl.pallas_call(
        functools.partial(_conv_gate_kernel, nseq=nseq, emit=emit, carried=carried),
        grid=(m // tm, nj),
        in_specs=[pl.BlockSpec((tm, d), lambda i, j: (i, 0))]
        + [_wspec(w, (d, tn), wmap) for w in weights]
        + [pl.BlockSpec((nseq, CONV_W - 1, tn), hist_map),
           pl.BlockSpec((None, CONV_W, tn), lambda i, j: (0, 0, j))],
        out_specs=[
            pl.BlockSpec((tm, tn), lambda i, j: (i, j)),
            pl.BlockSpec((nseq, CONV_W - 1, tn), lambda i, j: (i, 0, j)),
        ] + wb_specs,
        out_shape=[
            jax.ShapeDtypeStruct((m, d), BF16),
            jax.ShapeDtypeStruct((n_states, CONV_W - 1, d), F32),
        ] + wb_shapes,
        scratch_shapes=scratch,
        compiler_params=_params(2),
        name="conv_gate",
    )(x, *[w[0] for w in weights], hist, conv_w)
    return g, states[-hist.shape[0]:], _as_weights(wb)


LN_ROWS = 64


def _layer_norm_rows(y, gain, bias, out_ref, outb_ref, out_scale=1.0):
    for r in range(0, y.shape[0], LN_ROWS):
        rows = slice(r, r + LN_ROWS)
        yr = y[rows, :]
        mu = jnp.mean(yr, axis=-1, keepdims=True)
        yc = yr - mu
        var = jnp.mean(yc * yc, axis=-1, keepdims=True)
        z = yc * lax.rsqrt(var + LN_EPS) * gain + bias
        out_ref[rows, :] = z if out_scale == 1.0 else z * out_scale
        if outb_ref is not None:
            outb_ref[rows, :] = z.astype(BF16)


def _mm_res_ln_kernel(a_ref, w_ref, x_ref, g_ref, b_ref, out_ref, outb_ref, *wbo, prescaled):
    k = pl.program_id(1)

    @pl.when(k == 0)
    def _():
        out_ref[...] = x_ref[...] if prescaled else ALPHA * x_ref[...]

    if len(a_ref.shape) == 3:
        a = jnp.concatenate([a_ref[h] for h in range(a_ref.shape[0])], axis=1)
    else:
        a = a_ref[...]
    out_ref[...] += _dot(a, _wload(w_ref, wbo[0] if wbo else None))

    @pl.when(k == pl.num_programs(1) - 1)
    def _():
        _layer_norm_rows(out_ref, g_ref[...], b_ref[...], out_ref, outb_ref, out_scale=ALPHA)


def _mm_res_ln(a, weight, x, ln_g, ln_b, ln_idx, *, tm, tk, emit, prescaled):
    m, d = x.shape
    kdim = weight[0].shape[-2]
    assert not emit or m == tm
    if a.ndim == 3:
        a_spec = pl.BlockSpec((tk // HEAD_DIM, tm, HEAD_DIM), lambda i, k: (k, i, 0))
    else:
        a_spec = pl.BlockSpec((tm, tk), lambda i, k: (i, k))
    wmap = lambda i, k: (k, 0)
    wb_specs, wb_shapes = _wb_outputs((weight,), emit, (tk, d), wmap, (kdim, d))
    layer, sub = ln_idx
    ln_spec = pl.BlockSpec((None, None, 1, d), lambda i, k: (layer, sub, 0, 0))
    x_new, xb_new, *wb = pl.pallas_call(
        functools.partial(_mm_res_ln_kernel, prescaled=prescaled),
        grid=(m // tm, kdim // tk),
        in_specs=[
            a_spec,
            _wspec(weight, (tk, d), wmap),
            pl.BlockSpec((tm, d), lambda i, k: (i, 0)),
            ln_spec,
            ln_spec,
        ],
        out_specs=[
            pl.BlockSpec((tm, d), lambda i, k: (i, 0)),
            pl.BlockSpec((tm, d), lambda i, k: (i, 0)),
        ] + wb_specs,
        out_shape=[
            jax.ShapeDtypeStruct((m, d), F32),
            jax.ShapeDtypeStruct((m, d), BF16),
        ] + wb_shapes,
        compiler_params=_params(2),
        name="mm_res_ln",
    )(a, weight[0], x, ln_g.reshape(DEPTH, 2, 1, d), ln_b.reshape(DEPTH, 2, 1, d))
    return x_new, xb_new, _as_weights(wb)


def _proj_res_ln_kernel(a_ref, w_ref, x_ref, g_ref, b_ref, out_ref, outb_ref, *, prescaled):
    if len(a_ref.shape) == 3:
        a = jnp.concatenate([a_ref[h] for h in range(a_ref.shape[0])], axis=1)
    else:
        a = a_ref[...]
    residual = x_ref[...] if prescaled else ALPHA * x_ref[...]
    y = residual + _dot(a, w_ref[...])
    _layer_norm_rows(y, g_ref[...], b_ref[...], out_ref, outb_ref, out_scale=ALPHA)


def _proj_res_ln(a, weight, x, ln_g, ln_b, ln_idx, *, tm, prescaled):
    m, d = x.shape
    w = weight[0]
    assert w.shape == (d, d) and w.dtype == BF16
    if a.ndim == 3:
        a_spec = pl.BlockSpec((N_HEADS, tm, HEAD_DIM), lambda i: (0, i, 0))
    else:
        a_spec = pl.BlockSpec((tm, d), lambda i: (i, 0))
    layer, sub = ln_idx
    ln_spec = pl.BlockSpec((None, None, 1, d), lambda i: (layer, sub, 0, 0))
    return pl.pallas_call(
        functools.partial(_proj_res_ln_kernel, prescaled=prescaled),
        grid=(m // tm,),
        in_specs=[
            a_spec,
            pl.BlockSpec((d, d), lambda i: (0, 0)),
            pl.BlockSpec((tm, d), lambda i: (i, 0)),
            ln_spec,
            ln_spec,
        ],
        out_specs=[
            pl.BlockSpec((tm, d), lambda i: (i, 0)),
            pl.BlockSpec((tm, d), lambda i: (i, 0)),
        ],
        out_shape=[
            jax.ShapeDtypeStruct((m, d), F32),
            jax.ShapeDtypeStruct((m, d), BF16),
        ],
        compiler_params=_params(1),
        name="proj_res_ln",
    )(a, w, x, ln_g.reshape(DEPTH, 2, 1, d), ln_b.reshape(DEPTH, 2, 1, d))


SWAP_STEP = 2


def _ffn_kernel(xs_hbm, xb_ref, wg_ref, wu_ref, wd_ref, g_ref, b_ref, *rest,
                n_tiles, out_scale, want_bf16, emit):
    rest = list(rest)
    y_hbm = rest.pop(0)
    yb_hbm = rest.pop(0) if want_bf16 else None
    wbo = [rest.pop(0) for _ in range(3)] if emit else [None] * 3
    acc = rest.pop(0)
    ybbuf = rest.pop(0) if want_bf16 else None
    sem_x, sem_y = rest.pop(0), rest.pop(0)
    sem_yb = rest.pop(0) if want_bf16 else None
    i = pl.program_id(0)
    j = pl.program_id(1)
    last_j = pl.num_programs(1) - 1
    n_slots, tm = acc.shape[:2]
    tf = wg_ref.shape[1]
    slot = i % n_slots
    other = (i + 1) % n_slots

    def x_copy(tile, to_slot):
        return pltpu.make_async_copy(xs_hbm.at[pl.ds(tile * tm, tm), :], acc.at[to_slot],
                                     sem_x.at[to_slot])

    def y_copies(tile, from_slot):
        rows = pl.ds(tile * tm, tm)
        copies = [pltpu.make_async_copy(acc.at[from_slot], y_hbm.at[rows, :], sem_y.at[from_slot])]
        if want_bf16:
            copies.append(pltpu.make_async_copy(ybbuf, yb_hbm.at[rows, :], sem_yb))
        return copies

    @pl.when((i == 0) & (j == 0))
    def _():
        x_copy(0, 0).start()

    @pl.when(j == 0)
    def _():
        x_copy(i, slot).wait()

    w_gate_up = jnp.concatenate([_wload(wg_ref, wbo[0]), _wload(wu_ref, wbo[1])], axis=1)
    gu = _dot(xb_ref[...], w_gate_up)
    h = (jax.nn.silu(gu[:, :tf]) * gu[:, tf:]).astype(BF16)
    acc[slot] += _dot(h, _wload(wd_ref, wbo[2]))

    @pl.when((j == SWAP_STEP) & (i >= 1))
    def _():
        for cp in y_copies(i - 1, other):
            cp.wait()

    @pl.when((j == SWAP_STEP) & (i + 1 < n_tiles))
    def _():
        x_copy(i + 1, other).start()

    @pl.when(j == last_j)
    def _():
        _layer_norm_rows(acc.at[slot], g_ref[...], b_ref[...], acc.at[slot], ybbuf, out_scale)
        for cp in y_copies(i, slot):
            cp.start()

    @pl.when((j == last_j) & (i == n_tiles - 1))
    def _():
        for cp in y_copies(i, slot):
            cp.wait()


def _ffn_fused(xs, xb, w_gate_up, w_down, ln_g, ln_b, layer, *, tm, tf, out_scale, want_bf16,
               emit):
    m, d = xs.shape
    nf = D_FF // tf
    n_tiles = m // tm
    n_slots = min(2, n_tiles)
    assert nf > SWAP_STEP and (not emit or n_tiles == 1)
    ln_spec = pl.BlockSpec((None, None, 1, d), lambda i, j: (layer, 1, 0, 0))
    any_spec = pl.BlockSpec(memory_space=pl.ANY)
    n_out = 2 if want_bf16 else 1
    up_map = lambda i, j: (0, j)
    down_map = lambda i, j: (j, 0)
    up_specs, up_shapes = _wb_outputs(w_gate_up, emit, (d, tf), up_map, (d, D_FF))
    down_specs, down_shapes = _wb_outputs((w_down,), emit, (tf, d), down_map, (D_FF, d))
    res = pl.pallas_call(
        functools.partial(_ffn_kernel, n_tiles=n_tiles, out_scale=out_scale, want_bf16=want_bf16,
                          emit=emit),
        grid=(n_tiles, nf),
        in_specs=[any_spec, pl.BlockSpec((tm, d), lambda i, j: (i, 0))]
        + [_wspec(w, (d, tf), up_map) for w in w_gate_up]
        + [_wspec(w_down, (tf, d), down_map), ln_spec, ln_spec],
        out_specs=[any_spec] * n_out + up_specs + down_specs,
        out_shape=[jax.ShapeDtypeStruct((m, d), F32), jax.ShapeDtypeStruct((m, d), BF16)][:n_out]
        + up_shapes + down_shapes,
        scratch_shapes=[pltpu.VMEM((n_slots, tm, d), F32)]
        + ([pltpu.VMEM((tm, d), BF16)] if want_bf16 else [])
        + [pltpu.SemaphoreType.DMA((n_slots,)), pltpu.SemaphoreType.DMA((n_slots,))]
        + ([pltpu.SemaphoreType.DMA] if want_bf16 else []),
        compiler_params=_params(2),
        name="ffn",
    )(xs, xb, *[w[0] for w in w_gate_up], w_down[0],
      ln_g.reshape(DEPTH, 2, 1, d), ln_b.reshape(DEPTH, 2, 1, d))
    wb = _as_weights(res[n_out:])
    return res[0], (res[1] if want_bf16 else None), wb[:2], wb[2:]


def _qkv_kernel(xb_ref, wq_ref, wk_ref, wv_ref, q_ref, kb_ref, vb_ref, k_hbm, v_hbm, *rest,
                transposed_v, emit):
    rest = list(rest)
    wbo = [rest.pop(0) for _ in range(3)] if emit else [None] * 3
    kstage, vstage, sem = rest
    i = pl.program_id(0)
    j = pl.program_id(1)
    nj = pl.num_programs(1)
    hb = q_ref.shape[0]
    tm = xb_ref.shape[0]
    kv_rows = kstage.shape[1]

    q, k, v = _dot_side_by_side(xb_ref[...], (wq_ref, wk_ref, wv_ref), wbo)

    def kv_copies(step):
        slot = step % 2
        return [pltpu.make_async_copy(stage.at[slot, :, hh * HEAD_DIM:(hh + 1) * HEAD_DIM],
                                      out.at[:, step * hb + hh, :], sem.at[slot])
                for stage, out in ((kstage, k_hbm), (vstage, v_hbm)) for hh in range(hb)]

    @pl.when(i == pl.num_programs(0) - 1)
    def _():
        @pl.when(j >= 2)
        def _():
            for cp in kv_copies(j - 2):
                cp.wait()

        kstage[j % 2] = k[tm - kv_rows:, :]
        vstage[j % 2] = v[tm - kv_rows:, :]
        for cp in kv_copies(j):
            cp.start()

        @pl.when(j == nj - 1)
        def _():
            for cp in kv_copies(j - 1) + kv_copies(j):
                cp.wait()

    for hh in range(hb):
        cols = slice(hh * HEAD_DIM, (hh + 1) * HEAD_DIM)
        q_ref[hh] = (q[:, cols] * SCALE_LOG2).astype(BF16)
        kb_ref[hh] = k[:, cols].astype(BF16)
        if transposed_v:
            vb_ref[hh] = v[:, cols].T.astype(BF16)
        else:
            vb_ref[hh] = v[:, cols].astype(BF16)


def _qkv(xb, weights, *, tm, tn, kv_rows, transposed_v, emit):
    m, d = xb.shape
    nj = d // tn
    hb = tn // HEAD_DIM
    assert (not emit or m == tm) and kv_rows <= tm and nj >= 2
    kv_spec = pl.BlockSpec(memory_space=pl.ANY)
    kv_shape = jax.ShapeDtypeStruct((kv_rows, N_HEADS, HEAD_DIM), F32)
    hm_spec = pl.BlockSpec((hb, tm, HEAD_DIM), lambda i, j: (j, i, 0))
    hm_shape = jax.ShapeDtypeStruct((N_HEADS, m, HEAD_DIM), BF16)
    if transposed_v:
        v_spec = pl.BlockSpec((hb, HEAD_DIM, tm), lambda i, j: (j, 0, i))
        v_shape = jax.ShapeDtypeStruct((N_HEADS, HEAD_DIM, m), BF16)
    else:
        v_spec, v_shape = hm_spec, hm_shape
    wmap = lambda i, j: (0, j)
    wb_specs, wb_shapes = _wb_outputs(weights, emit, (d, tn), wmap, (d, d))
    q, kb, vb, k, v, *wb = pl.pallas_call(
        functools.partial(_qkv_kernel, transposed_v=transposed_v, emit=emit),
        grid=(m // tm, nj),
        in_specs=[pl.BlockSpec((tm, d), lambda i, j: (i, 0))]
        + [_wspec(w, (d, tn), wmap) for w in weights],
        out_specs=[hm_spec, hm_spec, v_spec, kv_spec, kv_spec] + wb_specs,
        out_shape=[hm_shape, hm_shape, v_shape, kv_shape, kv_shape] + wb_shapes,
        scratch_shapes=[pltpu.VMEM((2, kv_rows, tn), F32), pltpu.VMEM((2, kv_rows, tn), F32),
                        pltpu.SemaphoreType.DMA((2,))],
        compiler_params=_params(2),
        name="qkv",
    )(xb, *[w[0] for w in weights])
    return q, kb, vb, k, v, _as_weights(wb)


def _bias_kernel(rb_ref, heads_ref, pair_ref):
    rb = rb_ref[...]
    hi = rb.astype(BF16)
    r1 = rb - hi.astype(F32)
    mid = r1.astype(BF16)
    lo = (r1 - mid.astype(F32)).astype(BF16)
    src = lax.broadcasted_iota(jnp.int32, (N_REL_PAD, T_PAD), 0)
    m = lax.broadcasted_iota(jnp.int32, (N_REL_PAD, T_PAD), 1)
    idx = jnp.clip(LEFT_ROWS + CHUNK - 1 - m, -MAX_REL, MAX_REL) + MAX_REL
    onehot = (src == idx).astype(BF16)
    t = (_dot(hi, onehot) + _dot(mid, onehot)) + _dot(lo, onehot)

    key = lax.broadcasted_iota(jnp.int32, (PAIR_BAND, PAIR), 0)
    qry = lax.broadcasted_iota(jnp.int32, (PAIR_BAND, PAIR), 1)
    in_band = ((qry < CHUNK) & (key < BAND)) | ((qry >= CHUNK) & (key >= CHUNK))
    chunk_rows = []
    for h in range(N_HEADS):
        rows = jnp.broadcast_to(t[h:h + 1, :], (PAIR, T_PAD))
        rows = pltpu.roll(rows, T_PAD - (CHUNK - 1), 1, stride=1, stride_axis=0)
        chunk_rows.append(rows[:CHUNK, :PAIR_BAND])
        pair_ref[h] = jnp.where(in_band, rows[:, :PAIR_BAND].T * LOG2E, -jnp.inf)
    for hp in range(N_HEADS // 2):
        both = jnp.concatenate(chunk_rows[2 * hp:2 * hp + 2], axis=0)
        heads_ref[hp] = both.T[:BAND] * LOG2E


def _bias_tables(rel_bias):
    rb = jnp.pad(rel_bias, ((0, 0), (0, N_REL_PAD - N_REL)))
    return pl.pallas_call(
        _bias_kernel,
        out_shape=[jax.ShapeDtypeStruct((N_HEADS // 2, BAND, 2 * CHUNK), F32),
                   jax.ShapeDtypeStruct((N_HEADS, PAIR_BAND, PAIR), F32)],
        name="rel_bias_tables",
    )(rb)


PAIRS = LEFT_ROWS // PAIR


def _attn_prompt_kernel(q_ref, kp_ref, kc_ref, vp_ref, vc_ref, bias_ref, o_ref):
    g = pl.program_id(0)
    key = lax.broadcasted_iota(jnp.int32, (PAIR_BAND, PAIR), 0)

    def head(h, first):
        bias = bias_ref[h]
        for p in range(PAIRS):
            rows = slice(p * PAIR, (p + 1) * PAIR)
            lo = slice(p * PAIR, LEFT_ROWS)
            hi = slice(0, (p + 1) * PAIR)
            kband = jnp.concatenate([kp_ref[h, lo, :], kc_ref[h, hi, :]], axis=0)
            vband = jnp.concatenate([vp_ref[h, :, lo], vc_ref[h, :, hi]], axis=1)
            s = _dot_t(kband, q_ref[h, rows, :]) + bias
            if first:
                s = jnp.where(key >= LEFT_ROWS - p * PAIR, s, -jnp.inf)
            e = jnp.exp2(s - jnp.max(s, axis=0, keepdims=True))
            denom = jnp.sum(e, axis=0, keepdims=True)
            o = _dot(vband, e.astype(BF16)) * (1.0 / denom)
            o_ref[h, rows, :] = o.T.astype(BF16)

    def heads(first):
        def body(h, carry):
            head(h, first)
            return carry
        lax.fori_loop(0, N_HEADS, body, 0, unroll=True)

    pl.when(g == 0)(functools.partial(heads, True))
    pl.when(g > 0)(functools.partial(heads, False))


def _attn_prompt(q, kb, vt, bias):
    _, m, _ = q.shape
    blk = (N_HEADS, LEFT_ROWS, HEAD_DIM)
    cur = pl.BlockSpec(blk, lambda g: (0, g, 0))
    prev = pl.BlockSpec(blk, lambda g: (0, jnp.maximum(g - 1, 0), 0))
    blk_t = (N_HEADS, HEAD_DIM, LEFT_ROWS)
    cur_t = pl.BlockSpec(blk_t, lambda g: (0, 0, g))
    prev_t = pl.BlockSpec(blk_t, lambda g: (0, 0, jnp.maximum(g - 1, 0)))
    return pl.pallas_call(
        _attn_prompt_kernel,
        grid=(m // LEFT_ROWS,),
        in_specs=[cur, prev, cur, prev_t, cur_t,
                  pl.BlockSpec((N_HEADS, PAIR_BAND, PAIR), lambda g: (0, 0, 0))],
        out_specs=cur,
        out_shape=jax.ShapeDtypeStruct(q.shape, BF16),
        compiler_params=_params(1),
        name="attn_prompt",
    )(q, kb, kb, vt, vt, bias)


def _cache_copies(ck_hbm, cv_hbm, kbuf, vbuf, sem, stream, slot):
    return [pltpu.make_async_copy(src.at[stream, :, h, :],
                                  buf.at[slot, :, h * HEAD_DIM:(h + 1) * HEAD_DIM],
                                  sem.at[slot])
            for src, buf in ((ck_hbm, kbuf), (cv_hbm, vbuf)) for h in range(N_HEADS)]


def _attn_sample_kernel(q_ref, kc_ref, vc_ref, ck_hbm, cv_hbm, bias_ref, o_ref, kbuf, vbuf, sem):
    n = pl.program_id(0)
    slot = n % 2
    copies = functools.partial(_cache_copies, ck_hbm, cv_hbm, kbuf, vbuf, sem)

    @pl.when(n == 0)
    def _():
        for cp in copies(0, 0):
            cp.start()

    @pl.when(n + 1 < pl.num_programs(0))
    def _():
        for cp in copies(n + 1, 1 - slot):
            cp.start()

    for cp in copies(n, slot):
        cp.wait()

    zeros = jnp.zeros((CHUNK, HEAD_DIM), BF16)
    for hp in range(N_HEADS // 2):
        h0, h1 = 2 * hp, 2 * hp + 1
        cols = slice(h0 * HEAD_DIM, (h1 + 1) * HEAD_DIM)
        knew = jnp.concatenate([kc_ref[h0], kc_ref[h1]], axis=1)
        vnew = jnp.concatenate([vc_ref[h0], vc_ref[h1]], axis=1)
        kband = jnp.concatenate([kbuf[slot, :, cols].astype(BF16), knew], axis=0)
        vband = jnp.concatenate([vbuf[slot, :, cols].astype(BF16), vnew], axis=0)
        qbd = jnp.concatenate([jnp.concatenate([q_ref[h0], zeros], axis=1),
                               jnp.concatenate([zeros, q_ref[h1]], axis=1)], axis=0)
        s = _dot_t(kband, qbd) + bias_ref[hp]
        e = jnp.exp2(s - jnp.max(s, axis=0, keepdims=True))
        denom = jnp.sum(e, axis=0, keepdims=True)
        o = lax.dot_general(vband, e.astype(BF16), (((0,), (0,)), ((), ())),
                            preferred_element_type=F32) * (1.0 / denom)
        o_ref[h0] = o[:HEAD_DIM].T[:CHUNK].astype(BF16)
        o_ref[h1] = o[HEAD_DIM:].T[CHUNK:].astype(BF16)


def _attn_sample(q, kb, vb, cache_k, cache_v, bias):
    n = cache_k.shape[0]
    new = pl.BlockSpec((N_HEADS, CHUNK, HEAD_DIM), lambda s: (0, s, 0))
    cache = pl.BlockSpec(memory_space=pl.ANY)
    return pl.pallas_call(
        _attn_sample_kernel,
        grid=(n,),
        in_specs=[new, new, new, cache, cache,
                  pl.BlockSpec((N_HEADS // 2, BAND, 2 * CHUNK), lambda s: (0, 0, 0))],
        out_specs=new,
        out_shape=jax.ShapeDtypeStruct(q.shape, BF16),
        scratch_shapes=[pltpu.VMEM((2, LEFT_ROWS, D_MODEL), F32),
                        pltpu.VMEM((2, LEFT_ROWS, D_MODEL), F32),
                        pltpu.SemaphoreType.DMA((2,))],
        compiler_params=_params(1),
        name="attn_sample",
    )(q, kb, vb, cache_k, cache_v, bias)


TM = 1024
TK_OUT = 512
TM_PROJ = 512


def _tn_three(emit):
    return 256 if emit else 512


def _tf_fused(emit):
    return 256 if emit else 512


def _f32_weights(w_in_a, w_out_a, w_kv, w_q, w_o, w_gate_up, w_down):
    return dict(
        w_in=tuple((w_in_a, 0, t * D_MODEL) for t in range(3)),
        w_out=(w_out_a, 0, 0),
        w_qkv=((w_q, 0, 0), (w_kv, 0, 0), (w_kv, 0, D_MODEL)),
        w_o=(w_o, 0, 0),
        w_gate_up=tuple(tuple((w_gate_up, l, t * D_FF) for t in range(2)) for l in range(DEPTH)),
        w_down=tuple((w_down, l, 0) for l in range(DEPTH)),
    )


def _trunk(x, hist, seq_len, cache, bias_tables, kv_rows, w, conv_w, ln_g, ln_b, *, emit):
    wb = dict(w_gate_up=[None] * DEPTH, w_down=[None] * DEPTH)

    def ffn(x, xb, layer):
        more = layer + 1 < DEPTH
        x, xb, wb["w_gate_up"][layer], down = _ffn_fused(
            x, xb, w["w_gate_up"][layer], w["w_down"][layer], ln_g, ln_b, layer, tm=TM,
            tf=_tf_fused(emit), out_scale=ALPHA if more else 1.0, want_bf16=more, emit=emit)
        wb["w_down"][layer] = down[0] if emit else None
        return x, xb

    def proj(a, name, ln_idx, prescaled):
        if emit:
            x_new, xb_new, (wb[name],) = _mm_res_ln(a, w[name], x, ln_g, ln_b, ln_idx, tm=TM,
                                                    tk=TK_OUT, emit=True, prescaled=prescaled)
            return x_new, xb_new
        return _proj_res_ln(a, w[name], x, ln_g, ln_b, ln_idx, tm=TM_PROJ, prescaled=prescaled)

    g, conv_state, wb["w_in"] = _conv_gate(x, w["w_in"], hist, conv_w, seq_len=seq_len,
                                           tm=TM, tn=_tn_three(emit), emit=emit)
    x, xb = proj(g, "w_out", (0, 0), prescaled=False)
    x, xb = ffn(x, xb, 0)

    bias_heads, bias_pair = bias_tables
    q, kb, vb, k, v, wb["w_qkv"] = _qkv(xb, w["w_qkv"], tm=TM, tn=_tn_three(emit),
                                        kv_rows=kv_rows, transposed_v=cache is None, emit=emit)
    if cache is None:
        att = _attn_prompt(q, kb, vb, bias_pair)
    else:
        att = _attn_sample(q, kb, vb, cache[0], cache[1], bias_heads)
    x, xb = proj(att, "w_o", (1, 0), prescaled=True)
    x, _ = ffn(x, xb, 1)
    return (x, conv_state, k, v), wb


def kernel(x_prompt, x_sample, state_conv, cache_k, cache_v, w_in_a, conv_w, w_out_a, w_kv, w_q,
           w_o, rel_bias, ln_g, ln_b, w_gate_up, w_down):
    batch, seq, d = x_prompt.shape
    dec_batch, dec_seq, _ = x_sample.shape
    assert batch == 1 and DEPTH == 2 and dec_seq == CHUNK and cache_k.shape[1] == LEFT_ROWS
    assert dec_batch * dec_seq == TM
    bias_tables = _bias_tables(rel_bias[0])
    w_f32 = _f32_weights(w_in_a, w_out_a, w_kv, w_q, w_o, w_gate_up, w_down)

    (y_s, conv_s, k_s, v_s), w_bf16 = _trunk(
        x_sample.reshape(dec_batch * dec_seq, d), state_conv[0], dec_seq, (cache_k, cache_v),
        bias_tables, dec_batch * dec_seq, w_f32, conv_w, ln_g, ln_b, emit=True)
    conv_zero = jnp.zeros((batch, CONV_W - 1, d), x_prompt.dtype)
    (y_p, conv_p, k_p, v_p), _ = _trunk(
        x_prompt.reshape(seq, d), conv_zero, seq, None, bias_tables, LEFT_ROWS,
        w_bf16, conv_w, ln_g, ln_b, emit=False)

    kv_prompt = (batch, LEFT_ROWS, N_HEADS, HEAD_DIM)
    kv_sample = (dec_batch, dec_seq, N_HEADS, HEAD_DIM)
    return (y_p.reshape(batch, seq, d),
            y_s.reshape(dec_batch, dec_seq, d),
            conv_p.reshape(1, batch, CONV_W - 1, d),
            conv_s.reshape(1, dec_batch, CONV_W - 1, d),
            k_p.reshape(kv_prompt),
            v_p.reshape(kv_prompt),
            k_s.reshape(kv_sample),
            v_s.reshape(kv_sample))
```

```python
import functools

import jax
import jax.numpy as jnp
from jax import lax
from jax.experimental import pallas as pl
from jax.experimental.pallas import tpu as pltpu

D_MODEL = 2048
DEPTH = 2
CHUNK = 64
N_LEFT_CHUNKS = 8
LEFT_ROWS = N_LEFT_CHUNKS * CHUNK
BAND = LEFT_ROWS + CHUNK
N_HEADS = 16
HEAD_DIM = D_MODEL // N_HEADS
MAX_REL = 256
N_REL = 2 * MAX_REL + 1
CONV_W = 3
D_FF = -(-8 * D_MODEL // (3 * 256)) * 256
ALPHA = (2.0 * DEPTH) ** 0.25
LN_EPS = 1e-5
SCALE = HEAD_DIM ** -0.5
LOG2E = 1.4426950408889634
SCALE_LOG2 = SCALE * LOG2E

LANES = 128
VMEM_LIMIT_BYTES = 56 * 1024 * 1024

PAIR = 2 * CHUNK
PAIR_BAND = BAND + CHUNK
N_REL_PAD = -(-N_REL // LANES) * LANES
T_PAD = -(-(PAIR_BAND + PAIR) // LANES) * LANES

BF16 = jnp.bfloat16
F32 = jnp.float32


def _params(n_axes):
    return pltpu.CompilerParams(dimension_semantics=("arbitrary",) * n_axes,
                                vmem_limit_bytes=VMEM_LIMIT_BYTES)


def _dot(a, b):
    return jnp.dot(a, b, preferred_element_type=F32)


def _dot_t(a, b):
    return lax.dot_general(a, b, (((1,), (1,)), ((), ())), preferred_element_type=F32)


def _wspec(weight, blk, imap):
    arr, layer, first_col = weight
    first = first_col // blk[1]
    if arr.ndim == 3:
        return pl.BlockSpec((None,) + blk, lambda *g: (layer,) + _shift(imap(*g), first))
    return pl.BlockSpec(blk, lambda *g: _shift(imap(*g), first))


def _shift(idx, first):
    return (idx[0], idx[1] + first)


def _wload(w_ref, wb_ref):
    w = w_ref[...]
    if w.dtype != BF16:
        w = w.astype(BF16)
    if wb_ref is not None:
        wb_ref[...] = w
    return w


def _dot_side_by_side(x, w_refs, wb_refs):
    tn = w_refs[0].shape[1]
    w = jnp.concatenate([_wload(w_ref, wb_ref) for w_ref, wb_ref in zip(w_refs, wb_refs)], axis=1)
    y = _dot(x, w)
    return [y[:, t * tn:(t + 1) * tn] for t in range(len(w_refs))]


def _wb_outputs(weights, emit, blk, imap, shape):
    if not emit:
        return [], []
    return ([pl.BlockSpec(blk, imap)] * len(weights),
            [jax.ShapeDtypeStruct(shape, BF16)] * len(weights))


def _as_weights(arrs):
    return tuple((a, 0, 0) for a in arrs)


def _conv_gate_kernel(x_ref, wb_ref, wc_ref, wh_ref, hist_ref, cw_ref, g_ref, state_ref, *rest,
                      nseq, emit, carried):
    rest = list(rest)
    wbo = [rest.pop(0) for _ in range(3)] if emit else [None] * 3
    xb_ref = rest.pop(0)
    i = pl.program_id(0)
    j = pl.program_id(1)
    tm, tn = g_ref.shape
    ls = tm // nseq

    @pl.when(j == 0)
    def _():
        xb_ref[...] = x_ref[...].astype(BF16)

    b, c, h = _dot_side_by_side(xb_ref[...], (wb_ref, wc_ref, wh_ref), wbo)
    u = c * h

    if carried:
        carry_ref = rest.pop(0)

        @pl.when(i == 0)
        def _():
            carry_ref[j] = hist_ref[0]

        prev = carry_ref[j][None]
    else:
        prev = hist_ref[...]

    shape3 = (nseq, ls, tn)
    pos = lax.broadcasted_iota(jnp.int32, shape3, 1)
    u3 = u.reshape(shape3)
    p1 = pltpu.roll(u, 1, 0).reshape(shape3)
    p2 = pltpu.roll(u, 2, 0).reshape(shape3)
    h0 = prev[:, 0:1, :]
    h1 = prev[:, 1:2, :]
    p1 = jnp.where(pos == 0, h1, p1)
    p2 = jnp.where(pos == 0, h0, jnp.where(pos == 1, h1, p2))
    conv = cw_ref[0:1, :] * p2 + cw_ref[1:2, :] * p1 + cw_ref[2:3, :] * u3
    g_ref[...] = (b * conv.reshape(tm, tn)).astype(BF16)

    new_state = u3[:, ls - 2:ls, :]
    state_ref[...] = new_state
    if carried:
        carry_ref[j] = new_state[0]


def _conv_gate(x, weights, hist, conv_w, *, seq_len, tm, tn, emit):
    m, d = x.shape
    nj = d // tn
    carried = seq_len > tm
    nseq = 1 if carried else tm // seq_len
    assert not emit or m == tm
    hist_map = (lambda i, j: (0, 0, j)) if carried else (lambda i, j: (i, 0, j))
    scratch = [pltpu.VMEM((tm, d), BF16)]
    if carried:
        scratch.append(pltpu.VMEM((nj, CONV_W - 1, tn), F32))
    wmap = lambda i, j: (0, j)
    wb_specs, wb_shapes = _wb_outputs(weights, emit, (d, tn), wmap, (d, d))
    n_states = (m // tm) * nseq
    g, states, *wb = pl.pallas_call(
        functools.partial(_conv_gate_kernel, nseq=nseq, emit=emit, carried=carried),
        grid=(m // tm, nj),
        in_specs=[pl.BlockSpec((tm, d), lambda i, j: (i, 0))]
        + [_wspec(w, (d, tn), wmap) for w in weights]
        + [pl.BlockSpec((nseq, CONV_W - 1, tn), hist_map),
           pl.BlockSpec((None, CONV_W, tn), lambda i, j: (0, 0, j))],
        out_specs=[
            pl.BlockSpec((tm, tn), lambda i, j: (i, j)),
            pl.BlockSpec((nseq, CONV_W - 1, tn), lambda i, j: (i, 0, j)),
        ] + wb_specs,
        out_shape=[
            jax.ShapeDtypeStruct((m, d), BF16),
            jax.ShapeDtypeStruct((n_states, CONV_W - 1, d), F32),
        ] + wb_shapes,
        scratch_shapes=scratch,
        compiler_params=_params(2),
        name="conv_gate",
    )(x, *[w[0] for w in weights], hist, conv_w)
    return g, states[-hist.shape[0]:], _as_weights(wb)


LN_ROWS = 64


def _layer_norm_rows(y, gain, bias, out_ref, outb_ref, out_scale=1.0):
    for r in range(0, y.shape[0], LN_ROWS):
        rows = slice(r, r + LN_ROWS)
        yr = y[rows, :]
        mu = jnp.mean(yr, axis=-1, keepdims=True)
        yc = yr - mu
        var = jnp.mean(yc * yc, axis=-1, keepdims=True)
        z = yc * lax.rsqrt(var + LN_EPS) * gain + bias
        out_ref[rows, :] = z if out_scale == 1.0 else z * out_scale
        if outb_ref is not None:
            outb_ref[rows, :] = z.astype(BF16)


def _mm_res_ln_kernel(a_ref, w_ref, x_ref, g_ref, b_ref, out_ref, outb_ref, *wbo, prescaled):
    k = pl.program_id(1)

    @pl.when(k == 0)
    def _():
        out_ref[...] = x_ref[...] if prescaled else ALPHA * x_ref[...]

    if len(a_ref.shape) == 3:
        a = jnp.concatenate([a_ref[h] for h in range(a_ref.shape[0])], axis=1)
    else:
        a = a_ref[...]
    out_ref[...] += _dot(a, _wload(w_ref, wbo[0] if wbo else None))

    @pl.when(k == pl.num_programs(1) - 1)
    def _():
        _layer_norm_rows(out_ref, g_ref[...], b_ref[...], out_ref, outb_ref, out_scale=ALPHA)


def _mm_res_ln(a, weight, x, ln_g, ln_b, ln_idx, *, tm, tk, emit, prescaled):
    m, d = x.shape
    kdim = weight[0].shape[-2]
    assert not emit or m == tm
    if a.ndim == 3:
        a_spec = pl.BlockSpec((tk // HEAD_DIM, tm, HEAD_DIM), lambda i, k: (k, i, 0))
    else:
        a_spec = pl.BlockSpec((tm, tk), lambda i, k: (i, k))
    wmap = lambda i, k: (k, 0)
    wb_specs, wb_shapes = _wb_outputs((weight,), emit, (tk, d), wmap, (kdim, d))
    layer, sub = ln_idx
    ln_spec = pl.BlockSpec((None, None, 1, d), lambda i, k: (layer, sub, 0, 0))
    x_new, xb_new, *wb = pl.pallas_call(
        functools.partial(_mm_res_ln_kernel, prescaled=prescaled),
        grid=(m // tm, kdim // tk),
        in_specs=[
            a_spec,
            _wspec(weight, (tk, d), wmap),
            pl.BlockSpec((tm, d), lambda i, k: (i, 0)),
            ln_spec,
            ln_spec,
        ],
        out_specs=[
            pl.BlockSpec((tm, d), lambda i, k: (i, 0)),
            pl.BlockSpec((tm, d), lambda i, k: (i, 0)),
        ] + wb_specs,
        out_shape=[
            jax.ShapeDtypeStruct((m, d), F32),
            jax.ShapeDtypeStruct((m, d), BF16),
        ] + wb_shapes,
        compiler_params=_params(2),
        name="mm_res_ln",
    )(a, weight[0], x, ln_g.reshape(DEPTH, 2, 1, d), ln_b.reshape(DEPTH, 2, 1, d))
    return x_new, xb_new, _as_weights(wb)


def _proj_res_ln_kernel(a_ref, w_ref, x_ref, g_ref, b_ref, out_ref, outb_ref, *, prescaled):
    if len(a_ref.shape) == 3:
        a = jnp.concatenate([a_ref[h] for h in range(a_ref.shape[0])], axis=1)
    else:
        a = a_ref[...]
    residual = x_ref[...] if prescaled else ALPHA * x_ref[...]
    y = residual + _dot(a, w_ref[...])
    _layer_norm_rows(y, g_ref[...], b_ref[...], out_ref, outb_ref, out_scale=ALPHA)


def _proj_res_ln(a, weight, x, ln_g, ln_b, ln_idx, *, tm, prescaled):
    m, d = x.shape
    w = weight[0]
    assert w.shape == (d, d) and w.dtype == BF16
    if a.ndim == 3:
        a_spec = pl.BlockSpec((N_HEADS, tm, HEAD_DIM), lambda i: (0, i, 0))
    else:
        a_spec = pl.BlockSpec((tm, d), lambda i: (i, 0))
    layer, sub = ln_idx
    ln_spec = pl.BlockSpec((None, None, 1, d), lambda i: (layer, sub, 0, 0))
    return pl.pallas_call(
        functools.partial(_proj_res_ln_kernel, prescaled=prescaled),
        grid=(m // tm,),
        in_specs=[
            a_spec,
            pl.BlockSpec((d, d), lambda i: (0, 0)),
            pl.BlockSpec((tm, d), lambda i: (i, 0)),
            ln_spec,
            ln_spec,
        ],
        out_specs=[
            pl.BlockSpec((tm, d), lambda i: (i, 0)),
            pl.BlockSpec((tm, d), lambda i: (i, 0)),
        ],
        out_shape=[
            jax.ShapeDtypeStruct((m, d), F32),
            jax.ShapeDtypeStruct((m, d), BF16),
        ],
        compiler_params=_params(1),
        name="proj_res_ln",
    )(a, w, x, ln_g.reshape(DEPTH, 2, 1, d), ln_b.reshape(DEPTH, 2, 1, d))


SWAP_STEP = 2
RING = 3


def _ffn_kernel(xs_hbm, xb_ref, wg_ref, wu_ref, wd_ref, g_ref, b_ref, *rest,
                n_tiles, out_scale, want_bf16, emit, ring):
    rest = list(rest)
    y_hbm = rest.pop(0)
    yb_hbm = rest.pop(0) if want_bf16 else None
    wbo = [rest.pop(0) for _ in range(3)] if emit else [None] * 3
    acc = rest.pop(0)
    ybbuf = rest.pop(0) if want_bf16 else None
    sem_x, sem_y = rest.pop(0), rest.pop(0)
    sem_yb = rest.pop(0) if want_bf16 else None
    i = pl.program_id(0)
    j = pl.program_id(1)
    last_j = pl.num_programs(1) - 1
    n_slots, tm = acc.shape[:2]
    slot = i % n_slots
    other = (i + 1) % n_slots

    if ring is not None:
        layer, col_g, col_u = ring
        ring_g, ring_u, ring_d, sem_w = rest
        tf = ring_g.shape[2]

        def w_copies(step):
            s = step % RING
            cols_g = pl.ds(pl.multiple_of(col_g + step * tf, tf), tf)
            cols_u = pl.ds(pl.multiple_of(col_u + step * tf, tf), tf)
            rows_d = pl.ds(pl.multiple_of(step * tf, tf), tf)
            return [pltpu.make_async_copy(wg_ref.at[layer, :, cols_g], ring_g.at[s], sem_w.at[s]),
                    pltpu.make_async_copy(wu_ref.at[layer, :, cols_u], ring_u.at[s], sem_w.at[s]),
                    pltpu.make_async_copy(wd_ref.at[layer, rows_d, :], ring_d.at[s], sem_w.at[s])]

        @pl.when(j == 0)
        def _():
            for step in range(RING - 1):
                for cp in w_copies(step):
                    cp.start()

        @pl.when(j + RING - 1 <= last_j)
        def _():
            for cp in w_copies(j + RING - 1):
                cp.start()

        for cp in w_copies(j):
            cp.wait()
        wg_ref, wu_ref, wd_ref = (r.at[j % RING] for r in (ring_g, ring_u, ring_d))
    tf = wg_ref.shape[1]

    def x_copy(tile, to_slot):
        return pltpu.make_async_copy(xs_hbm.at[pl.ds(tile * tm, tm), :], acc.at[to_slot],
                                     sem_x.at[to_slot])

    def y_copies(tile, from_slot):
        rows = pl.ds(tile * tm, tm)
        copies = [pltpu.make_async_copy(acc.at[from_slot], y_hbm.at[rows, :], sem_y.at[from_slot])]
        if want_bf16:
            copies.append(pltpu.make_async_copy(ybbuf, yb_hbm.at[rows, :], sem_yb))
        return copies

    @pl.when((i == 0) & (j == 0))
    def _():
        x_copy(0, 0).start()

    @pl.when(j == 0)
    def _():
        x_copy(i, slot).wait()

    w_gate_up = jnp.concatenate([_wload(wg_ref, wbo[0]), _wload(wu_ref, wbo[1])], axis=1)
    gu = _dot(xb_ref[...], w_gate_up)
    h = (jax.nn.silu(gu[:, :tf]) * gu[:, tf:]).astype(BF16)
    acc[slot] += _dot(h, _wload(wd_ref, wbo[2]))

    @pl.when((j == SWAP_STEP) & (i >= 1))
    def _():
        for cp in y_copies(i - 1, other):
            cp.wait()

    @pl.when((j == SWAP_STEP) & (i + 1 < n_tiles))
    def _():
        x_copy(i + 1, other).start()

    @pl.when(j == last_j)
    def _():
        _layer_norm_rows(acc.at[slot], g_ref[...], b_ref[...], acc.at[slot], ybbuf, out_scale)
        for cp in y_copies(i, slot):
            cp.start()

    @pl.when((j == last_j) & (i == n_tiles - 1))
    def _():
        for cp in y_copies(i, slot):
            cp.wait()


def _ffn_fused(xs, xb, w_gate_up, w_down, ln_g, ln_b, layer, *, tm, tf, out_scale, want_bf16,
               emit):
    m, d = xs.shape
    nf = D_FF // tf
    n_tiles = m // tm
    n_slots = min(2, n_tiles)
    assert nf > SWAP_STEP and (not emit or n_tiles == 1)
    ln_spec = pl.BlockSpec((None, None, 1, d), lambda i, j: (layer, 1, 0, 0))
    any_spec = pl.BlockSpec(memory_space=pl.ANY)
    n_out = 2 if want_bf16 else 1
    up_map = lambda i, j: (0, j)
    down_map = lambda i, j: (j, 0)
    up_specs, up_shapes = _wb_outputs(w_gate_up, emit, (d, tf), up_map, (d, D_FF))
    down_specs, down_shapes = _wb_outputs((w_down,), emit, (tf, d), down_map, (D_FF, d))
    if emit:
        (arr_g, w_layer, col_g), (arr_u, _, col_u) = w_gate_up
        assert arr_g.ndim == 3 and arr_u is arr_g and w_down[0].ndim == 3 and nf >= RING
        ring = (w_layer, col_g, col_u)
        w_specs = [any_spec] * 3
        ring_scratch = [pltpu.VMEM((RING, d, tf), F32), pltpu.VMEM((RING, d, tf), F32),
                        pltpu.VMEM((RING, tf, d), F32), pltpu.SemaphoreType.DMA((RING,))]
    else:
        ring = None
        w_specs = ([_wspec(w, (d, tf), up_map) for w in w_gate_up]
                   + [_wspec(w_down, (tf, d), down_map)])
        ring_scratch = []
    res = pl.pallas_call(
        functools.partial(_ffn_kernel, n_tiles=n_tiles, out_scale=out_scale, want_bf16=want_bf16,
                          emit=emit, ring=ring),
        grid=(n_tiles, nf),
        in_specs=[any_spec, pl.BlockSpec((tm, d), lambda i, j: (i, 0))] + w_specs
        + [ln_spec, ln_spec],
        out_specs=[any_spec] * n_out + up_specs + down_specs,
        out_shape=[jax.ShapeDtypeStruct((m, d), F32), jax.ShapeDtypeStruct((m, d), BF16)][:n_out]
        + up_shapes + down_shapes,
        scratch_shapes=[pltpu.VMEM((n_slots, tm, d), F32)]
        + ([pltpu.VMEM((tm, d), BF16)] if want_bf16 else [])
        + [pltpu.SemaphoreType.DMA((n_slots,)), pltpu.SemaphoreType.DMA((n_slots,))]
        + ([pltpu.SemaphoreType.DMA] if want_bf16 else []) + ring_scratch,
        compiler_params=_params(2),
        name="ffn",
    )(xs, xb, *[w[0] for w in w_gate_up], w_down[0],
      ln_g.reshape(DEPTH, 2, 1, d), ln_b.reshape(DEPTH, 2, 1, d))
    wb = _as_weights(res[n_out:])
    return res[0], (res[1] if want_bf16 else None), wb[:2], wb[2:]


def _qkv_kernel(xb_ref, wq_ref, wk_ref, wv_ref, q_ref, kb_ref, vb_ref, k_hbm, v_hbm, *rest,
                transposed_v, emit):
    rest = list(rest)
    wbo = [rest.pop(0) for _ in range(3)] if emit else [None] * 3
    kstage, vstage, sem = rest
    i = pl.program_id(0)
    j = pl.program_id(1)
    nj = pl.num_programs(1)
    hb = q_ref.shape[0]
    tm = xb_ref.shape[0]
    kv_rows = kstage.shape[1]

    q, k, v = _dot_side_by_side(xb_ref[...], (wq_ref, wk_ref, wv_ref), wbo)

    def kv_copies(step):
        slot = step % 2
        return [pltpu.make_async_copy(stage.at[slot, :, hh * HEAD_DIM:(hh + 1) * HEAD_DIM],
                                      out.at[:, step * hb + hh, :], sem.at[slot])
                for stage, out in ((kstage, k_hbm), (vstage, v_hbm)) for hh in range(hb)]

    @pl.when(i == pl.num_programs(0) - 1)
    def _():
        @pl.when(j >= 2)
        def _():
            for cp in kv_copies(j - 2):
                cp.wait()

        kstage[j % 2] = k[tm - kv_rows:, :]
        vstage[j % 2] = v[tm - kv_rows:, :]
        for cp in kv_copies(j):
            cp.start()

        @pl.when(j == nj - 1)
        def _():
            for cp in kv_copies(j - 1) + kv_copies(j):
                cp.wait()

    for hh in range(hb):
        cols = slice(hh * HEAD_DIM, (hh + 1) * HEAD_DIM)
        q_ref[hh] = (q[:, cols] * SCALE_LOG2).astype(BF16)
        kb_ref[hh] = k[:, cols].astype(BF16)
        if transposed_v:
            vb_ref[hh] = v[:, cols].T.astype(BF16)
        else:
            vb_ref[hh] = v[:, cols].astype(BF16)


def _qkv(xb, weights, *, tm, tn, kv_rows, transposed_v, emit):
    m, d = xb.shape
    nj = d // tn
    hb = tn // HEAD_DIM
    assert (not emit or m == tm) and kv_rows <= tm and nj >= 2
    kv_spec = pl.BlockSpec(memory_space=pl.ANY)
    kv_shape = jax.ShapeDtypeStruct((kv_rows, N_HEADS, HEAD_DIM), F32)
    hm_spec = pl.BlockSpec((hb, tm, HEAD_DIM), lambda i, j: (j, i, 0))
    hm_shape = jax.ShapeDtypeStruct((N_HEADS, m, HEAD_DIM), BF16)
    if transposed_v:
        v_spec = pl.BlockSpec((hb, HEAD_DIM, tm), lambda i, j: (j, 0, i))
        v_shape = jax.ShapeDtypeStruct((N_HEADS, HEAD_DIM, m), BF16)
    else:
        v_spec, v_shape = hm_spec, hm_shape
    wmap = lambda i, j: (0, j)
    wb_specs, wb_shapes = _wb_outputs(weights, emit, (d, tn), wmap, (d, d))
    q, kb, vb, k, v, *wb = pl.pallas_call(
        functools.partial(_qkv_kernel, transposed_v=transposed_v, emit=emit),
        grid=(m // tm, nj),
        in_specs=[pl.BlockSpec((tm, d), lambda i, j: (i, 0))]
        + [_wspec(w, (d, tn), wmap) for w in weights],
        out_specs=[hm_spec, hm_spec, v_spec, kv_spec, kv_spec] + wb_specs,
        out_shape=[hm_shape, hm_shape, v_shape, kv_shape, kv_shape] + wb_shapes,
        scratch_shapes=[pltpu.VMEM((2, kv_rows, tn), F32), pltpu.VMEM((2, kv_rows, tn), F32),
                        pltpu.SemaphoreType.DMA((2,))],
        compiler_params=_params(2),
        name="qkv",
    )(xb, *[w[0] for w in weights])
    return q, kb, vb, k, v, _as_weights(wb)


def _bias_kernel(rb_ref, heads_ref, pair_ref):
    rb = rb_ref[...]
    hi = rb.astype(BF16)
    r1 = rb - hi.astype(F32)
    mid = r1.astype(BF16)
    lo = (r1 - mid.astype(F32)).astype(BF16)
    src = lax.broadcasted_iota(jnp.int32, (N_REL_PAD, T_PAD), 0)
    m = lax.broadcasted_iota(jnp.int32, (N_REL_PAD, T_PAD), 1)
    idx = jnp.clip(LEFT_ROWS + CHUNK - 1 - m, -MAX_REL, MAX_REL) + MAX_REL
    onehot = (src == idx).astype(BF16)
    t = (_dot(hi, onehot) + _dot(mid, onehot)) + _dot(lo, onehot)

    key = lax.broadcasted_iota(jnp.int32, (PAIR_BAND, PAIR), 0)
    qry = lax.broadcasted_iota(jnp.int32, (PAIR_BAND, PAIR), 1)
    in_band = ((qry < CHUNK) & (key < BAND)) | ((qry >= CHUNK) & (key >= CHUNK))
    chunk_rows = []
    for h in range(N_HEADS):
        rows = jnp.broadcast_to(t[h:h + 1, :], (PAIR, T_PAD))
        rows = pltpu.roll(rows, T_PAD - (CHUNK - 1), 1, stride=1, stride_axis=0)
        chunk_rows.append(rows[:CHUNK, :PAIR_BAND])
        pair_ref[h] = jnp.where(in_band, rows[:, :PAIR_BAND].T * LOG2E, -jnp.inf)
    for hp in range(N_HEADS // 2):
        both = jnp.concatenate(chunk_rows[2 * hp:2 * hp + 2], axis=0)
        heads_ref[hp] = both.T[:BAND] * LOG2E


def _bias_tables(rel_bias):
    rb = jnp.pad(rel_bias, ((0, 0), (0, N_REL_PAD - N_REL)))
    return pl.pallas_call(
        _bias_kernel,
        out_shape=[jax.ShapeDtypeStruct((N_HEADS // 2, BAND, 2 * CHUNK), F32),
                   jax.ShapeDtypeStruct((N_HEADS, PAIR_BAND, PAIR), F32)],
        name="rel_bias_tables",
    )(rb)


PAIRS = LEFT_ROWS // PAIR


def _attn_prompt_kernel(q_ref, kp_ref, kc_ref, vp_ref, vc_ref, bias_ref, o_ref):
    g = pl.program_id(0)
    key = lax.broadcasted_iota(jnp.int32, (PAIR_BAND, PAIR), 0)

    def head(h, first):
        bias = bias_ref[h]
        for p in range(PAIRS):
            rows = slice(p * PAIR, (p + 1) * PAIR)
            lo = slice(p * PAIR, LEFT_ROWS)
            hi = slice(0, (p + 1) * PAIR)
            kband = jnp.concatenate([kp_ref[h, lo, :], kc_ref[h, hi, :]], axis=0)
            vband = jnp.concatenate([vp_ref[h, :, lo], vc_ref[h, :, hi]], axis=1)
            s = _dot_t(kband, q_ref[h, rows, :]) + bias
            if first:
                s = jnp.where(key >= LEFT_ROWS - p * PAIR, s, -jnp.inf)
            e = jnp.exp2(s - jnp.max(s, axis=0, keepdims=True))
            denom = jnp.sum(e, axis=0, keepdims=True)
            o = _dot(vband, e.astype(BF16)) * (1.0 / denom)
            o_ref[h, rows, :] = o.T.astype(BF16)

    def heads(first):
        def body(h, carry):
            head(h, first)
            return carry
        lax.fori_loop(0, N_HEADS, body, 0, unroll=True)

    pl.when(g == 0)(functools.partial(heads, True))
    pl.when(g > 0)(functools.partial(heads, False))


def _attn_prompt(q, kb, vt, bias):
    _, m, _ = q.shape
    blk = (N_HEADS, LEFT_ROWS, HEAD_DIM)
    cur = pl.BlockSpec(blk, lambda g: (0, g, 0))
    prev = pl.BlockSpec(blk, lambda g: (0, jnp.maximum(g - 1, 0), 0))
    blk_t = (N_HEADS, HEAD_DIM, LEFT_ROWS)
    cur_t = pl.BlockSpec(blk_t, lambda g: (0, 0, g))
    prev_t = pl.BlockSpec(blk_t, lambda g: (0, 0, jnp.maximum(g - 1, 0)))
    return pl.pallas_call(
        _attn_prompt_kernel,
        grid=(m // LEFT_ROWS,),
        in_specs=[cur, prev, cur, prev_t, cur_t,
                  pl.BlockSpec((N_HEADS, PAIR_BAND, PAIR), lambda g: (0, 0, 0))],
        out_specs=cur,
        out_shape=jax.ShapeDtypeStruct(q.shape, BF16),
        compiler_params=_params(1),
        name="attn_prompt",
    )(q, kb, kb, vt, vt, bias)


def _cache_copies(ck_hbm, cv_hbm, kbuf, vbuf, sem, stream, slot):
    return [pltpu.make_async_copy(src.at[stream, :, h, :],
                                  buf.at[slot, :, h * HEAD_DIM:(h + 1) * HEAD_DIM],
                                  sem.at[slot])
            for src, buf in ((ck_hbm, kbuf), (cv_hbm, vbuf)) for h in range(N_HEADS)]


def _attn_sample_kernel(q_ref, kc_ref, vc_ref, ck_hbm, cv_hbm, bias_ref, o_ref, kbuf, vbuf, sem):
    n = pl.program_id(0)
    slot = n % 2
    copies = functools.partial(_cache_copies, ck_hbm, cv_hbm, kbuf, vbuf, sem)

    @pl.when(n == 0)
    def _():
        for cp in copies(0, 0):
            cp.start()

    @pl.when(n + 1 < pl.num_programs(0))
    def _():
        for cp in copies(n + 1, 1 - slot):
            cp.start()

    for cp in copies(n, slot):
        cp.wait()

    zeros = jnp.zeros((CHUNK, HEAD_DIM), BF16)
    for hp in range(N_HEADS // 2):
        h0, h1 = 2 * hp, 2 * hp + 1
        cols = slice(h0 * HEAD_DIM, (h1 + 1) * HEAD_DIM)
        knew = jnp.concatenate([kc_ref[h0], kc_ref[h1]], axis=1)
        vnew = jnp.concatenate([vc_ref[h0], vc_ref[h1]], axis=1)
        kband = jnp.concatenate([kbuf[slot, :, cols].astype(BF16), knew], axis=0)
        vband = jnp.concatenate([vbuf[slot, :, cols].astype(BF16), vnew], axis=0)
        qbd = jnp.concatenate([jnp.concatenate([q_ref[h0], zeros], axis=1),
                               jnp.concatenate([zeros, q_ref[h1]], axis=1)], axis=0)
        s = _dot_t(kband, qbd) + bias_ref[hp]
        e = jnp.exp2(s - jnp.max(s, axis=0, keepdims=True))
        denom = jnp.sum(e, axis=0, keepdims=True)
        o = lax.dot_general(vband, e.astype(BF16), (((0,), (0,)), ((), ())),
                            preferred_element_type=F32) * (1.0 / denom)
        o_ref[h0] = o[:HEAD_DIM].T[:CHUNK].astype(BF16)
        o_ref[h1] = o[HEAD_DIM:].T[CHUNK:].astype(BF16)


def _attn_sample(q, kb, vb, cache_k, cache_v, bias):
    n = cache_k.shape[0]
    new = pl.BlockSpec((N_HEADS, CHUNK, HEAD_DIM), lambda s: (0, s, 0))
    cache = pl.BlockSpec(memory_space=pl.ANY)
    return pl.pallas_call(
        _attn_sample_kernel,
        grid=(n,),
        in_specs=[new, new, new, cache, cache,
                  pl.BlockSpec((N_HEADS // 2, BAND, 2 * CHUNK), lambda s: (0, 0, 0))],
        out_specs=new,
        out_shape=jax.ShapeDtypeStruct(q.shape, BF16),
        scratch_shapes=[pltpu.VMEM((2, LEFT_ROWS, D_MODEL), F32),
                        pltpu.VMEM((2, LEFT_ROWS, D_MODEL), F32),
                        pltpu.SemaphoreType.DMA((2,))],
        compiler_params=_params(1),
        name="attn_sample",
    )(q, kb, vb, cache_k, cache_v, bias)


TM = 1024
TK_OUT = 512
TM_PROJ = 512


def _tn_three(emit):
    return 256 if emit else 512


def _tf_fused(emit):
    return 256 if emit else 512


def _f32_weights(w_in_a, w_out_a, w_kv, w_q, w_o, w_gate_up, w_down):
    return dict(
        w_in=tuple((w_in_a, 0, t * D_MODEL) for t in range(3)),
        w_out=(w_out_a, 0, 0),
        w_qkv=((w_q, 0, 0), (w_kv, 0, 0), (w_kv, 0, D_MODEL)),
        w_o=(w_o, 0, 0),
        w_gate_up=tuple(tuple((w_gate_up, l, t * D_FF) for t in range(2)) for l in range(DEPTH)),
        w_down=tuple((w_down, l, 0) for l in range(DEPTH)),
    )


def _trunk(x, hist, seq_len, cache, bias_tables, kv_rows, w, conv_w, ln_g, ln_b, *, emit):
    wb = dict(w_gate_up=[None] * DEPTH, w_down=[None] * DEPTH)

    def ffn(x, xb, layer):
        more = layer + 1 < DEPTH
        x, xb, wb["w_gate_up"][layer], down = _ffn_fused(
            x, xb, w["w_gate_up"][layer], w["w_down"][layer], ln_g, ln_b, layer, tm=TM,
            tf=_tf_fused(emit), out_scale=ALPHA if more else 1.0, want_bf16=more, emit=emit)
        wb["w_down"][layer] = down[0] if emit else None
        return x, xb

    def proj(a, name, ln_idx, prescaled):
        if emit:
            x_new, xb_new, (wb[name],) = _mm_res_ln(a, w[name], x, ln_g, ln_b, ln_idx, tm=TM,
                                                    tk=TK_OUT, emit=True, prescaled=prescaled)
            return x_new, xb_new
        return _proj_res_ln(a, w[name], x, ln_g, ln_b, ln_idx, tm=TM_PROJ, prescaled=prescaled)

    g, conv_state, wb["w_in"] = _conv_gate(x, w["w_in"], hist, conv_w, seq_len=seq_len,
                                           tm=TM, tn=_tn_three(emit), emit=emit)
    x, xb = proj(g, "w_out", (0, 0), prescaled=False)
    x, xb = ffn(x, xb, 0)

    bias_heads, bias_pair = bias_tables
    q, kb, vb, k, v, wb["w_qkv"] = _qkv(xb, w["w_qkv"], tm=TM, tn=_tn_three(emit),
                                        kv_rows=kv_rows, transposed_v=cache is None, emit=emit)
    if cache is None:
        att = _attn_prompt(q, kb, vb, bias_pair)
    else:
        att = _attn_sample(q, kb, vb, cache[0], cache[1], bias_heads)
    x, xb = proj(att, "w_o", (1, 0), prescaled=True)
    x, _ = ffn(x, xb, 1)
    return (x, conv_state, k, v), wb


def kernel(x_prompt, x_sample, state_conv, cache_k, cache_v, w_in_a, conv_w, w_out_a, w_kv, w_q,
           w_o, rel_bias, ln_g, ln_b, w_gate_up, w_down):
    batch, seq, d = x_prompt.shape
    dec_batch, dec_seq, _ = x_sample.shape
    assert batch == 1 and DEPTH == 2 and dec_seq == CHUNK and cache_k.shape[1] == LEFT_ROWS
    assert dec_batch * dec_seq == TM
    bias_tables = _bias_tables(rel_bias[0])
    w_f32 = _f32_weights(w_in_a, w_out_a, w_kv, w_q, w_o, w_gate_up, w_down)

    (y_s, conv_s, k_s, v_s), w_bf16 = _trunk(
        x_sample.reshape(dec_batch * dec_seq, d), state_conv[0], dec_seq, (cache_k, cache_v),
        bias_tables, dec_batch * dec_seq, w_f32, conv_w, ln_g, ln_b, emit=True)
    conv_zero = jnp.zeros((batch, CONV_W - 1, d), x_prompt.dtype)
    (y_p, conv_p, k_p, v_p), _ = _trunk(
        x_prompt.reshape(seq, d), conv_zero, seq, None, bias_tables, LEFT_ROWS,
        w_bf16, conv_w, ln_g, ln_b, emit=False)

    kv_prompt = (batch, LEFT_ROWS, N_HEADS, HEAD_DIM)
    kv_sample = (dec_batch, dec_seq, N_HEADS, HEAD_DIM)
    return (y_p.reshape(batch, seq, d),
            y_s.reshape(dec_batch, dec_seq, d),
            conv_p.reshape(1, batch, CONV_W - 1, d),
            conv_s.reshape(1, dec_batch, CONV_W - 1, d),
            k_p.reshape(kv_prompt),
            v_p.reshape(kv_prompt),
            k_s.reshape(kv_sample),
            v_s.reshape(kv_sample))
```

```python
import functools

import jax
import jax.numpy as jnp
from jax import lax
from jax.experimental import pallas as pl
from jax.experimental.pallas import tpu as pltpu

D_MODEL = 2048
DEPTH = 2
CHUNK = 64
N_LEFT_CHUNKS = 8
LEFT_ROWS = N_LEFT_CHUNKS * CHUNK
BAND = LEFT_ROWS + CHUNK
N_HEADS = 16
HEAD_DIM = D_MODEL // N_HEADS
MAX_REL = 256
N_REL = 2 * MAX_REL + 1
CONV_W = 3
D_FF = -(-8 * D_MODEL // (3 * 256)) * 256
ALPHA = (2.0 * DEPTH) ** 0.25
LN_EPS = 1e-5
SCALE = HEAD_DIM ** -0.5
LOG2E = 1.4426950408889634
SCALE_LOG2 = SCALE * LOG2E

LANES = 128
VMEM_LIMIT_BYTES = 56 * 1024 * 1024

PAIR = 2 * CHUNK
PAIR_BAND = BAND + CHUNK
N_REL_PAD = -(-N_REL // LANES) * LANES
T_PAD = -(-(PAIR_BAND + PAIR) // LANES) * LANES

BF16 = jnp.bfloat16
F32 = jnp.float32


def _params(n_axes):
    return pltpu.CompilerParams(dimension_semantics=("arbitrary",) * n_axes,
                                vmem_limit_bytes=VMEM_LIMIT_BYTES)


def _dot(a, b):
    return jnp.dot(a, b, preferred_element_type=F32)


def _dot_t(a, b):
    return lax.dot_general(a, b, (((1,), (1,)), ((), ())), preferred_element_type=F32)


def _wspec(weight, blk, imap):
    arr, layer, first_col = weight
    first = first_col // blk[1]
    if arr.ndim == 3:
        return pl.BlockSpec((None,) + blk, lambda *g: (layer,) + _shift(imap(*g), first))
    return pl.BlockSpec(blk, lambda *g: _shift(imap(*g), first))


def _shift(idx, first):
    return (idx[0], idx[1] + first)


def _wload(w_ref, wb_ref):
    w = w_ref[...]
    if w.dtype != BF16:
        w = w.astype(BF16)
    if wb_ref is not None:
        wb_ref[...] = w
    return w


def _dot_side_by_side(x, w_refs, wb_refs):
    tn = w_refs[0].shape[1]
    w = jnp.concatenate([_wload(w_ref, wb_ref) for w_ref, wb_ref in zip(w_refs, wb_refs)], axis=1)
    y = _dot(x, w)
    return [y[:, t * tn:(t + 1) * tn] for t in range(len(w_refs))]


def _wb_outputs(weights, emit, blk, imap, shape):
    if not emit:
        return [], []
    return ([pl.BlockSpec(blk, imap)] * len(weights),
            [jax.ShapeDtypeStruct(shape, BF16)] * len(weights))


def _as_weights(arrs):
    return tuple((a, 0, 0) for a in arrs)


def _conv_gate_kernel(x_ref, wb_ref, wc_ref, wh_ref, hist_ref, cw_ref, g_ref, state_ref, *rest,
                      nseq, emit, carried):
    rest = list(rest)
    wbo = [rest.pop(0) for _ in range(3)] if emit else [None] * 3
    xb_ref = rest.pop(0)
    i = pl.program_id(0)
    j = pl.program_id(1)
    tm, tn = g_ref.shape
    ls = tm // nseq

    @pl.when(j == 0)
    def _():
        xb_ref[...] = x_ref[...].astype(BF16)

    b, c, h = _dot_side_by_side(xb_ref[...], (wb_ref, wc_ref, wh_ref), wbo)
    u = c * h

    if carried:
        carry_ref = rest.pop(0)

        @pl.when(i == 0)
        def _():
            carry_ref[j] = hist_ref[0]

        prev = carry_ref[j][None]
    else:
        prev = hist_ref[...]

    shape3 = (nseq, ls, tn)
    pos = lax.broadcasted_iota(jnp.int32, shape3, 1)
    u3 = u.reshape(shape3)
    p1 = pltpu.roll(u, 1, 0).reshape(shape3)
    p2 = pltpu.roll(u, 2, 0).reshape(shape3)
    h0 = prev[:, 0:1, :]
    h1 = prev[:, 1:2, :]
    p1 = jnp.where(pos == 0, h1, p1)
    p2 = jnp.where(pos == 0, h0, jnp.where(pos == 1, h1, p2))
    conv = cw_ref[0:1, :] * p2 + cw_ref[1:2, :] * p1 + cw_ref[2:3, :] * u3
    g_ref[...] = (b * conv.reshape(tm, tn)).astype(BF16)

    new_state = u3[:, ls - 2:ls, :]
    state_ref[...] = new_state
    if carried:
        carry_ref[j] = new_state[0]


def _conv_gate(x, weights, hist, conv_w, *, seq_len, tm, tn, emit):
    m, d = x.shape
    nj = d // tn
    carried = seq_len > tm
    nseq = 1 if carried else tm // seq_len
    assert not emit or m == tm
    hist_map = (lambda i, j: (0, 0, j)) if carried else (lambda i, j: (i, 0, j))
    scratch = [pltpu.VMEM((tm, d), BF16)]
    if carried:
        scratch.append(pltpu.VMEM((nj, CONV_W - 1, tn), F32))
    wmap = lambda i, j: (0, j)
    wb_specs, wb_shapes = _wb_outputs(weights, emit, (d, tn), wmap, (d, d))
    n_states = (m // tm) * nseq
    g, states, *wb = pl.pallas_call(
        functools.partial(_conv_gate_kernel, nseq=nseq, emit=emit, carried=carried),
        grid=(m // tm, nj),
        in_specs=[pl.BlockSpec((tm, d), lambda i, j: (i, 0))]
        + [_wspec(w, (d, tn), wmap) for w in weights]
        + [pl.BlockSpec((nseq, CONV_W - 1, tn), hist_map),
           pl.BlockSpec((None, CONV_W, tn), lambda i, j: (0, 0, j))],
        out_specs=[
            pl.BlockSpec((tm, tn), lambda i, j: (i, j)),
            pl.BlockSpec((nseq, CONV_W - 1, tn), lambda i, j: (i, 0, j)),
        ] + wb_specs,
        out_shape=[
            jax.ShapeDtypeStruct((m, d), BF16),
            jax.ShapeDtypeStruct((n_states, CONV_W - 1, d), F32),
        ] + wb_shapes,
        scratch_shapes=scratch,
        compiler_params=_params(2),
        name="conv_gate",
    )(x, *[w[0] for w in weights], hist, conv_w)
    return g, states[-hist.shape[0]:], _as_weights(wb)


LN_ROWS = 64


def _layer_norm_rows(y, gain, bias, out_ref, outb_ref, out_scale=1.0):
    for r in range(0, y.shape[0], LN_ROWS):
        rows = slice(r, r + LN_ROWS)
        yr = y[rows, :]
        mu = jnp.mean(yr, axis=-1, keepdims=True)
        yc = yr - mu
        var = jnp.mean(yc * yc, axis=-1, keepdims=True)
        z = yc * lax.rsqrt(var + LN_EPS) * gain + bias
        out_ref[rows, :] = z if out_scale == 1.0 else z * out_scale
        if outb_ref is not None:
            outb_ref[rows, :] = z.astype(BF16)


def _mm_res_ln_kernel(a_ref, w_ref, x_ref, g_ref, b_ref, out_ref, outb_ref, *wbo, prescaled):
    k = pl.program_id(1)

    @pl.when(k == 0)
    def _():
        out_ref[...] = x_ref[...] if prescaled else ALPHA * x_ref[...]

    if len(a_ref.shape) == 3:
        a = jnp.concatenate([a_ref[h] for h in range(a_ref.shape[0])], axis=1)
    else:
        a = a_ref[...]
    out_ref[...] += _dot(a, _wload(w_ref, wbo[0] if wbo else None))

    @pl.when(k == pl.num_programs(1) - 1)
    def _():
        _layer_norm_rows(out_ref, g_ref[...], b_ref[...], out_ref, outb_ref, out_scale=ALPHA)


def _mm_res_ln(a, weight, x, ln_g, ln_b, ln_idx, *, tm, tk, emit, prescaled):
    m, d = x.shape
    kdim = weight[0].shape[-2]
    assert not emit or m == tm
    if a.ndim == 3:
        a_spec = pl.BlockSpec((tk // HEAD_DIM, tm, HEAD_DIM), lambda i, k: (k, i, 0))
    else:
        a_spec = pl.BlockSpec((tm, tk), lambda i, k: (i, k))
    wmap = lambda i, k: (k, 0)
    wb_specs, wb_shapes = _wb_outputs((weight,), emit, (tk, d), wmap, (kdim, d))
    layer, sub = ln_idx
    ln_spec = pl.BlockSpec((None, None, 1, d), lambda i, k: (layer, sub, 0, 0))
    x_new, xb_new, *wb = pl.pallas_call(
        functools.partial(_mm_res_ln_kernel, prescaled=prescaled),
        grid=(m // tm, kdim // tk),
        in_specs=[
            a_spec,
            _wspec(weight, (tk, d), wmap),
            pl.BlockSpec((tm, d), lambda i, k: (i, 0)),
            ln_spec,
            ln_spec,
        ],
        out_specs=[
            pl.BlockSpec((tm, d), lambda i, k: (i, 0)),
            pl.BlockSpec((tm, d), lambda i, k: (i, 0)),
        ] + wb_specs,
        out_shape=[
            jax.ShapeDtypeStruct((m, d), F32),
            jax.ShapeDtypeStruct((m, d), BF16),
        ] + wb_shapes,
        compiler_params=_params(2),
        name="mm_res_ln",
    )(a, weight[0], x, ln_g.reshape(DEPTH, 2, 1, d), ln_b.reshape(DEPTH, 2, 1, d))
    return x_new, xb_new, _as_weights(wb)


def _proj_res_ln_kernel(a_ref, w_ref, x_ref, g_ref, b_ref, out_ref, outb_ref, *, prescaled):
    if len(a_ref.shape) == 3:
        a = jnp.concatenate([a_ref[h] for h in range(a_ref.shape[0])], axis=1)
    else:
        a = a_ref[...]
    residual = x_ref[...] if prescaled else ALPHA * x_ref[...]
    y = residual + _dot(a, w_ref[...])
    _layer_norm_rows(y, g_ref[...], b_ref[...], out_ref, outb_ref, out_scale=ALPHA)


def _proj_res_ln(a, weight, x, ln_g, ln_b, ln_idx, *, tm, prescaled):
    m, d = x.shape
    w = weight[0]
    assert w.shape == (d, d) and w.dtype == BF16
    if a.ndim == 3:
        a_spec = pl.BlockSpec((N_HEADS, tm, HEAD_DIM), lambda i: (0, i, 0))
    else:
        a_spec = pl.BlockSpec((tm, d), lambda i: (i, 0))
    layer, sub = ln_idx
    ln_spec = pl.BlockSpec((None, None, 1, d), lambda i: (layer, sub, 0, 0))
    return pl.pallas_call(
        functools.partial(_proj_res_ln_kernel, prescaled=prescaled),
        grid=(m // tm,),
        in_specs=[
            a_spec,
            pl.BlockSpec((d, d), lambda i: (0, 0)),
            pl.BlockSpec((tm, d), lambda i: (i, 0)),
            ln_spec,
            ln_spec,
        ],
        out_specs=[
            pl.BlockSpec((tm, d), lambda i: (i, 0)),
            pl.BlockSpec((tm, d), lambda i: (i, 0)),
        ],
        out_shape=[
            jax.ShapeDtypeStruct((m, d), F32),
            jax.ShapeDtypeStruct((m, d), BF16),
        ],
        compiler_params=_params(1),
        name="proj_res_ln",
    )(a, w, x, ln_g.reshape(DEPTH, 2, 1, d), ln_b.reshape(DEPTH, 2, 1, d))


SWAP_STEP = 2
RING = 3


def _ffn_kernel(xs_hbm, xb_ref, wg_ref, wu_ref, wd_ref, g_ref, b_ref, *rest,
                n_tiles, out_scale, want_bf16, emit, ring):
    rest = list(rest)
    y_hbm = rest.pop(0)
    yb_hbm = rest.pop(0) if want_bf16 else None
    wbo = [rest.pop(0) for _ in range(3)] if emit else [None] * 3
    acc = rest.pop(0)
    ybbuf = rest.pop(0) if want_bf16 else None
    sem_x, sem_y = rest.pop(0), rest.pop(0)
    sem_yb = rest.pop(0) if want_bf16 else None
    i = pl.program_id(0)
    j = pl.program_id(1)
    last_j = pl.num_programs(1) - 1
    n_slots, tm = acc.shape[:2]
    slot = i % n_slots
    other = (i + 1) % n_slots

    if ring is not None:
        layer, col_g, col_u = ring
        ring_g, ring_u, ring_d, sem_w = rest
        tf = ring_g.shape[2]

        def w_copies(step):
            s = step % RING
            cols_g = pl.ds(pl.multiple_of(col_g + step * tf, tf), tf)
            cols_u = pl.ds(pl.multiple_of(col_u + step * tf, tf), tf)
            rows_d = pl.ds(pl.multiple_of(step * tf, tf), tf)
            return [pltpu.make_async_copy(wg_ref.at[layer, :, cols_g], ring_g.at[s], sem_w.at[s]),
                    pltpu.make_async_copy(wu_ref.at[layer, :, cols_u], ring_u.at[s], sem_w.at[s]),
                    pltpu.make_async_copy(wd_ref.at[layer, rows_d, :], ring_d.at[s], sem_w.at[s])]

        @pl.when(j == 0)
        def _():
            for step in range(RING - 1):
                for cp in w_copies(step):
                    cp.start()

        @pl.when(j + RING - 1 <= last_j)
        def _():
            for cp in w_copies(j + RING - 1):
                cp.start()

        for cp in w_copies(j):
            cp.wait()
        wg_ref, wu_ref, wd_ref = (r.at[j % RING] for r in (ring_g, ring_u, ring_d))
    tf = wg_ref.shape[1]

    def x_copy(tile, to_slot):
        return pltpu.make_async_copy(xs_hbm.at[pl.ds(tile * tm, tm), :], acc.at[to_slot],
                                     sem_x.at[to_slot])

    def y_copies(tile, from_slot):
        rows = pl.ds(tile * tm, tm)
        copies = [pltpu.make_async_copy(acc.at[from_slot], y_hbm.at[rows, :], sem_y.at[from_slot])]
        if want_bf16:
            copies.append(pltpu.make_async_copy(ybbuf, yb_hbm.at[rows, :], sem_yb))
        return copies

    @pl.when((i == 0) & (j == 0))
    def _():
        x_copy(0, 0).start()

    @pl.when(j == 0)
    def _():
        x_copy(i, slot).wait()

    w_gate_up = jnp.concatenate([_wload(wg_ref, wbo[0]), _wload(wu_ref, wbo[1])], axis=1)
    gu = _dot(xb_ref[...], w_gate_up)
    h = (jax.nn.silu(gu[:, :tf]) * gu[:, tf:]).astype(BF16)
    acc[slot] += _dot(h, _wload(wd_ref, wbo[2]))

    @pl.when((j == SWAP_STEP) & (i >= 1))
    def _():
        for cp in y_copies(i - 1, other):
            cp.wait()

    @pl.when((j == SWAP_STEP) & (i + 1 < n_tiles))
    def _():
        x_copy(i + 1, other).start()

    @pl.when(j == last_j)
    def _():
        _layer_norm_rows(acc.at[slot], g_ref[...], b_ref[...], acc.at[slot], ybbuf, out_scale)
        for cp in y_copies(i, slot):
            cp.start()

    @pl.when((j == last_j) & (i == n_tiles - 1))
    def _():
        for cp in y_copies(i, slot):
            cp.wait()


def _ffn_fused(xs, xb, w_gate_up, w_down, ln_g, ln_b, layer, *, tm, tf, out_scale, want_bf16,
               emit):
    m, d = xs.shape
    nf = D_FF // tf
    n_tiles = m // tm
    n_slots = min(2, n_tiles)
    assert nf > SWAP_STEP and (not emit or n_tiles == 1)
    ln_spec = pl.BlockSpec((None, None, 1, d), lambda i, j: (layer, 1, 0, 0))
    any_spec = pl.BlockSpec(memory_space=pl.ANY)
    n_out = 2 if want_bf16 else 1
    up_map = lambda i, j: (0, j)
    down_map = lambda i, j: (j, 0)
    up_specs, up_shapes = _wb_outputs(w_gate_up, emit, (d, tf), up_map, (d, D_FF))
    down_specs, down_shapes = _wb_outputs((w_down,), emit, (tf, d), down_map, (D_FF, d))
    if emit:
        (arr_g, w_layer, col_g), (arr_u, _, col_u) = w_gate_up
        assert arr_g.ndim == 3 and arr_u is arr_g and w_down[0].ndim == 3 and nf >= RING
        ring = (w_layer, col_g, col_u)
        w_specs = [any_spec] * 3
        ring_scratch = [pltpu.VMEM((RING, d, tf), F32), pltpu.VMEM((RING, d, tf), F32),
                        pltpu.VMEM((RING, tf, d), F32), pltpu.SemaphoreType.DMA((RING,))]
    else:
        ring = None
        w_specs = ([_wspec(w, (d, tf), up_map) for w in w_gate_up]
                   + [_wspec(w_down, (tf, d), down_map)])
        ring_scratch = []
    res = pl.pallas_call(
        functools.partial(_ffn_kernel, n_tiles=n_tiles, out_scale=out_scale, want_bf16=want_bf16,
                          emit=emit, ring=ring),
        grid=(n_tiles, nf),
        in_specs=[any_spec, pl.BlockSpec((tm, d), lambda i, j: (i, 0))] + w_specs
        + [ln_spec, ln_spec],
        out_specs=[any_spec] * n_out + up_specs + down_specs,
        out_shape=[jax.ShapeDtypeStruct((m, d), F32), jax.ShapeDtypeStruct((m, d), BF16)][:n_out]
        + up_shapes + down_shapes,
        scratch_shapes=[pltpu.VMEM((n_slots, tm, d), F32)]
        + ([pltpu.VMEM((tm, d), BF16)] if want_bf16 else [])
        + [pltpu.SemaphoreType.DMA((n_slots,)), pltpu.SemaphoreType.DMA((n_slots,))]
        + ([pltpu.SemaphoreType.DMA] if want_bf16 else []) + ring_scratch,
        compiler_params=_params(2),
        name="ffn",
    )(xs, xb, *[w[0] for w in w_gate_up], w_down[0],
      ln_g.reshape(DEPTH, 2, 1, d), ln_b.reshape(DEPTH, 2, 1, d))
    wb = _as_weights(res[n_out:])
    return res[0], (res[1] if want_bf16 else None), wb[:2], wb[2:]


def _qkv_kernel(xb_ref, wq_ref, wk_ref, wv_ref, q_ref, kb_ref, vb_ref, k_hbm, v_hbm, *rest,
                transposed_v, emit):
    rest = list(rest)
    wbo = [rest.pop(0) for _ in range(3)] if emit else [None] * 3
    kstage, vstage, sem = rest
    i = pl.program_id(0)
    j = pl.program_id(1)
    nj = pl.num_programs(1)
    hb = q_ref.shape[0]
    tm = xb_ref.shape[0]
    kv_rows = kstage.shape[1]

    q, k, v = _dot_side_by_side(xb_ref[...], (wq_ref, wk_ref, wv_ref), wbo)

    def kv_copies(step):
        slot = step % 2
        return [pltpu.make_async_copy(stage.at[slot, :, hh * HEAD_DIM:(hh + 1) * HEAD_DIM],
                                      out.at[:, step * hb + hh, :], sem.at[slot])
                for stage, out in ((kstage, k_hbm), (vstage, v_hbm)) for hh in range(hb)]

    for hh in range(hb):
        cols = slice(hh * HEAD_DIM, (hh + 1) * HEAD_DIM)
        q_ref[hh] = (q[:, cols] * SCALE_LOG2).astype(BF16)
        kb_ref[hh] = k[:, cols].astype(BF16)
        if transposed_v:
            vb_ref[hh] = v[:, cols].T.astype(BF16)
        else:
            vb_ref[hh] = v[:, cols].astype(BF16)

    @pl.when(i == pl.num_programs(0) - 1)
    def _():
        @pl.when(j >= 2)
        def _():
            for cp in kv_copies(j - 2):
                cp.wait()

        kstage[j % 2] = k[tm - kv_rows:, :]
        vstage[j % 2] = v[tm - kv_rows:, :]
        for cp in kv_copies(j):
            cp.start()

        @pl.when(j == nj - 1)
        def _():
            for cp in kv_copies(j - 1) + kv_copies(j):
                cp.wait()


def _qkv(xb, weights, *, tm, tn, kv_rows, transposed_v, emit):
    m, d = xb.shape
    nj = d // tn
    hb = tn // HEAD_DIM
    assert (not emit or m == tm) and kv_rows <= tm and nj >= 2
    kv_spec = pl.BlockSpec(memory_space=pl.ANY)
    kv_shape = jax.ShapeDtypeStruct((kv_rows, N_HEADS, HEAD_DIM), F32)
    hm_spec = pl.BlockSpec((hb, tm, HEAD_DIM), lambda i, j: (j, i, 0))
    hm_shape = jax.ShapeDtypeStruct((N_HEADS, m, HEAD_DIM), BF16)
    if transposed_v:
        v_spec = pl.BlockSpec((hb, HEAD_DIM, tm), lambda i, j: (j, 0, i))
        v_shape = jax.ShapeDtypeStruct((N_HEADS, HEAD_DIM, m), BF16)
    else:
        v_spec, v_shape = hm_spec, hm_shape
    wmap = lambda i, j: (0, j)
    wb_specs, wb_shapes = _wb_outputs(weights, emit, (d, tn), wmap, (d, d))
    q, kb, vb, k, v, *wb = pl.pallas_call(
        functools.partial(_qkv_kernel, transposed_v=transposed_v, emit=emit),
        grid=(m // tm, nj),
        in_specs=[pl.BlockSpec((tm, d), lambda i, j: (i, 0))]
        + [_wspec(w, (d, tn), wmap) for w in weights],
        out_specs=[hm_spec, hm_spec, v_spec, kv_spec, kv_spec] + wb_specs,
        out_shape=[hm_shape, hm_shape, v_shape, kv_shape, kv_shape] + wb_shapes,
        scratch_shapes=[pltpu.VMEM((2, kv_rows, tn), F32), pltpu.VMEM((2, kv_rows, tn), F32),
                        pltpu.SemaphoreType.DMA((2,))],
        compiler_params=_params(2),
        name="qkv",
    )(xb, *[w[0] for w in weights])
    return q, kb, vb, k, v, _as_weights(wb)


def _bias_kernel(rb_ref, heads_ref, pair_ref):
    rb = rb_ref[...]
    hi = rb.astype(BF16)
    r1 = rb - hi.astype(F32)
    mid = r1.astype(BF16)
    lo = (r1 - mid.astype(F32)).astype(BF16)
    src = lax.broadcasted_iota(jnp.int32, (N_REL_PAD, T_PAD), 0)
    m = lax.broadcasted_iota(jnp.int32, (N_REL_PAD, T_PAD), 1)
    idx = jnp.clip(LEFT_ROWS + CHUNK - 1 - m, -MAX_REL, MAX_REL) + MAX_REL
    onehot = (src == idx).astype(BF16)
    t = (_dot(hi, onehot) + _dot(mid, onehot)) + _dot(lo, onehot)

    key = lax.broadcasted_iota(jnp.int32, (PAIR_BAND, PAIR), 0)
    qry = lax.broadcasted_iota(jnp.int32, (PAIR_BAND, PAIR), 1)
    in_band = ((qry < CHUNK) & (key < BAND)) | ((qry >= CHUNK) & (key >= CHUNK))
    chunk_rows = []
    for h in range(N_HEADS):
        rows = jnp.broadcast_to(t[h:h + 1, :], (PAIR, T_PAD))
        rows = pltpu.roll(rows, T_PAD - (CHUNK - 1), 1, stride=1, stride_axis=0)
        chunk_rows.append(rows[:CHUNK, :PAIR_BAND])
        pair_ref[h] = jnp.where(in_band, rows[:, :PAIR_BAND].T * LOG2E, -jnp.inf)
    for hp in range(N_HEADS // 2):
        both = jnp.concatenate(chunk_rows[2 * hp:2 * hp + 2], axis=0)
        heads_ref[hp] = both.T[:BAND] * LOG2E


def _bias_tables(rel_bias):
    rb = jnp.pad(rel_bias, ((0, 0), (0, N_REL_PAD - N_REL)))
    return pl.pallas_call(
        _bias_kernel,
        out_shape=[jax.ShapeDtypeStruct((N_HEADS // 2, BAND, 2 * CHUNK), F32),
                   jax.ShapeDtypeStruct((N_HEADS, PAIR_BAND, PAIR), F32)],
        name="rel_bias_tables",
    )(rb)


PAIRS = LEFT_ROWS // PAIR


def _attn_prompt_kernel(q_ref, kp_ref, kc_ref, vp_ref, vc_ref, bias_ref, o_ref):
    g = pl.program_id(0)
    key = lax.broadcasted_iota(jnp.int32, (PAIR_BAND, PAIR), 0)

    def head(h, first):
        bias = bias_ref[h]
        for p in range(PAIRS):
            rows = slice(p * PAIR, (p + 1) * PAIR)
            lo = slice(p * PAIR, LEFT_ROWS)
            hi = slice(0, (p + 1) * PAIR)
            kband = jnp.concatenate([kp_ref[h, lo, :], kc_ref[h, hi, :]], axis=0)
            vband = jnp.concatenate([vp_ref[h, :, lo], vc_ref[h, :, hi]], axis=1)
            s = _dot_t(kband, q_ref[h, rows, :]) + bias
            if first:
                s = jnp.where(key >= LEFT_ROWS - p * PAIR, s, -jnp.inf)
            e = jnp.exp2(s - jnp.max(s, axis=0, keepdims=True))
            denom = jnp.sum(e, axis=0, keepdims=True)
            o = _dot(vband, e.astype(BF16)) * (1.0 / denom)
            o_ref[h, rows, :] = o.T.astype(BF16)

    def heads(first):
        def body(h, carry):
            head(h, first)
            return carry
        lax.fori_loop(0, N_HEADS, body, 0, unroll=True)

    pl.when(g == 0)(functools.partial(heads, True))
    pl.when(g > 0)(functools.partial(heads, False))


def _attn_prompt(q, kb, vt, bias):
    _, m, _ = q.shape
    blk = (N_HEADS, LEFT_ROWS, HEAD_DIM)
    cur = pl.BlockSpec(blk, lambda g: (0, g, 0))
    prev = pl.BlockSpec(blk, lambda g: (0, jnp.maximum(g - 1, 0), 0))
    blk_t = (N_HEADS, HEAD_DIM, LEFT_ROWS)
    cur_t = pl.BlockSpec(blk_t, lambda g: (0, 0, g))
    prev_t = pl.BlockSpec(blk_t, lambda g: (0, 0, jnp.maximum(g - 1, 0)))
    return pl.pallas_call(
        _attn_prompt_kernel,
        grid=(m // LEFT_ROWS,),
        in_specs=[cur, prev, cur, prev_t, cur_t,
                  pl.BlockSpec((N_HEADS, PAIR_BAND, PAIR), lambda g: (0, 0, 0))],
        out_specs=cur,
        out_shape=jax.ShapeDtypeStruct(q.shape, BF16),
        compiler_params=_params(1),
        name="attn_prompt",
    )(q, kb, kb, vt, vt, bias)


def _cache_copies(ck_hbm, cv_hbm, kbuf, vbuf, sem, stream, slot):
    return [pltpu.make_async_copy(src.at[stream, :, h, :],
                                  buf.at[slot, :, h * HEAD_DIM:(h + 1) * HEAD_DIM],
                                  sem.at[slot])
            for src, buf in ((ck_hbm, kbuf), (cv_hbm, vbuf)) for h in range(N_HEADS)]


def _attn_sample_kernel(q_ref, kc_ref, vc_ref, ck_hbm, cv_hbm, bias_ref, o_ref, kbuf, vbuf, sem):
    n = pl.program_id(0)
    slot = n % 2
    copies = functools.partial(_cache_copies, ck_hbm, cv_hbm, kbuf, vbuf, sem)

    @pl.when(n == 0)
    def _():
        for cp in copies(0, 0):
            cp.start()

    @pl.when(n + 1 < pl.num_programs(0))
    def _():
        for cp in copies(n + 1, 1 - slot):
            cp.start()

    for cp in copies(n, slot):
        cp.wait()

    zeros = jnp.zeros((CHUNK, HEAD_DIM), BF16)
    for hp in range(N_HEADS // 2):
        h0, h1 = 2 * hp, 2 * hp + 1
        cols = slice(h0 * HEAD_DIM, (h1 + 1) * HEAD_DIM)
        knew = jnp.concatenate([kc_ref[h0], kc_ref[h1]], axis=1)
        vnew = jnp.concatenate([vc_ref[h0], vc_ref[h1]], axis=1)
        kband = jnp.concatenate([kbuf[slot, :, cols].astype(BF16), knew], axis=0)
        vband = jnp.concatenate([vbuf[slot, :, cols].astype(BF16), vnew], axis=0)
        qbd = jnp.concatenate([jnp.concatenate([q_ref[h0], zeros], axis=1),
                               jnp.concatenate([zeros, q_ref[h1]], axis=1)], axis=0)
        s = _dot_t(kband, qbd) + bias_ref[hp]
        e = jnp.exp2(s - jnp.max(s, axis=0, keepdims=True))
        denom = jnp.sum(e, axis=0, keepdims=True)
        o = lax.dot_general(vband, e.astype(BF16), (((0,), (0,)), ((), ())),
                            preferred_element_type=F32) * (1.0 / denom)
        o_ref[h0] = o[:HEAD_DIM].T[:CHUNK].astype(BF16)
        o_ref[h1] = o[HEAD_DIM:].T[CHUNK:].astype(BF16)


def _attn_sample(q, kb, vb, cache_k, cache_v, bias):
    n = cache_k.shape[0]
    new = pl.BlockSpec((N_HEADS, CHUNK, HEAD_DIM), lambda s: (0, s, 0))
    cache = pl.BlockSpec(memory_space=pl.ANY)
    return pl.pallas_call(
        _attn_sample_kernel,
        grid=(n,),
        in_specs=[new, new, new, cache, cache,
                  pl.BlockSpec((N_HEADS // 2, BAND, 2 * CHUNK), lambda s: (0, 0, 0))],
        out_specs=new,
        out_shape=jax.ShapeDtypeStruct(q.shape, BF16),
        scratch_shapes=[pltpu.VMEM((2, LEFT_ROWS, D_MODEL), F32),
                        pltpu.VMEM((2, LEFT_ROWS, D_MODEL), F32),
                        pltpu.SemaphoreType.DMA((2,))],
        compiler_params=_params(1),
        name="attn_sample",
    )(q, kb, vb, cache_k, cache_v, bias)


TM = 1024
TK_OUT = 512
TM_PROJ = 512


def _tn_three(emit):
    return 256 if emit else 512


def _tf_fused(emit):
    return 256 if emit else 512


def _f32_weights(w_in_a, w_out_a, w_kv, w_q, w_o, w_gate_up, w_down):
    return dict(
        w_in=tuple((w_in_a, 0, t * D_MODEL) for t in range(3)),
        w_out=(w_out_a, 0, 0),
        w_qkv=((w_q, 0, 0), (w_kv, 0, 0), (w_kv, 0, D_MODEL)),
        w_o=(w_o, 0, 0),
        w_gate_up=tuple(tuple((w_gate_up, l, t * D_FF) for t in range(2)) for l in range(DEPTH)),
        w_down=tuple((w_down, l, 0) for l in range(DEPTH)),
    )


def _trunk(x, hist, seq_len, cache, bias_tables, kv_rows, w, conv_w, ln_g, ln_b, *, emit):
    wb = dict(w_gate_up=[None] * DEPTH, w_down=[None] * DEPTH)

    def ffn(x, xb, layer):
        more = layer + 1 < DEPTH
        x, xb, wb["w_gate_up"][layer], down = _ffn_fused(
            x, xb, w["w_gate_up"][layer], w["w_down"][layer], ln_g, ln_b, layer, tm=TM,
            tf=_tf_fused(emit), out_scale=ALPHA if more else 1.0, want_bf16=more, emit=emit)
        wb["w_down"][layer] = down[0] if emit else None
        return x, xb

    def proj(a, name, ln_idx, prescaled):
        if emit:
            x_new, xb_new, (wb[name],) = _mm_res_ln(a, w[name], x, ln_g, ln_b, ln_idx, tm=TM,
                                                    tk=TK_OUT, emit=True, prescaled=prescaled)
            return x_new, xb_new
        return _proj_res_ln(a, w[name], x, ln_g, ln_b, ln_idx, tm=TM_PROJ, prescaled=prescaled)

    g, conv_state, wb["w_in"] = _conv_gate(x, w["w_in"], hist, conv_w, seq_len=seq_len,
                                           tm=TM, tn=_tn_three(emit), emit=emit)
    x, xb = proj(g, "w_out", (0, 0), prescaled=False)
    x, xb = ffn(x, xb, 0)

    bias_heads, bias_pair = bias_tables
    q, kb, vb, k, v, wb["w_qkv"] = _qkv(xb, w["w_qkv"], tm=TM, tn=_tn_three(emit),
                                        kv_rows=kv_rows, transposed_v=cache is None, emit=emit)
    if cache is None:
        att = _attn_prompt(q, kb, vb, bias_pair)
    else:
        att = _attn_sample(q, kb, vb, cache[0], cache[1], bias_heads)
    x, xb = proj(att, "w_o", (1, 0), prescaled=True)
    x, _ = ffn(x, xb, 1)
    return (x, conv_state, k, v), wb


def kernel(x_prompt, x_sample, state_conv, cache_k, cache_v, w_in_a, conv_w, w_out_a, w_kv, w_q,
           w_o, rel_bias, ln_g, ln_b, w_gate_up, w_down):
    batch, seq, d = x_prompt.shape
    dec_batch, dec_seq, _ = x_sample.shape
    assert batch == 1 and DEPTH == 2 and dec_seq == CHUNK and cache_k.shape[1] == LEFT_ROWS
    assert dec_batch * dec_seq == TM
    bias_tables = _bias_tables(rel_bias[0])
    w_f32 = _f32_weights(w_in_a, w_out_a, w_kv, w_q, w_o, w_gate_up, w_down)

    (y_s, conv_s, k_s, v_s), w_bf16 = _trunk(
        x_sample.reshape(dec_batch * dec_seq, d), state_conv[0], dec_seq, (cache_k, cache_v),
        bias_tables, dec_batch * dec_seq, w_f32, conv_w, ln_g, ln_b, emit=True)
    conv_zero = jnp.zeros((batch, CONV_W - 1, d), x_prompt.dtype)
    (y_p, conv_p, k_p, v_p), _ = _trunk(
        x_prompt.reshape(seq, d), conv_zero, seq, None, bias_tables, LEFT_ROWS,
        w_bf16, conv_w, ln_g, ln_b, emit=False)

    kv_prompt = (batch, LEFT_ROWS, N_HEADS, HEAD_DIM)
    kv_sample = (dec_batch, dec_seq, N_HEADS, HEAD_DIM)
    return (y_p.reshape(batch, seq, d),
            y_s.reshape(dec_batch, dec_seq, d),
            conv_p.reshape(1, batch, CONV_W - 1, d),
            conv_s.reshape(1, dec_batch, CONV_W - 1, d),
            k_p.reshape(kv_prompt),
            v_p.reshape(kv_prompt),
            k_s.reshape(kv_sample),
            v_s.reshape(kv_sample))
```
